```python
import jax
import jax.numpy as jnp
from jax import lax
import numpy as np

D_MODEL = 2048
BATCH = 4
SEQ = 8192
DEPTH = 2

N_MIXERS = 2
N_RET_LAYERS = (DEPTH + 1) // 2
N_MLA_LAYERS = DEPTH // 2
EPS = 1e-6
ROPE_BASE = 10000.0
BLOCK = 128

RET_HEADS = 8
RET_DK = D_MODEL // RET_HEADS
RET_DV = 2 * RET_DK
RET_WIDTH = RET_HEADS * RET_DV
RET_QK = RET_HEADS * RET_DK
RET_IN = 2 * RET_QK + 2 * RET_WIDTH

MLA_HEADS = 16
MLA_NOPE = 128
MLA_ROPE = 64
MLA_V = 128
MLA_Q_RANK = 512
MLA_KV_RANK = 512
MLA_WIDTH = MLA_HEADS * MLA_V
MLA_IN = MLA_Q_RANK + MLA_KV_RANK + MLA_ROPE + MLA_WIDTH

kernel_name = "hybrid_retention_mla_adaln"


def rmsnorm(x, g):
    xf = x.astype(jnp.float32)
    y = xf * lax.rsqrt(jnp.mean(xf * xf, axis=-1, keepdims=True) + EPS)
    return (y * g.astype(jnp.float32)).astype(x.dtype)


def rope(x, pos):
    d = x.shape[-1]
    inv_freq = ROPE_BASE ** (-jnp.arange(0, d, 2, dtype=jnp.float32) / d)
    ang = pos.astype(jnp.float32)[:, :, None, None] * inv_freq
    cos = jnp.cos(ang).astype(x.dtype)
    sin = jnp.sin(ang).astype(x.dtype)
    x1, x2 = x[..., : d // 2], x[..., d // 2:]
    return jnp.concatenate([x1 * cos - x2 * sin, x2 * cos + x1 * sin], axis=-1)


def chunkwise_retention(q, k, v):
    B, S, H, _ = q.shape
    nc = S // BLOCK
    log_gamma = jnp.log1p(-jnp.exp2(-5.0 - jnp.arange(H, dtype=jnp.float32)))
    idx = jnp.arange(BLOCK, dtype=jnp.float32)
    diff = idx[:, None] - idx[None, :]
    d_intra = jnp.where(diff >= 0, jnp.exp(log_gamma[:, None, None] * jnp.maximum(diff, 0.0)), 0.0)
    xi = jnp.exp(log_gamma[:, None] * (idx + 1.0))[None, :, :, None]
    zeta = jnp.exp(log_gamma[:, None] * (BLOCK - 1.0 - idx))[None, :, :, None]
    chunk_decay = jnp.exp(log_gamma * BLOCK)[None, :, None, None]

    def to_chunks(t):
        return t.astype(jnp.float32).reshape(B, nc, BLOCK, H, t.shape[-1]).transpose(1, 0, 3, 2, 4)

    def step(state, inp):
        qi, ki, vi = inp
        scores = jnp.einsum('bhnd,bhmd->bhnm', qi, ki) * d_intra
        inner = jnp.einsum('bhnm,bhmv->bhnv', scores, vi)
        cross = jnp.einsum('bhnd,bhdv->bhnv', qi, state) * xi
        state = state * chunk_decay + jnp.einsum('bhmd,bhmv->bhdv', ki, vi * zeta)
        return state, inner + cross

    state0 = jnp.zeros((B, H, q.shape[-1], v.shape[-1]), jnp.float32)
    _, out = lax.scan(step, state0, (to_chunks(q), to_chunks(k), to_chunks(v)))
    return out.transpose(1, 0, 3, 2, 4).reshape(B, S, H, v.shape[-1])


def retention_branch(h, pos, w_in, gn_g, w_out):
    B, S, _ = h.shape
    proj = h @ w_in
    q, k, v, g = jnp.split(proj, [RET_QK, 2 * RET_QK, 2 * RET_QK + RET_WIDTH], axis=-1)
    q = rope(q.reshape(B, S, RET_HEADS, RET_DK), pos)
    k = rope(k.reshape(B, S, RET_HEADS, RET_DK), pos) * (RET_DK ** -0.5)
    v = v.reshape(B, S, RET_HEADS, RET_DV)
    y = chunkwise_retention(q, k, v)
    mu = jnp.mean(y, axis=-1, keepdims=True)
    var = jnp.mean(jnp.square(y - mu), axis=-1, keepdims=True)
    y = (y - mu) * lax.rsqrt(var + EPS)
    y = (y.reshape(B, S, RET_WIDTH) * gn_g.astype(jnp.float32)).astype(h.dtype)
    return (jax.nn.silu(g) * y) @ w_out


def causal_block_attention(q, k, v):
    B, S, H, dq = q.shape
    nb = S // BLOCK
    scale = dq ** -0.5
    qb = q.reshape(B, nb, BLOCK, H, dq).transpose(1, 0, 3, 2, 4)
    kt = k.transpose(0, 2, 1, 3)
    vt = v.transpose(0, 2, 1, 3)
    key_pos = jnp.arange(S)

    def one_block(args):
        i, qi = args
        s = jnp.einsum('bhqd,bhkd->bhqk', qi, kt).astype(jnp.float32) * scale
        qpos = i * BLOCK + jnp.arange(BLOCK)
        s = jnp.where(key_pos[None, :] <= qpos[:, None], s, -1e30)
        p = jax.nn.softmax(s, axis=-1).astype(vt.dtype)
        return jnp.einsum('bhqk,bhkd->bhqd', p, vt)

    o = lax.map(one_block, (jnp.arange(nb), qb))
    return o.transpose(1, 0, 3, 2, 4).reshape(B, S, H, v.shape[-1])


def mla_branch(h, pos, w_in, q_norm_g, w_uq, kv_norm_g, w_ukv, w_out):
    B, S, _ = h.shape
    proj = h @ w_in
    cq, ckv, k_rope, g = jnp.split(
        proj, [MLA_Q_RANK, MLA_Q_RANK + MLA_KV_RANK, MLA_Q_RANK + MLA_KV_RANK + MLA_ROPE], axis=-1)
    q = (rmsnorm(cq, q_norm_g) @ w_uq).reshape(B, S, MLA_HEADS, MLA_NOPE + MLA_ROPE)
    q = jnp.concatenate([q[..., :MLA_NOPE], rope(q[..., MLA_NOPE:], pos)], axis=-1)
    kv = (rmsnorm(ckv, kv_norm_g) @ w_ukv).reshape(B, S, MLA_HEADS, MLA_NOPE + MLA_V)
    k_nope, v = kv[..., :MLA_NOPE], kv[..., MLA_NOPE:]
    k_rope = rope(k_rope[:, :, None, :], pos)
    k = jnp.concatenate([k_nope, jnp.broadcast_to(k_rope, (B, S, MLA_HEADS, MLA_ROPE))], axis=-1)
    o = causal_block_attention(q, k, v).reshape(B, S, MLA_WIDTH)
    return (jax.nn.silu(g) * o) @ w_out


def setup_inputs(seed: int = 0) -> dict:
    key = jax.random.key(seed)
    ks = jax.random.split(key, 18)

    def w(k, shape, fan_in, gain=1.0):
        return jax.random.normal(k, shape, jnp.float32) * (gain * fan_in ** -0.5)

    def gain(k, shape):
        return 1.0 + 0.05 * jax.random.normal(k, shape, jnp.float32)

    x = jax.random.normal(ks[0], (BATCH, SEQ, D_MODEL), jnp.float32)
    c = jax.random.normal(ks[1], (BATCH, D_MODEL), jnp.float32)
    start = jax.random.randint(ks[2], (BATCH, 1), 0, 1024, dtype=jnp.int32)
    positions = start + jnp.arange(SEQ, dtype=jnp.int32)[None, :]
    return {
        "x": x,
        "c": c,
        "positions": positions,
        "ada_w": w(ks[3], (DEPTH, D_MODEL, 3 * D_MODEL), D_MODEL, 0.5),
        "ada_b": 0.01 * jax.random.normal(ks[4], (DEPTH, 3 * D_MODEL), jnp.float32),
        "norm_g": gain(ks[5], (DEPTH, D_MODEL)),
        "ret_w_in": w(ks[6], (N_RET_LAYERS, D_MODEL, RET_IN), D_MODEL),
        "ret_gn_g": gain(ks[7], (N_RET_LAYERS, RET_WIDTH)),
        "ret_w_out": w(ks[8], (N_RET_LAYERS, RET_WIDTH, D_MODEL), RET_WIDTH),
        "mla_w_in": w(ks[9], (N_MLA_LAYERS, D_MODEL, MLA_IN), D_MODEL),
        "mla_q_norm_g": gain(ks[10], (N_MLA_LAYERS, MLA_Q_RANK)),
        "mla_w_uq": w(ks[11], (N_MLA_LAYERS, MLA_Q_RANK, MLA_HEADS * (MLA_NOPE + MLA_ROPE)), MLA_Q_RANK),
        "mla_kv_norm_g": gain(ks[12], (N_MLA_LAYERS, MLA_KV_RANK)),
        "mla_w_ukv": w(ks[13], (N_MLA_LAYERS, MLA_KV_RANK, MLA_HEADS * (MLA_NOPE + MLA_V)), MLA_KV_RANK),
        "mla_w_out": w(ks[14], (N_MLA_LAYERS, MLA_WIDTH, D_MODEL), MLA_WIDTH),
        "final_norm_g": gain(ks[15], (D_MODEL,)),
    }


def reference(x, c, positions, ada_w, ada_b, norm_g, ret_w_in, ret_gn_g, ret_w_out,
              mla_w_in, mla_q_norm_g, mla_w_uq, mla_kv_norm_g, mla_w_ukv, mla_w_out,
              final_norm_g):
    c_act = jax.nn.silu(c)
    for i in range(DEPTH):
        mod = (c_act @ ada_w[i] + ada_b[i])[:, None, :]
        shift, scale, gate = jnp.split(mod, 3, axis=-1)
        h = rmsnorm(x, norm_g[i]) * (1.0 + scale) + shift
        j = i // N_MIXERS
        if i % N_MIXERS == 0:
            y = retention_branch(h, positions, ret_w_in[j], ret_gn_g[j], ret_w_out[j])
        else:
            y = mla_branch(h, positions, mla_w_in[j], mla_q_norm_g[j], mla_w_uq[j],
                           mla_kv_norm_g[j], mla_w_ukv[j], mla_w_out[j])
        x = x + gate * y
    return rmsnorm(x, final_norm_g)
```

```python
import functools

import jax
import jax.numpy as jnp
from jax import lax
from jax.experimental import pallas as pl
from jax.experimental.pallas import tpu as pltpu

F32 = jnp.float32
BF16 = jnp.bfloat16

EPS = 1e-6
ROPE_BASE = 10000.0
NEG_BIG = -1e30

LANES = 128
SUBLANES = 8
VMEM_LIMIT_BYTES = 56 * 2**20

RET_HEADS = 8
RET_CHUNK = 128
MLA_HEADS = 16
MLA_NOPE = 128
MLA_ROPE = 64
MLA_V = 128
MLA_HEAD_PAD = 256


def _params(*semantics):
    return pltpu.CompilerParams(dimension_semantics=semantics,
                                vmem_limit_bytes=VMEM_LIMIT_BYTES)


def _silu(x):
    return x * jax.nn.sigmoid(x)


def _mod_kernel(c_ref, w_ref, b_ref, o_ref):
    act = _silu(c_ref[...]).astype(BF16)
    o_ref[...] = jnp.dot(act, w_ref[...].astype(BF16),
                         preferred_element_type=F32) + b_ref[...]


def _modulation(c, ada_w, ada_b):
    depth, d, n3 = ada_w.shape
    b = c.shape[0]
    assert b <= SUBLANES
    tn = 768
    assert n3 % tn == 0
    c_pad = jnp.zeros((SUBLANES, d), F32).at[:b].set(c)
    out = pl.pallas_call(
        _mod_kernel,
        grid=(depth, n3 // tn),
        in_specs=[
            pl.BlockSpec((SUBLANES, d), lambda i, n: (0, 0)),
            pl.BlockSpec((None, d, tn), lambda i, n: (i, 0, n)),
            pl.BlockSpec((None, 1, tn), lambda i, n: (i, 0, n)),
        ],
        out_specs=pl.BlockSpec((None, SUBLANES, tn), lambda i, n: (i, 0, n)),
        out_shape=jax.ShapeDtypeStruct((depth, SUBLANES, n3), F32),
        compiler_params=_params("parallel", "parallel"),
        name="adaln_modulation",
    )(c_pad, ada_w, ada_b.reshape(depth, 1, n3))
    return out[:, :b].reshape(depth, b, 1, n3)


def _rope_table_kernel(pos_ref, freq_ref, cos_ref, sin_ref):
    ang = pos_ref[...] * freq_ref[...]
    cos_ref[...] = jnp.cos(ang)
    sin_ref[...] = jnp.sin(ang)


def _rope_tables(pos_lanes, freq_lanes):
    rows = pos_lanes.shape[0]
    tr = min(rows, 1024)
    assert rows % tr == 0
    spec = pl.BlockSpec((tr, LANES), lambda i: (i, 0))
    return pl.pallas_call(
        _rope_table_kernel,
        grid=(rows // tr,),
        in_specs=[spec, pl.BlockSpec((1, LANES), lambda i: (0, 0))],
        out_specs=[spec, spec],
        out_shape=[jax.ShapeDtypeStruct((rows, LANES), F32)] * 2,
        compiler_params=_params("parallel"),
        name="rope_tables",
    )(pos_lanes, freq_lanes)


def _inv_freq(d):
    return ROPE_BASE ** (-jnp.arange(0, d, 2, dtype=F32) / d)


def _adaln_rmsnorm(x_ref, g_ref, shift_ref, scale_ref):
    x = x_ref[...]
    y = x * lax.rsqrt(jnp.mean(x * x, axis=-1, keepdims=True) + EPS)
    return (y * (g_ref[...] * (1.0 + scale_ref[...])) + shift_ref[...]).astype(BF16)


def _mod_specs(layer, rows_per_batch_block, d):
    def spec(part):
        return pl.BlockSpec((None, None, 1, d),
                            lambda i, n: (layer, i // rows_per_batch_block, 0, part))
    return spec


def _ret_inproj_kernel(x_ref, g_ref, shift_ref, scale_ref, cos_ref, sin_ref, w_ref,
                       o_ref, h_ref, *, rope_blocks, head_dim):
    n = pl.program_id(1)

    @pl.when(n == 0)
    def _():
        h_ref[...] = _adaln_rmsnorm(x_ref, g_ref, shift_ref, scale_ref)

    acc = jnp.dot(h_ref[...], w_ref[...], preferred_element_type=F32)
    half = head_dim // 2

    @pl.when(n < rope_blocks)
    def _():
        cos = cos_ref[...]
        sin = sin_ref[...]
        for c0 in range(0, acc.shape[1], head_dim):
            x1 = acc[:, c0:c0 + half]
            x2 = acc[:, c0 + half:c0 + head_dim]
            o_ref[:, c0:c0 + half] = (x1 * cos - x2 * sin).astype(BF16)
            o_ref[:, c0 + half:c0 + head_dim] = (x2 * cos + x1 * sin).astype(BF16)

    @pl.when(n >= rope_blocks)
    def _():
        o_ref[...] = acc.astype(BF16)


def _ret_inproj(x2d, norm_g, mod, layer, cos, sin, w_bf16, seq, rope_cols, head_dim):
    t, d = x2d.shape
    n_out = w_bf16.shape[1]
    tm = min(seq, 1024)
    tn = 1024
    assert seq % tm == 0 and n_out % tn == 0 and rope_cols % tn == 0 and tn % head_dim == 0
    assert head_dim // 2 == LANES
    mspec = _mod_specs(layer, seq // tm, d)
    return pl.pallas_call(
        functools.partial(_ret_inproj_kernel, rope_blocks=rope_cols // tn, head_dim=head_dim),
        grid=(t // tm, n_out // tn),
        in_specs=[
            pl.BlockSpec((tm, d), lambda i, n: (i, 0)),
            pl.BlockSpec((1, d), lambda i, n: (0, 0)),
            mspec(0), mspec(1),
            pl.BlockSpec((tm, LANES), lambda i, n: (i, 0)),
            pl.BlockSpec((tm, LANES), lambda i, n: (i, 0)),
            pl.BlockSpec((d, tn), lambda i, n: (0, n)),
        ],
        out_specs=pl.BlockSpec((tm, tn), lambda i, n: (i, n)),
        out_shape=jax.ShapeDtypeStruct((t, n_out), BF16),
        scratch_shapes=[pltpu.VMEM((tm, d), BF16)],
        compiler_params=_params("parallel", "arbitrary"),
        name="ret_in_proj",
    )(x2d, norm_g.reshape(1, d), mod, mod, cos, sin, w_bf16)


def _retention_kernel(decay_ref, q_ref, k_ref, v_ref, g_ref, gn_ref, dmat_ref, xi_ref,
                      zeta_ref, o_ref, state_ref, *, chunk):
    @pl.when(pl.program_id(2) == 0)
    def _():
        state_ref[...] = jnp.zeros_like(state_ref)

    decay = decay_ref[pl.program_id(1)]
    n_chunks = q_ref.shape[0] // chunk

    def body(c, carry):
        rows = pl.ds(pl.multiple_of(c * chunk, chunk), chunk)
        q = q_ref[rows, :]
        k = k_ref[rows, :]
        v = v_ref[rows, :]
        s = lax.dot_general(q, k, (((1,), (1,)), ((), ())), preferred_element_type=F32)
        s = (s * dmat_ref[...]).astype(BF16)
        state = state_ref[...]
        y = (jnp.dot(s, v, preferred_element_type=F32)
             + jnp.dot(q, state.astype(BF16), preferred_element_type=F32) * xi_ref[...])
        vz = (v.astype(F32) * zeta_ref[...]).astype(BF16)
        state_ref[...] = state * decay + lax.dot_general(
            k, vz, (((0,), (0,)), ((), ())), preferred_element_type=F32)
        yc = y - jnp.mean(y, axis=-1, keepdims=True)
        var = jnp.mean(yc * yc, axis=-1, keepdims=True)
        yn = yc * lax.rsqrt(var + EPS) * gn_ref[...]
        o_ref[rows, :] = (_silu(g_ref[rows, :].astype(F32)) * yn).astype(BF16)
        return carry

    lax.fori_loop(0, n_chunks, body, 0)


def _retention_constants(heads, chunk, dk, dv):
    log_gamma = jnp.log1p(-jnp.exp2(-5.0 - jnp.arange(heads, dtype=F32)))
    idx = jnp.arange(chunk, dtype=F32)
    diff = idx[:, None] - idx[None, :]
    k_scale = dk ** -0.5
    d_intra = jnp.where(diff >= 0,
                        jnp.exp(log_gamma[:, None, None] * jnp.maximum(diff, 0.0)), 0.0) * k_scale
    xi = jnp.exp(log_gamma[:, None] * (idx + 1.0))
    zeta = jnp.exp(log_gamma[:, None] * (chunk - 1.0 - idx)) * k_scale
    chunk_decay = jnp.exp(log_gamma * chunk)
    bcast = lambda a: jnp.broadcast_to(a[:, :, None], (heads, chunk, dv))
    return d_intra, bcast(xi), bcast(zeta), chunk_decay


def _retention(proj, gn_g, batch, seq, heads, dk, dv):
    t = proj.shape[0]
    width = heads * dv
    chunk = RET_CHUNK
    sb = min(seq, 1024)
    assert seq % sb == 0 and sb % chunk == 0
    nsb = seq // sb
    d_intra, xi, zeta, chunk_decay = _retention_constants(heads, chunk, dk, dv)
    k_blk0 = heads
    v_blk0 = 2 * heads * dk // dv
    g_blk0 = v_blk0 + heads
    row = lambda b, h, s: b * nsb + s
    return pl.pallas_call(
        functools.partial(_retention_kernel, chunk=chunk),
        grid=(batch, heads, nsb),
        in_specs=[
            pl.BlockSpec(memory_space=pltpu.SMEM),
            pl.BlockSpec((sb, dk), lambda b, h, s: (row(b, h, s), h)),
            pl.BlockSpec((sb, dk), lambda b, h, s: (row(b, h, s), k_blk0 + h)),
            pl.BlockSpec((sb, dv), lambda b, h, s: (row(b, h, s), v_blk0 + h)),
            pl.BlockSpec((sb, dv), lambda b, h, s: (row(b, h, s), g_blk0 + h)),
            pl.BlockSpec((1, dv), lambda b, h, s: (0, h)),
            pl.BlockSpec((None, chunk, chunk), lambda b, h, s: (h, 0, 0)),
            pl.BlockSpec((None, chunk, dv), lambda b, h, s: (h, 0, 0)),
            pl.BlockSpec((None, chunk, dv), lambda b, h, s: (h, 0, 0)),
        ],
        out_specs=pl.BlockSpec((sb, dv), lambda b, h, s: (row(b, h, s), h)),
        out_shape=jax.ShapeDtypeStruct((t, width), BF16),
        scratch_shapes=[pltpu.VMEM((dk, dv), F32)],
        compiler_params=_params("parallel", "parallel", "arbitrary"),
        name="retention",
    )(chunk_decay, proj, proj, proj, proj, gn_g.reshape(1, width), d_intra, xi, zeta)


def _outproj_kernel(a_ref, w_ref, x_ref, gate_ref, o_ref):
    y = jnp.dot(a_ref[...], w_ref[...], preferred_element_type=F32)
    o_ref[...] = x_ref[...] + gate_ref[...] * y


def _outproj_final_kernel(a_ref, w_ref, x_ref, gate_ref, fg_ref, o_ref):
    y = jnp.dot(a_ref[...], w_ref[...], preferred_element_type=F32)
    z = x_ref[...] + gate_ref[...] * y
    o_ref[...] = z * lax.rsqrt(jnp.mean(z * z, axis=-1, keepdims=True) + EPS) * fg_ref[...]


def _outproj(a, w_bf16, x2d, mod, layer, seq, final_g=None):
    t, k = a.shape
    d = w_bf16.shape[1]
    tm = min(seq, 512)
    tn = d if final_g is not None else min(d, 1024)
    assert seq % tm == 0 and d % tn == 0
    gate_part0 = 2 * (d // tn)
    in_specs = [
        pl.BlockSpec((tm, k), lambda i, n: (i, 0)),
        pl.BlockSpec((k, tn), lambda i, n: (0, n)),
        pl.BlockSpec((tm, tn), lambda i, n: (i, n)),
        pl.BlockSpec((None, None, 1, tn),
                     lambda i, n: (layer, i // (seq // tm), 0, gate_part0 + n)),
    ]
    args = [a, w_bf16, x2d, mod]
    kern = _outproj_kernel
    if final_g is not None:
        in_specs.append(pl.BlockSpec((1, d), lambda i, n: (0, 0)))
        args.append(final_g.reshape(1, d))
        kern = _outproj_final_kernel
    return pl.pallas_call(
        kern,
        grid=(t // tm, d // tn),
        in_specs=in_specs,
        out_specs=pl.BlockSpec((tm, tn), lambda i, n: (i, n)),
        out_shape=jax.ShapeDtypeStruct((t, d), F32),
        compiler_params=_params("parallel", "arbitrary"),
        name="out_proj_final" if final_g is not None else "out_proj",
    )(*args)


def _mla_inproj_kernel(x_ref, g_ref, shift_ref, scale_ref, w_ref, o_ref, h_ref):
    @pl.when(pl.program_id(1) == 0)
    def _():
        h_ref[...] = _adaln_rmsnorm(x_ref, g_ref, shift_ref, scale_ref)

    o_ref[...] = jnp.dot(h_ref[...], w_ref[...], preferred_element_type=F32).astype(BF16)


def _mla_inproj(x2d, norm_g, mod, layer, w_bf16, seq):
    t, d = x2d.shape
    n_out = w_bf16.shape[1]
    tm = min(seq, 1024)
    tn = 640
    assert seq % tm == 0 and n_out % tn == 0
    mspec = _mod_specs(layer, seq // tm, d)
    return pl.pallas_call(
        _mla_inproj_kernel,
        grid=(t // tm, n_out // tn),
        in_specs=[
            pl.BlockSpec((tm, d), lambda i, n: (i, 0)),
            pl.BlockSpec((1, d), lambda i, n: (0, 0)),
            mspec(0), mspec(1),
            pl.BlockSpec((d, tn), lambda i, n: (0, n)),
        ],
        out_specs=pl.BlockSpec((tm, tn), lambda i, n: (i, n)),
        out_shape=jax.ShapeDtypeStruct((t, n_out), BF16),
        scratch_shapes=[pltpu.VMEM((tm, d), BF16)],
        compiler_params=_params("parallel", "arbitrary"),
        name="mla_in_proj",
    )(x2d, norm_g.reshape(1, d), mod, mod, w_bf16)


def _mla_up_kernel(cq_ref, ckv_ref, kr_ref, qg_ref, kvg_ref, wq_ref, wkv_ref, cos_ref,
                   sin_ref, q_ref, k_ref, v_ref, *, heads, q_scale, group):
    cos = cos_ref[...]
    sin = sin_ref[...]

    def rope_pad(x):
        return x * cos + pltpu.roll(x, LANES // 2, axis=1) * sin

    def rms(x_ref, g_ref):
        x = x_ref[...].astype(F32)
        return (x * lax.rsqrt(jnp.mean(x * x, axis=-1, keepdims=True) + EPS)
                * g_ref[...]).astype(BF16)

    cqn = rms(cq_ref, qg_ref)
    ckvn = rms(ckv_ref, kvg_ref)
    kr = rope_pad(kr_ref[...].astype(F32)).astype(BF16)
    gw = group * MLA_HEAD_PAD
    for g0 in range(0, heads, group):
        q = jnp.dot(cqn, wq_ref[:, g0 * MLA_HEAD_PAD:g0 * MLA_HEAD_PAD + gw],
                    preferred_element_type=F32)
        for j in range(group):
            c0 = j * MLA_HEAD_PAD
            o0 = (g0 + j) * MLA_HEAD_PAD
            q_ref[:, o0:o0 + LANES] = (q[:, c0:c0 + LANES] * q_scale).astype(BF16)
            q_ref[:, o0 + LANES:o0 + 2 * LANES] = (
                rope_pad(q[:, c0 + LANES:c0 + 2 * LANES]) * q_scale).astype(BF16)
    nope_w = heads * MLA_NOPE
    for g0 in range(0, heads, group):
        kn = jnp.dot(ckvn, wkv_ref[:, g0 * MLA_NOPE:(g0 + group) * MLA_NOPE],
                     preferred_element_type=F32)
        for j in range(group):
            o0 = (g0 + j) * MLA_HEAD_PAD
            k_ref[:, o0:o0 + LANES] = kn[:, j * MLA_NOPE:(j + 1) * MLA_NOPE].astype(BF16)
            k_ref[:, o0 + LANES:o0 + 2 * LANES] = kr
    v_ref[...] = jnp.dot(ckvn, wkv_ref[:, nope_w:], preferred_element_type=F32).astype(BF16)


def _mla_up(proj, q_norm_g, kv_norm_g, wq_bf16, wkv_bf16, cos, sin, q_rank, kv_rank,
            kr_col, heads):
    t = proj.shape[0]
    tm = min(t, 512)
    assert t % tm == 0 and q_rank == kv_rank and kr_col % LANES == 0
    return pl.pallas_call(
        functools.partial(_mla_up_kernel, heads=heads,
                          q_scale=(MLA_NOPE + MLA_ROPE) ** -0.5, group=4),
        grid=(t // tm,),
        in_specs=[
            pl.BlockSpec((tm, q_rank), lambda i: (i, 0)),
            pl.BlockSpec((tm, kv_rank), lambda i: (i, 1)),
            pl.BlockSpec((tm, LANES), lambda i: (i, kr_col // LANES)),
            pl.BlockSpec((1, q_rank), lambda i: (0, 0)),
            pl.BlockSpec((1, kv_rank), lambda i: (0, 0)),
            pl.BlockSpec(wq_bf16.shape, lambda i: (0, 0)),
            pl.BlockSpec(wkv_bf16.shape, lambda i: (0, 0)),
            pl.BlockSpec((tm, LANES), lambda i: (i, 0)),
            pl.BlockSpec((tm, LANES), lambda i: (i, 0)),
        ],
        out_specs=[
            pl.BlockSpec((tm, heads * MLA_HEAD_PAD), lambda i: (i, 0)),
            pl.BlockSpec((tm, heads * MLA_HEAD_PAD), lambda i: (i, 0)),
            pl.BlockSpec((tm, heads * MLA_V), lambda i: (i, 0)),
        ],
        out_shape=[
            jax.ShapeDtypeStruct((t, heads * MLA_HEAD_PAD), BF16),
            jax.ShapeDtypeStruct((t, heads * MLA_HEAD_PAD), BF16),
            jax.ShapeDtypeStruct((t, heads * MLA_V), BF16),
        ],
        compiler_params=_params("parallel"),
        name="mla_up_proj",
    )(proj, proj, proj, q_norm_g.reshape(1, q_rank), kv_norm_g.reshape(1, kv_rank),
      wq_bf16, wkv_bf16, cos, sin)


def _attn_kernel(q_ref, k_ref, v_ref, g_ref, o_ref, vext_ref, acc_ref, m_ref, *, blk):
    qi = pl.program_id(2)

    @pl.when(qi == 0)
    def _():
        vext_ref[:, :MLA_V] = v_ref[...]
        vext_ref[:, MLA_V:] = jnp.ones((vext_ref.shape[0], LANES), BF16)

    q = q_ref[...]
    m_ref[...] = jnp.full(m_ref.shape, NEG_BIG, F32)
    acc_ref[...] = jnp.zeros_like(acc_ref)

    def step(j, masked):
        rows = pl.ds(pl.multiple_of(j * blk, blk), blk)
        s = lax.dot_general(q, k_ref[rows, :], (((1,), (1,)), ((), ())),
                            preferred_element_type=F32)
        if masked:
            r = lax.broadcasted_iota(jnp.int32, s.shape, 0)
            c = lax.broadcasted_iota(jnp.int32, s.shape, 1)
            s = jnp.where(c <= r, s, NEG_BIG)
        m_old = m_ref[...]
        m_new = jnp.maximum(m_old, jnp.max(s, axis=-1, keepdims=True))
        p = jnp.exp(s - m_new).astype(BF16)
        acc_ref[...] = jnp.exp(m_old - m_new) * acc_ref[...] + jnp.dot(
            p, vext_ref[rows, :], preferred_element_type=F32)
        m_ref[...] = m_new

    def body(j, carry):
        step(j, False)
        return carry

    lax.fori_loop(0, qi, body, 0)
    step(qi, True)
    acc = acc_ref[...]
    o = acc[:, :MLA_V] / acc[:, MLA_V:]
    o_ref[...] = (_silu(g_ref[...].astype(F32)) * o).astype(BF16)


def _attention(q, k, v, proj, gate_col, batch, seq, heads):
    t = q.shape[0]
    blk = min(seq, 512)
    assert seq % blk == 0 and gate_col % MLA_V == 0
    nq = seq // blk
    g_blk0 = gate_col // MLA_V
    return pl.pallas_call(
        functools.partial(_attn_kernel, blk=blk),
        grid=(batch, heads, nq),
        in_specs=[
            pl.BlockSpec((blk, MLA_HEAD_PAD), lambda b, h, i: (b * nq + i, h)),
            pl.BlockSpec((seq, MLA_HEAD_PAD), lambda b, h, i: (b, h)),
            pl.BlockSpec((seq, MLA_V), lambda b, h, i: (b, h)),
            pl.BlockSpec((blk, MLA_V), lambda b, h, i: (b * nq + i, g_blk0 + h)),
        ],
        out_specs=pl.BlockSpec((blk, MLA_V), lambda b, h, i: (b * nq + i, h)),
        out_shape=jax.ShapeDtypeStruct((t, heads * MLA_V), BF16),
        scratch_shapes=[
            pltpu.VMEM((seq, MLA_V + LANES), BF16),
            pltpu.VMEM((blk, MLA_V + LANES), F32),
            pltpu.VMEM((blk, 1), F32),
        ],
        compiler_params=_params("parallel", "parallel", "arbitrary"),
        name="mla_attention",
    )(q, k, v, proj)


def _rotate_half_cols(w):
    half = w.shape[-1] // 2
    return jnp.concatenate([-w[..., half:], w[..., :half]], axis=-1)


def _mla_weights(w_in, w_uq, w_ukv, q_rank, kv_rank, heads):
    d = w_in.shape[0]
    cq = w_in[:, :q_rank]
    ckv = w_in[:, q_rank:q_rank + kv_rank]
    kr = w_in[:, q_rank + kv_rank:q_rank + kv_rank + MLA_ROPE]
    gate = w_in[:, q_rank + kv_rank + MLA_ROPE:]
    w_in_p = jnp.concatenate([cq, ckv, gate, kr, _rotate_half_cols(kr)], axis=1).astype(BF16)
    gate_col = q_rank + kv_rank
    kr_col = gate_col + gate.shape[1]
    uq = w_uq.reshape(q_rank, heads, MLA_NOPE + MLA_ROPE)
    rope = uq[..., MLA_NOPE:]
    uq_p = jnp.concatenate([uq[..., :MLA_NOPE], rope, _rotate_half_cols(rope)], axis=-1)
    uq_p = uq_p.reshape(q_rank, heads * MLA_HEAD_PAD).astype(BF16)
    ukv = w_ukv.reshape(kv_rank, heads, MLA_NOPE + MLA_V)
    ukv_p = jnp.concatenate([ukv[..., :MLA_NOPE].reshape(kv_rank, heads * MLA_NOPE),
                             ukv[..., MLA_NOPE:].reshape(kv_rank, heads * MLA_V)],
                            axis=1).astype(BF16)
    del d
    return w_in_p, uq_p, ukv_p, gate_col, kr_col


def kernel(x, c, positions, ada_w, ada_b, norm_g, ret_w_in, ret_gn_g, ret_w_out, mla_w_in,
           mla_q_norm_g, mla_w_uq, mla_kv_norm_g, mla_w_ukv, mla_w_out, final_norm_g):
    batch, seq, d = x.shape
    t = batch * seq
    depth = ada_w.shape[0]
    assert depth % 2 == 0, "the last layer must be a latent-attention layer (final norm fusion)"
    ret_dk = d // RET_HEADS
    ret_dv = ret_w_out.shape[1] // RET_HEADS
    q_rank = mla_q_norm_g.shape[1]
    kv_rank = mla_kv_norm_g.shape[1]

    mod = _modulation(c, ada_w, ada_b)

    pos = positions.reshape(t, 1).astype(F32)
    cos_r, sin_r = _rope_tables(jnp.broadcast_to(pos, (t, LANES)),
                                _inv_freq(ret_dk).reshape(1, LANES))
    n_freq = MLA_ROPE // 2
    per_row = LANES // n_freq
    pos4 = jnp.repeat(positions.reshape(t // per_row, per_row).astype(F32), n_freq, axis=1)
    cos4, sin4 = _rope_tables(pos4, jnp.tile(_inv_freq(MLA_ROPE), per_row).reshape(1, LANES))
    pad = jnp.zeros((t, LANES - MLA_ROPE), F32)
    cos_m = jnp.concatenate([cos4.reshape(t, n_freq)] * 2 + [pad], axis=1)
    sin_m = jnp.concatenate([sin4.reshape(t, n_freq)] * 2 + [pad], axis=1)

    x2d = x.reshape(t, d)
    for layer in range(depth):
        j = layer // 2
        if layer % 2 == 0:
            proj = _ret_inproj(x2d, norm_g[layer], mod, layer, cos_r, sin_r,
                               ret_w_in[j].astype(BF16), seq, 2 * d, ret_dk)
            y = _retention(proj, ret_gn_g[j], batch, seq, RET_HEADS, ret_dk, ret_dv)
            x2d = _outproj(y, ret_w_out[j].astype(BF16), x2d, mod, layer, seq)
        else:
            w_in_p, uq_p, ukv_p, gate_col, kr_col = _mla_weights(
                mla_w_in[j], mla_w_uq[j], mla_w_ukv[j], q_rank, kv_rank, MLA_HEADS)
            proj = _mla_inproj(x2d, norm_g[layer], mod, layer, w_in_p, seq)
            q, k, v = _mla_up(proj, mla_q_norm_g[j], mla_kv_norm_g[j], uq_p, ukv_p,
                              cos_m, sin_m, q_rank, kv_rank, kr_col, MLA_HEADS)
            o = _attention(q, k, v, proj, gate_col, batch, seq, MLA_HEADS)
            final_g = final_norm_g if layer == depth - 1 else None
            x2d = _outproj(o, mla_w_out[j].astype(BF16), x2d, mod, layer, seq, final_g)
    return x2d.reshape(batch, seq, d)
```

```python
import functools

import jax
import jax.numpy as jnp
from jax import lax
from jax.experimental import pallas as pl
from jax.experimental.pallas import tpu as pltpu

F32 = jnp.float32
BF16 = jnp.bfloat16

EPS = 1e-6
ROPE_BASE = 10000.0
NEG_BIG = -1e30
LOG2_E = 1.4426950408889634

LANES = 128
SUBLANES = 8
MXU_TILE = 256
OUT_CHUNK = 2 * MXU_TILE
VMEM_LIMIT_BYTES = 56 * 2**20

RET_HEADS = 8
RET_CHUNK = 128
MLA_HEADS = 16
MLA_NOPE = 128
MLA_ROPE = 64
MLA_V = 128
MLA_HEAD_PAD = 256
ONES_ROWS = 16


def _params(*semantics):
    return pltpu.CompilerParams(dimension_semantics=semantics,
                                vmem_limit_bytes=VMEM_LIMIT_BYTES)


def _silu(x):
    return x * jax.nn.sigmoid(x)


def _mod_kernel(c_ref, w_ref, b_ref, o_ref):
    act = _silu(c_ref[...]).astype(BF16)
    o_ref[...] = jnp.dot(act, w_ref[...].astype(BF16),
                         preferred_element_type=F32) + b_ref[...]


def _modulation(c, ada_w, ada_b):
    depth, d, n3 = ada_w.shape
    b = c.shape[0]
    assert b <= SUBLANES
    tn = 768
    assert n3 % tn == 0
    c_pad = jnp.zeros((SUBLANES, d), F32).at[:b].set(c)
    out = pl.pallas_call(
        _mod_kernel,
        grid=(depth, n3 // tn),
        in_specs=[
            pl.BlockSpec((SUBLANES, d), lambda i, n: (0, 0)),
            pl.BlockSpec((None, d, tn), lambda i, n: (i, 0, n)),
            pl.BlockSpec((None, 1, tn), lambda i, n: (i, 0, n)),
        ],
        out_specs=pl.BlockSpec((None, SUBLANES, tn), lambda i, n: (i, 0, n)),
        out_shape=jax.ShapeDtypeStruct((depth, SUBLANES, n3), F32),
        compiler_params=_params("parallel", "parallel"),
        name="adaln_modulation",
    )(c_pad, ada_w, ada_b.reshape(depth, 1, n3))
    return out[:, :b].reshape(depth, b, 1, n3)


def _rope_table_kernel(pos_ref, freq_ref, cos_ref, sin_ref):
    ang = pos_ref[...] * freq_ref[...]
    cos_ref[...] = jnp.cos(ang)
    sin_ref[...] = jnp.sin(ang)


def _rope_tables(pos_lanes, freq_lanes):
    rows = pos_lanes.shape[0]
    tr = min(rows, 1024)
    assert rows % tr == 0
    spec = pl.BlockSpec((tr, LANES), lambda i: (i, 0))
    return pl.pallas_call(
        _rope_table_kernel,
        grid=(rows // tr,),
        in_specs=[spec, pl.BlockSpec((1, LANES), lambda i: (0, 0))],
        out_specs=[spec, spec],
        out_shape=[jax.ShapeDtypeStruct((rows, LANES), F32)] * 2,
        compiler_params=_params("parallel"),
        name="rope_tables",
    )(pos_lanes, freq_lanes)


def _inv_freq(d):
    return ROPE_BASE ** (-jnp.arange(0, d, 2, dtype=F32) / d)


def _adaln_rmsnorm(x_ref, g_ref, shift_ref, scale_ref):
    x = x_ref[...]
    y = x * lax.rsqrt(jnp.mean(x * x, axis=-1, keepdims=True) + EPS)
    return (y * (g_ref[...] * (1.0 + scale_ref[...])) + shift_ref[...]).astype(BF16)


def _mod_specs(layer, rows_per_batch_block, d):
    def spec(part):
        return pl.BlockSpec((None, None, 1, d),
                            lambda i, *_: (layer, i // rows_per_batch_block, 0, part))
    return spec


def _resident_spec(shape):
    return pl.BlockSpec(shape, lambda *_: (0,) * len(shape), pipeline_mode=pl.Buffered(1))


def _ret_inproj_kernel(x_ref, g_ref, shift_ref, scale_ref, cos_ref, sin_ref, w_ref,
                       o_ref, h_ref, *, rope_blocks, head_dim):
    n = pl.program_id(1)

    @pl.when(n == 0)
    def _():
        h_ref[...] = _adaln_rmsnorm(x_ref, g_ref, shift_ref, scale_ref)

    half = head_dim // 2
    heads = range(0, o_ref.shape[1], head_dim)

    def project(c0):
        return jnp.dot(h_ref[...], w_ref[:, c0:c0 + head_dim], preferred_element_type=F32)

    @pl.when(n < rope_blocks)
    def _():
        for c0 in heads:
            acc = project(c0)
            x1 = acc[:, :half]
            x2 = acc[:, half:]
            cos = cos_ref[...]
            sin = sin_ref[...]
            o_ref[:, c0:c0 + half] = (x1 * cos - x2 * sin).astype(BF16)
            o_ref[:, c0 + half:c0 + head_dim] = (x2 * cos + x1 * sin).astype(BF16)

    @pl.when(n >= rope_blocks)
    def _():
        for c0 in heads:
            o_ref[:, c0:c0 + head_dim] = project(c0).astype(BF16)


def _ret_inproj(x2d, norm_g, mod, layer, cos, sin, w_bf16, seq, rope_cols, head_dim):
    t, d = x2d.shape
    n_out = w_bf16.shape[1]
    tm = min(seq, 1024)
    tn = 1024
    assert seq % tm == 0 and n_out % tn == 0 and rope_cols % tn == 0 and tn % head_dim == 0
    assert head_dim // 2 == LANES
    mspec = _mod_specs(layer, seq // tm, d)
    return pl.pallas_call(
        functools.partial(_ret_inproj_kernel, rope_blocks=rope_cols // tn, head_dim=head_dim),
        grid=(t // tm, n_out // tn),
        in_specs=[
            pl.BlockSpec((tm, d), lambda i, n: (i, 0)),
            pl.BlockSpec((1, d), lambda i, n: (0, 0)),
            mspec(0), mspec(1),
            pl.BlockSpec((tm, LANES), lambda i, n: (i, 0)),
            pl.BlockSpec((tm, LANES), lambda i, n: (i, 0)),
            pl.BlockSpec((d, tn), lambda i, n: (0, n)),
        ],
        out_specs=pl.BlockSpec((tm, tn), lambda i, n: (i, n)),
        out_shape=jax.ShapeDtypeStruct((t, n_out), BF16),
        scratch_shapes=[pltpu.VMEM((tm, d), BF16)],
        compiler_params=_params("parallel", "arbitrary"),
        name="ret_in_proj",
    )(x2d, norm_g.reshape(1, d), mod, mod, cos, sin, w_bf16)


def _retention_kernel(decay_ref, q_ref, k_ref, v_ref, g_ref, gn_ref, dmat_ref, xi_ref,
                      zeta_ref, o_ref, state_ref, *, chunk):
    @pl.when(pl.program_id(2) == 0)
    def _():
        state_ref[...] = jnp.zeros_like(state_ref)

    decay = decay_ref[pl.program_id(1)]
    n_chunks = q_ref.shape[0] // chunk

    def body(c, carry):
        rows = pl.ds(pl.multiple_of(c * chunk, chunk), chunk)
        q = q_ref[rows, :]
        k = k_ref[rows, :]
        v = v_ref[rows, :]
        s = lax.dot_general(q, k, (((1,), (1,)), ((), ())), preferred_element_type=F32)
        s = (s * dmat_ref[...]).astype(BF16)
        state = state_ref[...]
        y = (jnp.dot(s, v, preferred_element_type=F32)
             + jnp.dot(q, state.astype(BF16), preferred_element_type=F32) * xi_ref[...])
        vz = (v.astype(F32) * zeta_ref[...]).astype(BF16)
        state_ref[...] = state * decay + lax.dot_general(
            k, vz, (((0,), (0,)), ((), ())), preferred_element_type=F32)
        yc = y - jnp.mean(y, axis=-1, keepdims=True)
        var = jnp.mean(yc * yc, axis=-1, keepdims=True)
        yn = yc * lax.rsqrt(var + EPS) * gn_ref[...]
        o_ref[rows, :] = (_silu(g_ref[rows, :].astype(F32)) * yn).astype(BF16)
        return carry

    lax.fori_loop(0, n_chunks, body, 0)


def _retention_constants(heads, chunk, dk, dv):
    log_gamma = jnp.log1p(-jnp.exp2(-5.0 - jnp.arange(heads, dtype=F32)))
    idx = jnp.arange(chunk, dtype=F32)
    diff = idx[:, None] - idx[None, :]
    k_scale = dk ** -0.5
    d_intra = jnp.where(diff >= 0,
                        jnp.exp(log_gamma[:, None, None] * jnp.maximum(diff, 0.0)), 0.0) * k_scale
    xi = jnp.exp(log_gamma[:, None] * (idx + 1.0))
    zeta = jnp.exp(log_gamma[:, None] * (chunk - 1.0 - idx)) * k_scale
    chunk_decay = jnp.exp(log_gamma * chunk)
    bcast = lambda a: jnp.broadcast_to(a[:, :, None], (heads, chunk, dv))
    return d_intra, bcast(xi), bcast(zeta), chunk_decay


def _retention(proj, gn_g, batch, seq, heads, dk, dv):
    t = proj.shape[0]
    width = heads * dv
    chunk = RET_CHUNK
    sb = min(seq, 1024)
    assert seq % sb == 0 and sb % chunk == 0
    nsb = seq // sb
    d_intra, xi, zeta, chunk_decay = _retention_constants(heads, chunk, dk, dv)
    k_blk0 = heads
    v_blk0 = 2 * heads * dk // dv
    g_blk0 = v_blk0 + heads
    row = lambda b, h, s: b * nsb + s
    return pl.pallas_call(
        functools.partial(_retention_kernel, chunk=chunk),
        grid=(batch, heads, nsb),
        in_specs=[
            pl.BlockSpec(memory_space=pltpu.SMEM),
            pl.BlockSpec((sb, dk), lambda b, h, s: (row(b, h, s), h)),
            pl.BlockSpec((sb, dk), lambda b, h, s: (row(b, h, s), k_blk0 + h)),
            pl.BlockSpec((sb, dv), lambda b, h, s: (row(b, h, s), v_blk0 + h)),
            pl.BlockSpec((sb, dv), lambda b, h, s: (row(b, h, s), g_blk0 + h)),
            pl.BlockSpec((1, dv), lambda b, h, s: (0, h)),
            pl.BlockSpec((None, chunk, chunk), lambda b, h, s: (h, 0, 0)),
            pl.BlockSpec((None, chunk, dv), lambda b, h, s: (h, 0, 0)),
            pl.BlockSpec((None, chunk, dv), lambda b, h, s: (h, 0, 0)),
        ],
        out_specs=pl.BlockSpec((sb, dv), lambda b, h, s: (row(b, h, s), h)),
        out_shape=jax.ShapeDtypeStruct((t, width), BF16),
        scratch_shapes=[pltpu.VMEM((dk, dv), F32)],
        compiler_params=_params("parallel", "parallel", "arbitrary"),
        name="retention",
    )(chunk_decay, proj, proj, proj, proj, gn_g.reshape(1, width), d_intra, xi, zeta)


def _residual_chunks(a_ref, w_ref, x_ref, gate_ref, o_ref, want_ssq):
    ssq = None
    for c0 in range(0, o_ref.shape[1], OUT_CHUNK):
        cols = slice(c0, c0 + OUT_CHUNK)
        y = jnp.dot(a_ref[...], w_ref[:, cols], preferred_element_type=F32)
        z = x_ref[:, cols] + gate_ref[:, cols] * y
        o_ref[:, cols] = z
        if want_ssq:
            part = jnp.sum(z * z, axis=-1, keepdims=True)
            ssq = part if ssq is None else ssq + part
    return ssq


def _outproj_kernel(a_ref, w_ref, x_ref, gate_ref, o_ref):
    _residual_chunks(a_ref, w_ref, x_ref, gate_ref, o_ref, False)


def _outproj_final_kernel(a_ref, w_ref, x_ref, gate_ref, fg_ref, o_ref):
    ssq = _residual_chunks(a_ref, w_ref, x_ref, gate_ref, o_ref, True)
    inv = lax.rsqrt(ssq * (1.0 / o_ref.shape[1]) + EPS)
    o_ref[...] = o_ref[...] * inv * fg_ref[...]


def _outproj(a, w_bf16, x2d, mod, layer, seq, final_g=None):
    t, k = a.shape
    d = w_bf16.shape[1]
    tm = min(seq, 512)
    assert seq % tm == 0 and d % OUT_CHUNK == 0
    in_specs = [
        pl.BlockSpec((tm, k), lambda i: (i, 0)),
        _resident_spec(w_bf16.shape),
        pl.BlockSpec((tm, d), lambda i: (i, 0)),
        _mod_specs(layer, seq // tm, d)(2),
    ]
    args = [a, w_bf16, x2d, mod]
    kern = _outproj_kernel
    if final_g is not None:
        in_specs.append(pl.BlockSpec((1, d), lambda i: (0, 0)))
        args.append(final_g.reshape(1, d))
        kern = _outproj_final_kernel
    return pl.pallas_call(
        kern,
        grid=(t // tm,),
        in_specs=in_specs,
        out_specs=pl.BlockSpec((tm, d), lambda i: (i, 0)),
        out_shape=jax.ShapeDtypeStruct((t, d), F32),
        compiler_params=_params("parallel"),
        name="out_proj_final" if final_g is not None else "out_proj",
    )(*args)


def _mla_inproj_kernel(x_ref, g_ref, shift_ref, scale_ref, w_ref, o_ref, h_ref):
    h_ref[...] = _adaln_rmsnorm(x_ref, g_ref, shift_ref, scale_ref)
    for c0 in range(0, o_ref.shape[1], MXU_TILE):
        o_ref[:, c0:c0 + MXU_TILE] = jnp.dot(
            h_ref[...], w_ref[:, c0:c0 + MXU_TILE], preferred_element_type=F32).astype(BF16)


def _mla_inproj(x2d, norm_g, mod, layer, w_bf16, seq):
    t, d = x2d.shape
    n_out = w_bf16.shape[1]
    tm = min(seq, 512)
    assert seq % tm == 0 and n_out % MXU_TILE == 0
    mspec = _mod_specs(layer, seq // tm, d)
    return pl.pallas_call(
        _mla_inproj_kernel,
        grid=(t // tm,),
        in_specs=[
            pl.BlockSpec((tm, d), lambda i: (i, 0)),
            pl.BlockSpec((1, d), lambda i: (0, 0)),
            mspec(0), mspec(1),
            _resident_spec(w_bf16.shape),
        ],
        out_specs=pl.BlockSpec((tm, n_out), lambda i: (i, 0)),
        out_shape=jax.ShapeDtypeStruct((t, n_out), BF16),
        scratch_shapes=[pltpu.VMEM((tm, d), BF16)],
        compiler_params=_params("parallel"),
        name="mla_in_proj",
    )(x2d, norm_g.reshape(1, d), mod, mod, w_bf16)


def _mla_up_kernel(cq_ref, ckv_ref, kr_ref, qg_ref, kvg_ref, wq_ref, wkv_ref, cos_ref,
                   sin_ref, q_ref, k_ref, v_ref, *, heads, q_scale, group):
    cos = cos_ref[...]
    sin = sin_ref[...]

    def rope_pad(x):
        return x * cos + pltpu.roll(x, LANES // 2, axis=1) * sin

    def rms(x_ref, g_ref):
        x = x_ref[...].astype(F32)
        return (x * lax.rsqrt(jnp.mean(x * x, axis=-1, keepdims=True) + EPS)
                * g_ref[...]).astype(BF16)

    cqn = rms(cq_ref, qg_ref)
    ckvn = rms(ckv_ref, kvg_ref)
    kr = rope_pad(kr_ref[...].astype(F32)).astype(BF16)
    gw = group * MLA_HEAD_PAD
    for g0 in range(0, heads, group):
        q = jnp.dot(cqn, wq_ref[:, g0 * MLA_HEAD_PAD:g0 * MLA_HEAD_PAD + gw],
                    preferred_element_type=F32)
        for j in range(group):
            c0 = j * MLA_HEAD_PAD
            o0 = (g0 + j) * MLA_HEAD_PAD
            q_ref[:, o0:o0 + LANES] = (q[:, c0:c0 + LANES] * q_scale).astype(BF16)
            q_ref[:, o0 + LANES:o0 + 2 * LANES] = (
                rope_pad(q[:, c0 + LANES:c0 + 2 * LANES]) * q_scale).astype(BF16)
    nope_w = heads * MLA_NOPE
    for g0 in range(0, heads, group):
        kn = jnp.dot(ckvn, wkv_ref[:, g0 * MLA_NOPE:(g0 + group) * MLA_NOPE],
                     preferred_element_type=F32)
        for j in range(group):
            o0 = (g0 + j) * MLA_HEAD_PAD
            k_ref[:, o0:o0 + LANES] = kn[:, j * MLA_NOPE:(j + 1) * MLA_NOPE].astype(BF16)
            k_ref[:, o0 + LANES:o0 + 2 * LANES] = kr
    v_ref[...] = jnp.dot(ckvn, wkv_ref[:, nope_w:], preferred_element_type=F32).astype(BF16)


def _mla_up(proj, q_norm_g, kv_norm_g, wq_bf16, wkv_bf16, cos, sin, q_rank, kv_rank,
            kr_col, heads):
    t = proj.shape[0]
    tm = min(t, 512)
    assert t % tm == 0 and q_rank == kv_rank and kr_col % LANES == 0
    return pl.pallas_call(
        functools.partial(_mla_up_kernel, heads=heads,
                          q_scale=LOG2_E * (MLA_NOPE + MLA_ROPE) ** -0.5, group=4),
        grid=(t // tm,),
        in_specs=[
            pl.BlockSpec((tm, q_rank), lambda i: (i, 0)),
            pl.BlockSpec((tm, kv_rank), lambda i: (i, 1)),
            pl.BlockSpec((tm, LANES), lambda i: (i, kr_col // LANES)),
            pl.BlockSpec((1, q_rank), lambda i: (0, 0)),
            pl.BlockSpec((1, kv_rank), lambda i: (0, 0)),
            pl.BlockSpec(wq_bf16.shape, lambda i: (0, 0)),
            pl.BlockSpec(wkv_bf16.shape, lambda i: (0, 0)),
            pl.BlockSpec((tm, LANES), lambda i: (i, 0)),
            pl.BlockSpec((tm, LANES), lambda i: (i, 0)),
        ],
        out_specs=[
            pl.BlockSpec((tm, heads * MLA_HEAD_PAD), lambda i: (i, 0)),
            pl.BlockSpec((tm, heads * MLA_HEAD_PAD), lambda i: (i, 0)),
            pl.BlockSpec((tm, heads * MLA_V), lambda i: (i, 0)),
        ],
        out_shape=[
            jax.ShapeDtypeStruct((t, heads * MLA_HEAD_PAD), BF16),
            jax.ShapeDtypeStruct((t, heads * MLA_HEAD_PAD), BF16),
            jax.ShapeDtypeStruct((t, heads * MLA_V), BF16),
        ],
        compiler_params=_params("parallel"),
        name="mla_up_proj",
    )(proj, proj, proj, q_norm_g.reshape(1, q_rank), kv_norm_g.reshape(1, kv_rank),
      wq_bf16, wkv_bf16, cos, sin)


def _attn_kernel(q_ref, k_ref, v_ref, g_ref, o_ref, vt_ref, s_ref, acc_ref, m_ref, *, tk):
    qi = pl.program_id(2)
    tq = q_ref.shape[0]
    all_q = slice(0, tq)
    upper_q = slice(tk, tq)

    @pl.when(qi == 0)
    def _():
        for j in range(vt_ref.shape[0]):
            vt_ref[j, :MLA_V, :] = v_ref[j * tk:(j + 1) * tk, :].astype(F32).T.astype(BF16)
            vt_ref[j, MLA_V:, :] = jnp.ones((ONES_ROWS, tk), BF16)

    m_ref[...] = jnp.full(m_ref.shape, NEG_BIG, F32)
    acc_ref[...] = jnp.zeros_like(acc_ref)

    def scores(t, qs):
        kt = k_ref[pl.ds(pl.multiple_of(t * tk, tk), tk), :]
        return lax.dot_general(kt, q_ref[qs, :], (((1,), (1,)), ((), ())),
                               preferred_element_type=F32)

    def consume(t, slot, qs, diagonal):
        st = s_ref[slot, :, qs]
        if diagonal:
            key = lax.broadcasted_iota(jnp.int32, st.shape, 0)
            qry = lax.broadcasted_iota(jnp.int32, st.shape, 1)
            st = jnp.where(key <= qry, st, NEG_BIG)
        m_old = m_ref[:, qs]
        m_new = jnp.maximum(m_old, jnp.max(st, axis=0, keepdims=True))
        p = jnp.exp2(st - m_new).astype(BF16)
        acc_ref[:, qs] = jnp.exp2(m_old - m_new) * acc_ref[:, qs] + jnp.dot(
            vt_ref[t], p, preferred_element_type=F32)
        m_ref[:, qs] = m_new

    s_ref[0] = scores(0, all_q)

    def pair(u, carry):
        t = 2 * u
        s_ref[1] = scores(t + 1, all_q)
        consume(t, 0, all_q, False)
        s_ref[0] = scores(t + 2, all_q)
        consume(t + 1, 1, all_q, False)
        return carry

    lax.fori_loop(0, qi, pair, 0)
    s_ref[1, :, upper_q] = scores(2 * qi + 1, upper_q)
    consume(2 * qi, 0, all_q, True)
    consume(2 * qi + 1, 1, upper_q, True)

    acc = acc_ref[...]
    o = (acc[:MLA_V, :] * (1.0 / acc[MLA_V:MLA_V + 1, :])).T
    o_ref[...] = (_silu(g_ref[...].astype(F32)) * o).astype(BF16)


def _attention(q, k, v, proj, gate_col, batch, seq, heads):
    t = q.shape[0]
    tk = min(seq // 2, 512)
    tq = 2 * tk
    assert seq % tq == 0 and gate_col % MLA_V == 0 and tk % LANES == 0
    nq = seq // tq
    g_blk0 = gate_col // MLA_V
    return pl.pallas_call(
        functools.partial(_attn_kernel, tk=tk),
        grid=(batch, heads, nq),
        in_specs=[
            pl.BlockSpec((tq, MLA_HEAD_PAD), lambda b, h, i: (b * nq + i, h)),
            pl.BlockSpec((seq, MLA_HEAD_PAD), lambda b, h, i: (b, h)),
            pl.BlockSpec((seq, MLA_V), lambda b, h, i: (b, h)),
            pl.BlockSpec((tq, MLA_V), lambda b, h, i: (b * nq + i, g_blk0 + h)),
        ],
        out_specs=pl.BlockSpec((tq, MLA_V), lambda b, h, i: (b * nq + i, h)),
        out_shape=jax.ShapeDtypeStruct((t, heads * MLA_V), BF16),
        scratch_shapes=[
            pltpu.VMEM((seq // tk, MLA_V + ONES_ROWS, tk), BF16),
            pltpu.VMEM((2, tk, tq), F32),
            pltpu.VMEM((MLA_V + ONES_ROWS, tq), F32),
            pltpu.VMEM((1, tq), F32),
        ],
        compiler_params=_params("parallel", "parallel", "arbitrary"),
        name="mla_attention",
    )(q, k, v, proj)


def _rotate_half_cols(w):
    half = w.shape[-1] // 2
    return jnp.concatenate([-w[..., half:], w[..., :half]], axis=-1)


def _mla_weights(w_in, w_uq, w_ukv, q_rank, kv_rank, heads):
    d = w_in.shape[0]
    cq = w_in[:, :q_rank]
    ckv = w_in[:, q_rank:q_rank + kv_rank]
    kr = w_in[:, q_rank + kv_rank:q_rank + kv_rank + MLA_ROPE]
    gate = w_in[:, q_rank + kv_rank + MLA_ROPE:]
    gate_col = q_rank + kv_rank
    kr_col = gate_col + gate.shape[1]
    pad = jnp.zeros((d, -(kr_col + 2 * MLA_ROPE) % MXU_TILE), w_in.dtype)
    w_in_p = jnp.concatenate([cq, ckv, gate, kr, _rotate_half_cols(kr), pad],
                             axis=1).astype(BF16)
    uq = w_uq.reshape(q_rank, heads, MLA_NOPE + MLA_ROPE)
    rope = uq[..., MLA_NOPE:]
    uq_p = jnp.concatenate([uq[..., :MLA_NOPE], rope, _rotate_half_cols(rope)], axis=-1)
    uq_p = uq_p.reshape(q_rank, heads * MLA_HEAD_PAD).astype(BF16)
    ukv = w_ukv.reshape(kv_rank, heads, MLA_NOPE + MLA_V)
    ukv_p = jnp.concatenate([ukv[..., :MLA_NOPE].reshape(kv_rank, heads * MLA_NOPE),
                             ukv[..., MLA_NOPE:].reshape(kv_rank, heads * MLA_V)],
                            axis=1).astype(BF16)
    return w_in_p, uq_p, ukv_p, gate_col, kr_col


def kernel(x, c, positions, ada_w, ada_b, norm_g, ret_w_in, ret_gn_g, ret_w_out, mla_w_in,
           mla_q_norm_g, mla_w_uq, mla_kv_norm_g, mla_w_ukv, mla_w_out, final_norm_g):
    batch, seq, d = x.shape
    t = batch * seq
    depth = ada_w.shape[0]
    assert depth % 2 == 0, "the last layer must be a latent-attention layer (final norm fusion)"
    ret_dk = d // RET_HEADS
    ret_dv = ret_w_out.shape[1] // RET_HEADS
    q_rank = mla_q_norm_g.shape[1]
    kv_rank = mla_kv_norm_g.shape[1]

    mod = _modulation(c, ada_w, ada_b)

    pos = positions.reshape(t, 1).astype(F32)
    cos_r, sin_r = _rope_tables(jnp.broadcast_to(pos, (t, LANES)),
                                _inv_freq(ret_dk).reshape(1, LANES))
    n_freq = MLA_ROPE // 2
    per_row = LANES // n_freq
    pos4 = jnp.repeat(positions.reshape(t // per_row, per_row).astype(F32), n_freq, axis=1)
    cos4, sin4 = _rope_tables(pos4, jnp.tile(_inv_freq(MLA_ROPE), per_row).reshape(1, LANES))
    pad = jnp.zeros((t, LANES - MLA_ROPE), F32)
    cos_m = jnp.concatenate([cos4.reshape(t, n_freq)] * 2 + [pad], axis=1)
    sin_m = jnp.concatenate([sin4.reshape(t, n_freq)] * 2 + [pad], axis=1)

    x2d = x.reshape(t, d)
    for layer in range(depth):
        j = layer // 2
        if layer % 2 == 0:
            proj = _ret_inproj(x2d, norm_g[layer], mod, layer, cos_r, sin_r,
                               ret_w_in[j].astype(BF16), seq, 2 * d, ret_dk)
            y = _retention(proj, ret_gn_g[j], batch, seq, RET_HEADS, ret_dk, ret_dv)
            x2d = _outproj(y, ret_w_out[j].astype(BF16), x2d, mod, layer, seq)
        else:
            w_in_p, uq_p, ukv_p, gate_col, kr_col = _mla_weights(
                mla_w_in[j], mla_w_uq[j], mla_w_ukv[j], q_rank, kv_rank, MLA_HEADS)
            proj = _mla_inproj(x2d, norm_g[layer], mod, layer, w_in_p, seq)
            q, k, v = _mla_up(proj, mla_q_norm_g[j], mla_kv_norm_g[j], uq_p, ukv_p,
                              cos_m, sin_m, q_rank, kv_rank, kr_col, MLA_HEADS)
            o = _attention(q, k, v, proj, gate_col, batch, seq, MLA_HEADS)
            final_g = final_norm_g if layer == depth - 1 else None
            x2d = _outproj(o, mla_w_out[j].astype(BF16), x2d, mod, layer, seq, final_g)
    return x2d.reshape(batch, seq, d)
```

```python
import functools

import jax
import jax.numpy as jnp
import numpy as np
from jax import lax
from jax.experimental import pallas as pl
from jax.experimental.pallas import tpu as pltpu

F32 = jnp.float32
BF16 = jnp.bfloat16

EPS = 1e-6
ROPE_BASE = 10000.0
NEG_BIG = -1e30
LOG2_E = 1.4426950408889634

LANES = 128
SUBLANES = 8
MXU_TILE = 256
OUT_CHUNK = 2 * MXU_TILE
VMEM_LIMIT_BYTES = 56 * 2**20

RET_HEADS = 8
RET_CHUNK = 128
MLA_HEADS = 16
MLA_NOPE = 128
MLA_ROPE = 64
MLA_V = 128
MLA_HEAD_PAD = 256
ONES_ROWS = 16


def _params(*semantics):
    return pltpu.CompilerParams(dimension_semantics=semantics,
                                vmem_limit_bytes=VMEM_LIMIT_BYTES)


def _silu(x):
    return x * jax.nn.sigmoid(x)


def _mod_kernel(c_ref, w_ref, b_ref, o_ref):
    act = _silu(c_ref[...]).astype(BF16)
    o_ref[...] = jnp.dot(act, w_ref[...].astype(BF16),
                         preferred_element_type=F32) + b_ref[...]


def _modulation(c, ada_w, ada_b):
    depth, d, n3 = ada_w.shape
    b = c.shape[0]
    assert b <= SUBLANES
    tn = 768
    assert n3 % tn == 0
    c_pad = jnp.zeros((SUBLANES, d), F32).at[:b].set(c)
    out = pl.pallas_call(
        _mod_kernel,
        grid=(depth, n3 // tn),
        in_specs=[
            pl.BlockSpec((SUBLANES, d), lambda i, n: (0, 0)),
            pl.BlockSpec((None, d, tn), lambda i, n: (i, 0, n)),
            pl.BlockSpec((None, 1, tn), lambda i, n: (i, 0, n)),
        ],
        out_specs=pl.BlockSpec((None, SUBLANES, tn), lambda i, n: (i, 0, n)),
        out_shape=jax.ShapeDtypeStruct((depth, SUBLANES, n3), F32),
        compiler_params=_params("parallel", "parallel"),
        name="adaln_modulation",
    )(c_pad, ada_w, ada_b.reshape(depth, 1, n3))
    return out[:, :b].reshape(depth, b, 1, n3)


def _rope_table_kernel(pos_ref, freq_ref, cos_ref, sin_ref):
    ang = pos_ref[...] * freq_ref[...]
    cos_ref[...] = jnp.cos(ang)
    sin_ref[...] = jnp.sin(ang)


def _rope_tables(pos_lanes, freq_lanes):
    rows = pos_lanes.shape[0]
    tr = min(rows, 1024)
    assert rows % tr == 0
    spec = pl.BlockSpec((tr, LANES), lambda i: (i, 0))
    return pl.pallas_call(
        _rope_table_kernel,
        grid=(rows // tr,),
        in_specs=[spec, pl.BlockSpec((1, LANES), lambda i: (0, 0))],
        out_specs=[spec, spec],
        out_shape=[jax.ShapeDtypeStruct((rows, LANES), F32)] * 2,
        compiler_params=_params("parallel"),
        name="rope_tables",
    )(pos_lanes, freq_lanes)


def _inv_freq(d):
    return ROPE_BASE ** (-jnp.arange(0, d, 2, dtype=F32) / d)


def _adaln_rmsnorm(x_ref, g_ref, shift_ref, scale_ref):
    x = x_ref[...]
    y = x * lax.rsqrt(jnp.mean(x * x, axis=-1, keepdims=True) + EPS)
    return (y * (g_ref[...] * (1.0 + scale_ref[...])) + shift_ref[...]).astype(BF16)


def _mod_specs(layer, rows_per_batch_block, d):
    def spec(part):
        return pl.BlockSpec((None, None, 1, d),
                            lambda i, *_: (layer, i // rows_per_batch_block, 0, part))
    return spec


def _resident_spec(shape):
    return pl.BlockSpec(shape, lambda *_: (0,) * len(shape), pipeline_mode=pl.Buffered(1))


def _ret_inproj_kernel(x_ref, g_ref, shift_ref, scale_ref, cos_ref, sin_ref, w_ref,
                       o_ref, h_ref, *, rope_blocks, gate_block0, head_dim):
    n = pl.program_id(1)

    @pl.when(n == 0)
    def _():
        h_ref[...] = _adaln_rmsnorm(x_ref, g_ref, shift_ref, scale_ref)

    half = head_dim // 2
    heads = range(0, o_ref.shape[1], head_dim)

    def project(c0):
        return jnp.dot(h_ref[...], w_ref[:, c0:c0 + head_dim], preferred_element_type=F32)

    @pl.when(n < rope_blocks)
    def _():
        for c0 in heads:
            acc = project(c0)
            x1 = acc[:, :half]
            x2 = acc[:, half:]
            cos = cos_ref[...]
            sin = sin_ref[...]
            o_ref[:, c0:c0 + half] = (x1 * cos - x2 * sin).astype(BF16)
            o_ref[:, c0 + half:c0 + head_dim] = (x2 * cos + x1 * sin).astype(BF16)

    @pl.when((n >= rope_blocks) & (n < gate_block0))
    def _():
        for c0 in heads:
            o_ref[:, c0:c0 + head_dim] = project(c0).astype(BF16)

    @pl.when(n >= gate_block0)
    def _():
        for c0 in heads:
            o_ref[:, c0:c0 + head_dim] = _silu(project(c0)).astype(BF16)


def _ret_inproj(x2d, norm_g, mod, layer, cos, sin, w_bf16, seq, rope_cols, gate_col, head_dim):
    t, d = x2d.shape
    n_out = w_bf16.shape[1]
    tm = min(seq, 1024)
    tn = 1024
    assert seq % tm == 0 and n_out % tn == 0 and rope_cols % tn == 0 and tn % head_dim == 0
    assert head_dim // 2 == LANES and gate_col % tn == 0
    mspec = _mod_specs(layer, seq // tm, d)
    return pl.pallas_call(
        functools.partial(_ret_inproj_kernel, rope_blocks=rope_cols // tn,
                          gate_block0=gate_col // tn, head_dim=head_dim),
        grid=(t // tm, n_out // tn),
        in_specs=[
            pl.BlockSpec((tm, d), lambda i, n: (i, 0)),
            pl.BlockSpec((1, d), lambda i, n: (0, 0)),
            mspec(0), mspec(1),
            pl.BlockSpec((tm, LANES), lambda i, n: (i, 0)),
            pl.BlockSpec((tm, LANES), lambda i, n: (i, 0)),
            pl.BlockSpec((d, tn), lambda i, n: (0, n)),
        ],
        out_specs=pl.BlockSpec((tm, tn), lambda i, n: (i, n)),
        out_shape=jax.ShapeDtypeStruct((t, n_out), BF16),
        scratch_shapes=[pltpu.VMEM((tm, d), BF16)],
        compiler_params=_params("parallel", "arbitrary"),
        name="ret_in_proj",
    )(x2d, norm_g.reshape(1, d), mod, mod, cos, sin, w_bf16)


def _retention_kernel(decay_ref, q_ref, k_ref, v_ref, sg_ref, gn_ref, dmat_ref, xi_ref,
                      zeta_ref, o_ref, state_ref, *, chunk):
    @pl.when(pl.program_id(2) == 0)
    def _():
        state_ref[...] = jnp.zeros_like(state_ref)

    decay = decay_ref[pl.program_id(1)]
    for c in range(q_ref.shape[0] // chunk):
        rows = slice(c * chunk, (c + 1) * chunk)
        q = q_ref[rows, :]
        k = k_ref[rows, :]
        v = v_ref[rows, :]
        s = lax.dot_general(q, k, (((1,), (1,)), ((), ())), preferred_element_type=F32)
        kz = (k.astype(F32) * zeta_ref[...]).astype(BF16)
        update = lax.dot_general(kz, v, (((0,), (0,)), ((), ())), preferred_element_type=F32)
        s = (s * dmat_ref[...]).astype(BF16)
        state = state_ref[...]
        y = (jnp.dot(s, v, preferred_element_type=F32)
             + jnp.dot(q, state.astype(BF16), preferred_element_type=F32) * xi_ref[...])
        state_ref[...] = state * decay + update
        yc = y - jnp.mean(y, axis=-1, keepdims=True)
        var = jnp.mean(yc * yc, axis=-1, keepdims=True)
        yn = yc * lax.rsqrt(var + EPS) * gn_ref[...]
        o_ref[rows, :] = (sg_ref[rows, :].astype(F32) * yn).astype(BF16)


def _retention_constants(heads, chunk, dk, dv):
    f32 = np.float32
    log_gamma = np.log1p(-np.exp2(f32(-5.0) - np.arange(heads, dtype=f32))).astype(f32)
    idx = np.arange(chunk, dtype=f32)
    diff = idx[:, None] - idx[None, :]
    k_scale = f32(dk ** -0.5)
    d_intra = np.where(diff >= 0, np.exp(log_gamma[:, None, None] * np.maximum(diff, 0)),
                       0).astype(f32) * k_scale
    xi = np.exp(log_gamma[:, None] * (idx + 1)).astype(f32)
    zeta = np.exp(log_gamma[:, None] * (chunk - 1 - idx)).astype(f32) * k_scale
    chunk_decay = np.exp(log_gamma * chunk).astype(f32)
    bcast = lambda a, n: np.ascontiguousarray(np.broadcast_to(a[:, :, None], (heads, chunk, n)))
    return d_intra, bcast(xi, dv), bcast(zeta, dk), chunk_decay


def _retention(proj, gn_g, batch, seq, heads, dk, dv):
    t = proj.shape[0]
    width = heads * dv
    chunk = RET_CHUNK
    sb = min(seq, 1024)
    assert seq % sb == 0 and sb % chunk == 0
    nsb = seq // sb
    d_intra, xi, zeta, chunk_decay = _retention_constants(heads, chunk, dk, dv)
    k_blk0 = heads
    v_blk0 = 2 * heads * dk // dv
    g_blk0 = v_blk0 + heads
    row = lambda b, h, s: b * nsb + s
    return pl.pallas_call(
        functools.partial(_retention_kernel, chunk=chunk),
        grid=(batch, heads, nsb),
        in_specs=[
            pl.BlockSpec(memory_space=pltpu.SMEM),
            pl.BlockSpec((sb, dk), lambda b, h, s: (row(b, h, s), h)),
            pl.BlockSpec((sb, dk), lambda b, h, s: (row(b, h, s), k_blk0 + h)),
            pl.BlockSpec((sb, dv), lambda b, h, s: (row(b, h, s), v_blk0 + h)),
            pl.BlockSpec((sb, dv), lambda b, h, s: (row(b, h, s), g_blk0 + h)),
            pl.BlockSpec((1, dv), lambda b, h, s: (0, h)),
            pl.BlockSpec((None, chunk, chunk), lambda b, h, s: (h, 0, 0)),
            pl.BlockSpec((None, chunk, dv), lambda b, h, s: (h, 0, 0)),
            pl.BlockSpec((None, chunk, dk), lambda b, h, s: (h, 0, 0)),
        ],
        out_specs=pl.BlockSpec((sb, dv), lambda b, h, s: (row(b, h, s), h)),
        out_shape=jax.ShapeDtypeStruct((t, width), BF16),
        scratch_shapes=[pltpu.VMEM((dk, dv), F32)],
        compiler_params=_params("parallel", "parallel", "arbitrary"),
        name="retention",
    )(chunk_decay, proj, proj, proj, proj, gn_g.reshape(1, width), d_intra, xi, zeta)


def _residual_chunks(a_ref, w_ref, x_ref, gate_ref, o_ref, want_ssq):
    ssq = None
    for c0 in range(0, o_ref.shape[1], OUT_CHUNK):
        cols = slice(c0, c0 + OUT_CHUNK)
        y = jnp.dot(a_ref[...], w_ref[:, cols], preferred_element_type=F32)
        z = x_ref[:, cols] + gate_ref[:, cols] * y
        o_ref[:, cols] = z
        if want_ssq:
            part = jnp.sum(z * z, axis=-1, keepdims=True)
            ssq = part if ssq is None else ssq + part
    return ssq


def _outproj_kernel(a_ref, w_ref, x_ref, gate_ref, o_ref):
    _residual_chunks(a_ref, w_ref, x_ref, gate_ref, o_ref, False)


def _outproj_final_kernel(a_ref, w_ref, x_ref, gate_ref, fg_ref, o_ref):
    ssq = _residual_chunks(a_ref, w_ref, x_ref, gate_ref, o_ref, True)
    inv = lax.rsqrt(ssq * (1.0 / o_ref.shape[1]) + EPS)
    o_ref[...] = o_ref[...] * inv * fg_ref[...]


def _outproj(a, w_bf16, x2d, mod, layer, seq, final_g=None):
    t, k = a.shape
    d = w_bf16.shape[1]
    tm = min(seq, 512)
    assert seq % tm == 0 and d % OUT_CHUNK == 0
    in_specs = [
        pl.BlockSpec((tm, k), lambda i: (i, 0)),
        _resident_spec(w_bf16.shape),
        pl.BlockSpec((tm, d), lambda i: (i, 0)),
        _mod_specs(layer, seq // tm, d)(2),
    ]
    args = [a, w_bf16, x2d, mod]
    kern = _outproj_kernel
    if final_g is not None:
        in_specs.append(pl.BlockSpec((1, d), lambda i: (0, 0)))
        args.append(final_g.reshape(1, d))
        kern = _outproj_final_kernel
    return pl.pallas_call(
        kern,
        grid=(t // tm,),
        in_specs=in_specs,
        out_specs=pl.BlockSpec((tm, d), lambda i: (i, 0)),
        out_shape=jax.ShapeDtypeStruct((t, d), F32),
        compiler_params=_params("parallel"),
        name="out_proj_final" if final_g is not None else "out_proj",
    )(*args)


def _mla_inproj_kernel(x_ref, g_ref, shift_ref, scale_ref, w_ref, o_ref, h_ref, *, gate_cols):
    h_ref[...] = _adaln_rmsnorm(x_ref, g_ref, shift_ref, scale_ref)
    for c0 in range(0, o_ref.shape[1], MXU_TILE):
        acc = jnp.dot(h_ref[...], w_ref[:, c0:c0 + MXU_TILE], preferred_element_type=F32)
        if gate_cols[0] <= c0 < gate_cols[1]:
            acc = _silu(acc)
        o_ref[:, c0:c0 + MXU_TILE] = acc.astype(BF16)


def _mla_inproj(x2d, norm_g, mod, layer, w_bf16, seq, gate_cols):
    t, d = x2d.shape
    n_out = w_bf16.shape[1]
    tm = min(seq, 512)
    assert seq % tm == 0 and n_out % MXU_TILE == 0
    assert gate_cols[0] % MXU_TILE == 0 and gate_cols[1] % MXU_TILE == 0
    mspec = _mod_specs(layer, seq // tm, d)
    return pl.pallas_call(
        functools.partial(_mla_inproj_kernel, gate_cols=gate_cols),
        grid=(t // tm,),
        in_specs=[
            pl.BlockSpec((tm, d), lambda i: (i, 0)),
            pl.BlockSpec((1, d), lambda i: (0, 0)),
            mspec(0), mspec(1),
            _resident_spec(w_bf16.shape),
        ],
        out_specs=pl.BlockSpec((tm, n_out), lambda i: (i, 0)),
        out_shape=jax.ShapeDtypeStruct((t, n_out), BF16),
        scratch_shapes=[pltpu.VMEM((tm, d), BF16)],
        compiler_params=_params("parallel"),
        name="mla_in_proj",
    )(x2d, norm_g.reshape(1, d), mod, mod, w_bf16)


def _mla_up_kernel(cq_ref, ckv_ref, kr_ref, qg_ref, kvg_ref, wq_ref, wkv_ref, cos_ref,
                   sin_ref, q_ref, k_ref, v_ref, *, heads, q_scale, group):
    cos = cos_ref[...]
    sin = sin_ref[...]

    def rope_pad(x):
        return x * cos + pltpu.roll(x, LANES // 2, axis=1) * sin

    def rms(x_ref, g_ref):
        x = x_ref[...].astype(F32)
        return (x * lax.rsqrt(jnp.mean(x * x, axis=-1, keepdims=True) + EPS)
                * g_ref[...]).astype(BF16)

    cqn = rms(cq_ref, qg_ref)
    ckvn = rms(ckv_ref, kvg_ref)
    kr = rope_pad(kr_ref[...].astype(F32)).astype(BF16)
    gw = group * MLA_HEAD_PAD
    for g0 in range(0, heads, group):
        q = jnp.dot(cqn, wq_ref[:, g0 * MLA_HEAD_PAD:g0 * MLA_HEAD_PAD + gw],
                    preferred_element_type=F32)
        for j in range(group):
            c0 = j * MLA_HEAD_PAD
            o0 = (g0 + j) * MLA_HEAD_PAD
            q_ref[:, o0:o0 + LANES] = (q[:, c0:c0 + LANES] * q_scale).astype(BF16)
            q_ref[:, o0 + LANES:o0 + 2 * LANES] = (
                rope_pad(q[:, c0 + LANES:c0 + 2 * LANES]) * q_scale).astype(BF16)
    nope_w = heads * MLA_NOPE
    for g0 in range(0, heads, group):
        kn = jnp.dot(ckvn, wkv_ref[:, g0 * MLA_NOPE:(g0 + group) * MLA_NOPE],
                     preferred_element_type=F32)
        for j in range(group):
            o0 = (g0 + j) * MLA_HEAD_PAD
            k_ref[:, o0:o0 + LANES] = kn[:, j * MLA_NOPE:(j + 1) * MLA_NOPE].astype(BF16)
            k_ref[:, o0 + LANES:o0 + 2 * LANES] = kr
    v_ref[...] = jnp.dot(ckvn, wkv_ref[:, nope_w:], preferred_element_type=F32).astype(BF16)


def _mla_up(proj, q_norm_g, kv_norm_g, wq_bf16, wkv_bf16, cos, sin, q_rank, kv_rank,
            kr_col, heads):
    t = proj.shape[0]
    tm = min(t, 512)
    assert t % tm == 0 and q_rank == kv_rank and kr_col % LANES == 0
    return pl.pallas_call(
        functools.partial(_mla_up_kernel, heads=heads,
                          q_scale=LOG2_E * (MLA_NOPE + MLA_ROPE) ** -0.5, group=4),
        grid=(t // tm,),
        in_specs=[
            pl.BlockSpec((tm, q_rank), lambda i: (i, 0)),
            pl.BlockSpec((tm, kv_rank), lambda i: (i, 1)),
            pl.BlockSpec((tm, LANES), lambda i: (i, kr_col // LANES)),
            pl.BlockSpec((1, q_rank), lambda i: (0, 0)),
            pl.BlockSpec((1, kv_rank), lambda i: (0, 0)),
            pl.BlockSpec(wq_bf16.shape, lambda i: (0, 0)),
            pl.BlockSpec(wkv_bf16.shape, lambda i: (0, 0)),
            pl.BlockSpec((tm, LANES), lambda i: (i, 0)),
            pl.BlockSpec((tm, LANES), lambda i: (i, 0)),
        ],
        out_specs=[
            pl.BlockSpec((tm, heads * MLA_HEAD_PAD), lambda i: (i, 0)),
            pl.BlockSpec((tm, heads * MLA_HEAD_PAD), lambda i: (i, 0)),
            pl.BlockSpec((tm, heads * MLA_V), lambda i: (i, 0)),
        ],
        out_shape=[
            jax.ShapeDtypeStruct((t, heads * MLA_HEAD_PAD), BF16),
            jax.ShapeDtypeStruct((t, heads * MLA_HEAD_PAD), BF16),
            jax.ShapeDtypeStruct((t, heads * MLA_V), BF16),
        ],
        compiler_params=_params("parallel"),
        name="mla_up_proj",
    )(proj, proj, proj, q_norm_g.reshape(1, q_rank), kv_norm_g.reshape(1, kv_rank),
      wq_bf16, wkv_bf16, cos, sin)


def _attn_kernel(q_ref, k_ref, v_ref, sg_ref, o_ref, vt_ref, s_ref, acc_ref, m_ref, *, tk):
    tq = 2 * tk
    nq = q_ref.shape[0] // tq
    all_q = slice(0, tq)
    lower_q = slice(0, tk)
    upper_q = slice(tk, tq)

    for j in range(vt_ref.shape[0]):
        vt_ref[j, :MLA_V, :] = v_ref[j * tk:(j + 1) * tk, :].astype(F32).T.astype(BF16)
        vt_ref[j, MLA_V:, :] = jnp.ones((ONES_ROWS, tk), BF16)

    def scores(t, q0, width):
        kt = k_ref[pl.ds(pl.multiple_of(t * tk, tk), tk), :]
        qs = q_ref[pl.ds(pl.multiple_of(q0, tk), width), :]
        return lax.dot_general(kt, qs, (((1,), (1,)), ((), ())), preferred_element_type=F32)

    def consume(t, slot, qs, diagonal):
        st = s_ref[slot, :, qs]
        if diagonal:
            key = lax.broadcasted_iota(jnp.int32, st.shape, 0)
            qry = lax.broadcasted_iota(jnp.int32, st.shape, 1)
            st = jnp.where(key <= qry, st, NEG_BIG)
        m_old = m_ref[:, qs]
        m_new = jnp.maximum(m_old, jnp.max(st, axis=0, keepdims=True))
        p = jnp.exp2(st - m_new).astype(BF16)
        acc_ref[:, qs] = jnp.exp2(m_old - m_new) * acc_ref[:, qs] + jnp.dot(
            vt_ref[t], p, preferred_element_type=F32)
        m_ref[:, qs] = m_new

    def finish(q0, qs):
        acc = acc_ref[:, qs]
        o = (acc[:MLA_V, :] * (1.0 / acc[MLA_V:MLA_V + 1, :])).T
        rows = pl.ds(pl.multiple_of(q0 + qs.start, tk), qs.stop - qs.start)
        o_ref[rows, :] = (sg_ref[rows, :].astype(F32) * o).astype(BF16)

    s_ref[0] = scores(0, 0, tq)

    def query_block(qi, carry):
        q0 = qi * tq
        m_ref[...] = jnp.full(m_ref.shape, NEG_BIG, F32)
        acc_ref[...] = jnp.zeros_like(acc_ref)

        def pair(u, c):
            t = 2 * u
            s_ref[1] = scores(t + 1, q0, tq)
            consume(t, 0, all_q, False)
            s_ref[0] = scores(t + 2, q0, tq)
            consume(t + 1, 1, all_q, False)
            return c

        lax.fori_loop(0, qi, pair, 0)
        s_ref[1, :, upper_q] = scores(2 * qi + 1, q0 + tk, tk)
        consume(2 * qi, 0, all_q, True)
        finish(q0, lower_q)
        consume(2 * qi + 1, 1, upper_q, True)
        s_ref[0] = scores(0, jnp.minimum(qi + 1, nq - 1) * tq, tq)
        finish(q0, upper_q)
        return carry

    lax.fori_loop(0, nq, query_block, 0)


def _attention(q, k, v, proj, gate_col, batch, seq, heads):
    t = q.shape[0]
    tk = min(seq // 2, 512)
    assert seq % (2 * tk) == 0 and gate_col % MLA_V == 0 and tk % LANES == 0
    g_blk0 = gate_col // MLA_V
    return pl.pallas_call(
        functools.partial(_attn_kernel, tk=tk),
        grid=(batch, heads),
        in_specs=[
            pl.BlockSpec((seq, MLA_HEAD_PAD), lambda b, h: (b, h)),
            pl.BlockSpec((seq, MLA_HEAD_PAD), lambda b, h: (b, h)),
            pl.BlockSpec((seq, MLA_V), lambda b, h: (b, h)),
            pl.BlockSpec((seq, MLA_V), lambda b, h: (b, g_blk0 + h)),
        ],
        out_specs=pl.BlockSpec((seq, MLA_V), lambda b, h: (b, h)),
        out_shape=jax.ShapeDtypeStruct((t, heads * MLA_V), BF16),
        scratch_shapes=[
            pltpu.VMEM((seq // tk, MLA_V + ONES_ROWS, tk), BF16),
            pltpu.VMEM((2, tk, 2 * tk), F32),
            pltpu.VMEM((MLA_V + ONES_ROWS, 2 * tk), F32),
            pltpu.VMEM((1, 2 * tk), F32),
        ],
        compiler_params=_params("parallel", "parallel"),
        name="mla_attention",
    )(q, k, v, proj)


def _rotate_half_cols(w):
    half = w.shape[-1] // 2
    return jnp.concatenate([-w[..., half:], w[..., :half]], axis=-1)


def _mla_weights(w_in, w_uq, w_ukv, q_rank, kv_rank, heads):
    d = w_in.shape[0]
    cq = w_in[:, :q_rank]
    ckv = w_in[:, q_rank:q_rank + kv_rank]
    kr = w_in[:, q_rank + kv_rank:q_rank + kv_rank + MLA_ROPE]
    gate = w_in[:, q_rank + kv_rank + MLA_ROPE:]
    gate_col = q_rank + kv_rank
    kr_col = gate_col + gate.shape[1]
    pad = jnp.zeros((d, -(kr_col + 2 * MLA_ROPE) % MXU_TILE), w_in.dtype)
    w_in_p = jnp.concatenate([cq, ckv, gate, kr, _rotate_half_cols(kr), pad],
                             axis=1).astype(BF16)
    uq = w_uq.reshape(q_rank, heads, MLA_NOPE + MLA_ROPE)
    rope = uq[..., MLA_NOPE:]
    uq_p = jnp.concatenate([uq[..., :MLA_NOPE], rope, _rotate_half_cols(rope)], axis=-1)
    uq_p = uq_p.reshape(q_rank, heads * MLA_HEAD_PAD).astype(BF16)
    ukv = w_ukv.reshape(kv_rank, heads, MLA_NOPE + MLA_V)
    ukv_p = jnp.concatenate([ukv[..., :MLA_NOPE].reshape(kv_rank, heads * MLA_NOPE),
                             ukv[..., MLA_NOPE:].reshape(kv_rank, heads * MLA_V)],
                            axis=1).astype(BF16)
    return w_in_p, uq_p, ukv_p, gate_col, kr_col


def kernel(x, c, positions, ada_w, ada_b, norm_g, ret_w_in, ret_gn_g, ret_w_out, mla_w_in,
           mla_q_norm_g, mla_w_uq, mla_kv_norm_g, mla_w_ukv, mla_w_out, final_norm_g):
    batch, seq, d = x.shape
    t = batch * seq
    depth = ada_w.shape[0]
    assert depth % 2 == 0, "the last layer must be a latent-attention layer (final norm fusion)"
    ret_dk = d // RET_HEADS
    ret_dv = ret_w_out.shape[1] // RET_HEADS
    q_rank = mla_q_norm_g.shape[1]
    kv_rank = mla_kv_norm_g.shape[1]

    mod = _modulation(c, ada_w, ada_b)

    pos = positions.reshape(t, 1).astype(F32)
    cos_r, sin_r = _rope_tables(jnp.broadcast_to(pos, (t, LANES)),
                                _inv_freq(ret_dk).reshape(1, LANES))
    n_freq = MLA_ROPE // 2
    per_row = LANES // n_freq
    pos4 = jnp.repeat(positions.reshape(t // per_row, per_row).astype(F32), n_freq, axis=1)
    cos4, sin4 = _rope_tables(pos4, jnp.tile(_inv_freq(MLA_ROPE), per_row).reshape(1, LANES))
    pad = jnp.zeros((t, LANES - MLA_ROPE), F32)
    cos_m = jnp.concatenate([cos4.reshape(t, n_freq)] * 2 + [pad], axis=1)
    sin_m = jnp.concatenate([sin4.reshape(t, n_freq)] * 2 + [pad], axis=1)

    x2d = x.reshape(t, d)
    for layer in range(depth):
        j = layer // 2
        if layer % 2 == 0:
            proj = _ret_inproj(x2d, norm_g[layer], mod, layer, cos_r, sin_r,
                               ret_w_in[j].astype(BF16), seq, 2 * d,
                               2 * d + RET_HEADS * ret_dv, ret_dk)
            y = _retention(proj, ret_gn_g[j], batch, seq, RET_HEADS, ret_dk, ret_dv)
            x2d = _outproj(y, ret_w_out[j].astype(BF16), x2d, mod, layer, seq)
        else:
            w_in_p, uq_p, ukv_p, gate_col, kr_col = _mla_weights(
                mla_w_in[j], mla_w_uq[j], mla_w_ukv[j], q_rank, kv_rank, MLA_HEADS)
            proj = _mla_inproj(x2d, norm_g[layer], mod, layer, w_in_p, seq,
                               (gate_col, kr_col))
            q, k, v = _mla_up(proj, mla_q_norm_g[j], mla_kv_norm_g[j], uq_p, ukv_p,
                              cos_m, sin_m, q_rank, kv_rank, kr_col, MLA_HEADS)
            o = _attention(q, k, v, proj, gate_col, batch, seq, MLA_HEADS)
            final_g = final_norm_g if layer == depth - 1 else None
            x2d = _outproj(o, mla_w_out[j].astype(BF16), x2d, mod, layer, seq, final_g)
    return x2d.reshape(batch, seq, d)
```

```python
import functools

import jax
import jax.numpy as jnp
import numpy as np
from jax import lax
from jax.experimental import pallas as pl
from jax.experimental.pallas import tpu as pltpu

F32 = jnp.float32
BF16 = jnp.bfloat16

EPS = 1e-6
ROPE_BASE = 10000.0
NEG_BIG = -1e30
LOG2_E = 1.4426950408889634

LANES = 128
SUBLANES = 8
MXU_TILE = 256
OUT_CHUNK = 2 * MXU_TILE
VMEM_LIMIT_BYTES = 56 * 2**20

RET_HEADS = 8
RET_CHUNK = 128
MLA_HEADS = 16
MLA_NOPE = 128
MLA_ROPE = 64
MLA_V = 128
MLA_HEAD_PAD = 256
ONES_ROWS = 16


def _params(*semantics):
    return pltpu.CompilerParams(dimension_semantics=semantics,
                                vmem_limit_bytes=VMEM_LIMIT_BYTES)


def _silu(x):
    return x * jax.nn.sigmoid(x)


def _mod_kernel(c_ref, w_ref, b_ref, o_ref):
    act = _silu(c_ref[...]).astype(BF16)
    o_ref[...] = jnp.dot(act, w_ref[...].astype(BF16),
                         preferred_element_type=F32) + b_ref[...]


def _modulation(c, ada_w, ada_b):
    depth, d, n3 = ada_w.shape
    b = c.shape[0]
    assert b <= SUBLANES
    tn = 768
    assert n3 % tn == 0
    c_pad = jnp.zeros((SUBLANES, d), F32).at[:b].set(c)
    out = pl.pallas_call(
        _mod_kernel,
        grid=(depth, n3 // tn),
        in_specs=[
            pl.BlockSpec((SUBLANES, d), lambda i, n: (0, 0)),
            pl.BlockSpec((None, d, tn), lambda i, n: (i, 0, n)),
            pl.BlockSpec((None, 1, tn), lambda i, n: (i, 0, n)),
        ],
        out_specs=pl.BlockSpec((None, SUBLANES, tn), lambda i, n: (i, 0, n)),
        out_shape=jax.ShapeDtypeStruct((depth, SUBLANES, n3), F32),
        compiler_params=_params("parallel", "parallel"),
        name="adaln_modulation",
    )(c_pad, ada_w, ada_b.reshape(depth, 1, n3))
    return out[:, :b].reshape(depth, b, 1, n3)


def _rope_table_kernel(pos_ref, freq_ref, cos_ref, sin_ref):
    ang = pos_ref[...] * freq_ref[...]
    cos_ref[...] = jnp.cos(ang)
    sin_ref[...] = jnp.sin(ang)


def _rope_tables(pos_lanes, freq_lanes):
    rows = pos_lanes.shape[0]
    tr = min(rows, 1024)
    assert rows % tr == 0
    spec = pl.BlockSpec((tr, LANES), lambda i: (i, 0))
    return pl.pallas_call(
        _rope_table_kernel,
        grid=(rows // tr,),
        in_specs=[spec, pl.BlockSpec((1, LANES), lambda i: (0, 0))],
        out_specs=[spec, spec],
        out_shape=[jax.ShapeDtypeStruct((rows, LANES), F32)] * 2,
        compiler_params=_params("parallel"),
        name="rope_tables",
    )(pos_lanes, freq_lanes)


def _inv_freq(d):
    return ROPE_BASE ** (-jnp.arange(0, d, 2, dtype=F32) / d)


def _adaln_rmsnorm(x_ref, g_ref, shift_ref, scale_ref):
    x = x_ref[...]
    y = x * lax.rsqrt(jnp.mean(x * x, axis=-1, keepdims=True) + EPS)
    return (y * (g_ref[...] * (1.0 + scale_ref[...])) + shift_ref[...]).astype(BF16)


def _mod_specs(layer, rows_per_batch_block, d):
    def spec(part):
        return pl.BlockSpec((None, None, 1, d),
                            lambda i, *_: (layer, i // rows_per_batch_block, 0, part))
    return spec


def _resident_spec(shape):
    return pl.BlockSpec(shape, lambda *_: (0,) * len(shape), pipeline_mode=pl.Buffered(1))


def _ret_inproj_kernel(x_ref, g_ref, shift_ref, scale_ref, cos_ref, sin_ref, w_ref,
                       o_ref, h_ref, *, rope_blocks, gate_block0, head_dim):
    n = pl.program_id(1)

    @pl.when(n == 0)
    def _():
        h_ref[...] = _adaln_rmsnorm(x_ref, g_ref, shift_ref, scale_ref)

    half = head_dim // 2
    heads = range(0, o_ref.shape[1], head_dim)

    def project(c0):
        return jnp.dot(h_ref[...], w_ref[:, c0:c0 + head_dim], preferred_element_type=F32)

    @pl.when(n < rope_blocks)
    def _():
        for c0 in heads:
            acc = project(c0)
            x1 = acc[:, :half]
            x2 = acc[:, half:]
            cos = cos_ref[...]
            sin = sin_ref[...]
            o_ref[:, c0:c0 + half] = (x1 * cos - x2 * sin).astype(BF16)
            o_ref[:, c0 + half:c0 + head_dim] = (x2 * cos + x1 * sin).astype(BF16)

    @pl.when((n >= rope_blocks) & (n < gate_block0))
    def _():
        for c0 in heads:
            o_ref[:, c0:c0 + head_dim] = project(c0).astype(BF16)

    @pl.when(n >= gate_block0)
    def _():
        for c0 in heads:
            o_ref[:, c0:c0 + head_dim] = _silu(project(c0)).astype(BF16)


def _ret_inproj(x2d, norm_g, mod, layer, cos, sin, w_bf16, seq, rope_cols, gate_col, head_dim):
    t, d = x2d.shape
    n_out = w_bf16.shape[1]
    tm = min(seq, 1024)
    tn = 2048
    assert seq % tm == 0 and n_out % tn == 0 and rope_cols % tn == 0 and tn % head_dim == 0
    assert head_dim // 2 == LANES and gate_col % tn == 0
    mspec = _mod_specs(layer, seq // tm, d)
    return pl.pallas_call(
        functools.partial(_ret_inproj_kernel, rope_blocks=rope_cols // tn,
                          gate_block0=gate_col // tn, head_dim=head_dim),
        grid=(t // tm, n_out // tn),
        in_specs=[
            pl.BlockSpec((tm, d), lambda i, n: (i, 0)),
            pl.BlockSpec((1, d), lambda i, n: (0, 0)),
            mspec(0), mspec(1),
            pl.BlockSpec((tm, LANES), lambda i, n: (i, 0)),
            pl.BlockSpec((tm, LANES), lambda i, n: (i, 0)),
            pl.BlockSpec((d, tn), lambda i, n: (0, n)),
        ],
        out_specs=pl.BlockSpec((tm, tn), lambda i, n: (i, n)),
        out_shape=jax.ShapeDtypeStruct((t, n_out), BF16),
        scratch_shapes=[pltpu.VMEM((tm, d), BF16)],
        compiler_params=_params("parallel", "arbitrary"),
        name="ret_in_proj",
    )(x2d, norm_g.reshape(1, d), mod, mod, cos, sin, w_bf16)


def _retention_kernel(decay_ref, q_ref, k_ref, v_ref, sg_ref, gn_ref, dmat_ref, xi_ref,
                      zeta_ref, o_ref, state_ref, *, chunk):
    @pl.when(pl.program_id(2) == 0)
    def _():
        state_ref[...] = jnp.zeros_like(state_ref)

    decay = decay_ref[pl.program_id(1)]
    for c in range(q_ref.shape[0] // chunk):
        rows = slice(c * chunk, (c + 1) * chunk)
        q = q_ref[rows, :]
        k = k_ref[rows, :]
        v = v_ref[rows, :]
        s = lax.dot_general(q, k, (((1,), (1,)), ((), ())), preferred_element_type=F32)
        kz = (k.astype(F32) * zeta_ref[...]).astype(BF16)
        update = lax.dot_general(kz, v, (((0,), (0,)), ((), ())), preferred_element_type=F32)
        s = (s * dmat_ref[...]).astype(BF16)
        state = state_ref[...]
        y = (jnp.dot(s, v, preferred_element_type=F32)
             + jnp.dot(q, state.astype(BF16), preferred_element_type=F32) * xi_ref[...])
        state_ref[...] = state * decay + update
        yc = y - jnp.mean(y, axis=-1, keepdims=True)
        var = jnp.mean(yc * yc, axis=-1, keepdims=True)
        yn = yc * lax.rsqrt(var + EPS) * gn_ref[...]
        o_ref[rows, :] = (sg_ref[rows, :].astype(F32) * yn).astype(BF16)


def _retention_constants(heads, chunk, dk, dv):
    f32 = np.float32
    log_gamma = np.log1p(-np.exp2(f32(-5.0) - np.arange(heads, dtype=f32))).astype(f32)
    idx = np.arange(chunk, dtype=f32)
    diff = idx[:, None] - idx[None, :]
    k_scale = f32(dk ** -0.5)
    d_intra = np.where(diff >= 0, np.exp(log_gamma[:, None, None] * np.maximum(diff, 0)),
                       0).astype(f32) * k_scale
    xi = np.exp(log_gamma[:, None] * (idx + 1)).astype(f32)
    zeta = np.exp(log_gamma[:, None] * (chunk - 1 - idx)).astype(f32) * k_scale
    chunk_decay = np.exp(log_gamma * chunk).astype(f32)
    bcast = lambda a, n: np.ascontiguousarray(np.broadcast_to(a[:, :, None], (heads, chunk, n)))
    return d_intra, bcast(xi, dv), bcast(zeta, dk), chunk_decay


def _retention(proj, gn_g, batch, seq, heads, dk, dv):
    t = proj.shape[0]
    width = heads * dv
    chunk = RET_CHUNK
    sb = min(seq, 1024)
    assert seq % sb == 0 and sb % chunk == 0
    nsb = seq // sb
    d_intra, xi, zeta, chunk_decay = _retention_constants(heads, chunk, dk, dv)
    k_blk0 = heads
    v_blk0 = 2 * heads * dk // dv
    g_blk0 = v_blk0 + heads
    row = lambda b, h, s: b * nsb + s
    return pl.pallas_call(
        functools.partial(_retention_kernel, chunk=chunk),
        grid=(batch, heads, nsb),
        in_specs=[
            pl.BlockSpec(memory_space=pltpu.SMEM),
            pl.BlockSpec((sb, dk), lambda b, h, s: (row(b, h, s), h)),
            pl.BlockSpec((sb, dk), lambda b, h, s: (row(b, h, s), k_blk0 + h)),
            pl.BlockSpec((sb, dv), lambda b, h, s: (row(b, h, s), v_blk0 + h)),
            pl.BlockSpec((sb, dv), lambda b, h, s: (row(b, h, s), g_blk0 + h)),
            pl.BlockSpec((1, dv), lambda b, h, s: (0, h)),
            pl.BlockSpec((None, chunk, chunk), lambda b, h, s: (h, 0, 0)),
            pl.BlockSpec((None, chunk, dv), lambda b, h, s: (h, 0, 0)),
            pl.BlockSpec((None, chunk, dk), lambda b, h, s: (h, 0, 0)),
        ],
        out_specs=pl.BlockSpec((sb, dv), lambda b, h, s: (row(b, h, s), h)),
        out_shape=jax.ShapeDtypeStruct((t, width), BF16),
        scratch_shapes=[pltpu.VMEM((dk, dv), F32)],
        compiler_params=_params("parallel", "parallel", "arbitrary"),
        name="retention",
    )(chunk_decay, proj, proj, proj, proj, gn_g.reshape(1, width), d_intra, xi, zeta)


def _residual_chunks(a_ref, w_ref, x_ref, gate_ref, o_ref, want_ssq):
    ssq = None
    for c0 in range(0, o_ref.shape[1], OUT_CHUNK):
        cols = slice(c0, c0 + OUT_CHUNK)
        y = jnp.dot(a_ref[...], w_ref[:, cols], preferred_element_type=F32)
        z = x_ref[:, cols] + gate_ref[:, cols] * y
        o_ref[:, cols] = z
        if want_ssq:
            part = jnp.sum(z * z, axis=-1, keepdims=True)
            ssq = part if ssq is None else ssq + part
    return ssq


def _outproj_kernel(a_ref, w_ref, x_ref, gate_ref, o_ref):
    _residual_chunks(a_ref, w_ref, x_ref, gate_ref, o_ref, False)


def _outproj_final_kernel(a_ref, w_ref, x_ref, gate_ref, fg_ref, o_ref):
    ssq = _residual_chunks(a_ref, w_ref, x_ref, gate_ref, o_ref, True)
    inv = lax.rsqrt(ssq * (1.0 / o_ref.shape[1]) + EPS)
    o_ref[...] = o_ref[...] * inv * fg_ref[...]


def _outproj(a, w_bf16, x2d, mod, layer, seq, final_g=None):
    t, k = a.shape
    d = w_bf16.shape[1]
    tm = min(seq, 512)
    assert seq % tm == 0 and d % OUT_CHUNK == 0
    in_specs = [
        pl.BlockSpec((tm, k), lambda i: (i, 0)),
        _resident_spec(w_bf16.shape),
        pl.BlockSpec((tm, d), lambda i: (i, 0)),
        _mod_specs(layer, seq // tm, d)(2),
    ]
    args = [a, w_bf16, x2d, mod]
    kern = _outproj_kernel
    if final_g is not None:
        in_specs.append(pl.BlockSpec((1, d), lambda i: (0, 0)))
        args.append(final_g.reshape(1, d))
        kern = _outproj_final_kernel
    return pl.pallas_call(
        kern,
        grid=(t // tm,),
        in_specs=in_specs,
        out_specs=pl.BlockSpec((tm, d), lambda i: (i, 0)),
        out_shape=jax.ShapeDtypeStruct((t, d), F32),
        compiler_params=_params("parallel"),
        name="out_proj_final" if final_g is not None else "out_proj",
    )(*args)


def _mla_inproj_kernel(x_ref, g_ref, shift_ref, scale_ref, w_ref, o_ref, h_ref, *, gate_cols):
    h_ref[...] = _adaln_rmsnorm(x_ref, g_ref, shift_ref, scale_ref)
    for c0 in range(0, o_ref.shape[1], MXU_TILE):
        acc = jnp.dot(h_ref[...], w_ref[:, c0:c0 + MXU_TILE], preferred_element_type=F32)
        if gate_cols[0] <= c0 < gate_cols[1]:
            acc = _silu(acc)
        o_ref[:, c0:c0 + MXU_TILE] = acc.astype(BF16)


def _mla_inproj(x2d, norm_g, mod, layer, w_bf16, seq, gate_cols):
    t, d = x2d.shape
    n_out = w_bf16.shape[1]
    tm = min(seq, 512)
    assert seq % tm == 0 and n_out % MXU_TILE == 0
    assert gate_cols[0] % MXU_TILE == 0 and gate_cols[1] % MXU_TILE == 0
    mspec = _mod_specs(layer, seq // tm, d)
    return pl.pallas_call(
        functools.partial(_mla_inproj_kernel, gate_cols=gate_cols),
        grid=(t // tm,),
        in_specs=[
            pl.BlockSpec((tm, d), lambda i: (i, 0)),
            pl.BlockSpec((1, d), lambda i: (0, 0)),
            mspec(0), mspec(1),
            _resident_spec(w_bf16.shape),
        ],
        out_specs=pl.BlockSpec((tm, n_out), lambda i: (i, 0)),
        out_shape=jax.ShapeDtypeStruct((t, n_out), BF16),
        scratch_shapes=[pltpu.VMEM((tm, d), BF16)],
        compiler_params=_params("parallel"),
        name="mla_in_proj",
    )(x2d, norm_g.reshape(1, d), mod, mod, w_bf16)


def _mla_up_kernel(cq_ref, ckv_ref, kr_ref, qg_ref, kvg_ref, wqt_ref, wkv_ref, cos_ref,
                   sin_ref, cost_ref, sint_ref, qt_ref, k_ref, v_ref, *, heads, q_scale, group):
    cos = cos_ref[...]
    sin = sin_ref[...]

    def rope_pad(x):
        return x * cos + pltpu.roll(x, LANES // 2, axis=1) * sin

    def rms(x_ref, g_ref):
        x = x_ref[...].astype(F32)
        return x * lax.rsqrt(jnp.mean(x * x, axis=-1, keepdims=True) + EPS) * g_ref[...]

    cqn_t = rms(cq_ref, qg_ref).T.astype(BF16)
    cos_t = cost_ref[...]
    sin_t = sint_ref[...]
    gw = group * MLA_HEAD_PAD
    for g0 in range(0, heads, group):
        qt = jnp.dot(wqt_ref[g0 * MLA_HEAD_PAD:g0 * MLA_HEAD_PAD + gw, :], cqn_t,
                     preferred_element_type=F32)
        for j in range(group):
            r0 = j * MLA_HEAD_PAD
            o0 = (g0 + j) * MLA_HEAD_PAD
            r1 = r0 + MLA_NOPE
            qt_ref[o0:o0 + MLA_NOPE, :] = (qt[r0:r1, :] * q_scale).astype(BF16)
            rope = (qt[r1:r1 + MLA_ROPE, :] * cos_t
                    + qt[r1 + MLA_ROPE:r1 + 2 * MLA_ROPE, :] * sin_t)
            qt_ref[o0 + MLA_NOPE:o0 + MLA_NOPE + MLA_ROPE, :] = (rope * q_scale).astype(BF16)
            qt_ref[o0 + MLA_NOPE + MLA_ROPE:o0 + MLA_HEAD_PAD, :] = jnp.zeros(
                (MLA_HEAD_PAD - MLA_NOPE - MLA_ROPE, qt.shape[1]), BF16)

    ckvn = rms(ckv_ref, kvg_ref).astype(BF16)
    kr = rope_pad(kr_ref[...].astype(F32)).astype(BF16)
    nope_w = heads * MLA_NOPE
    for g0 in range(0, heads, group):
        kn = jnp.dot(ckvn, wkv_ref[:, g0 * MLA_NOPE:(g0 + group) * MLA_NOPE],
                     preferred_element_type=F32)
        for j in range(group):
            o0 = (g0 + j) * MLA_HEAD_PAD
            k_ref[:, o0:o0 + LANES] = kn[:, j * MLA_NOPE:(j + 1) * MLA_NOPE].astype(BF16)
            k_ref[:, o0 + LANES:o0 + 2 * LANES] = kr
    v_ref[...] = jnp.dot(ckvn, wkv_ref[:, nope_w:], preferred_element_type=F32).astype(BF16)


def _mla_up(proj, q_norm_g, kv_norm_g, wqt_bf16, wkv_bf16, cos, sin, cos_t, sin_t, q_rank,
            kv_rank, kr_col, heads, tm):
    t = proj.shape[0]
    assert t % tm == 0 and q_rank == kv_rank and kr_col % LANES == 0
    return pl.pallas_call(
        functools.partial(_mla_up_kernel, heads=heads,
                          q_scale=LOG2_E * (MLA_NOPE + MLA_ROPE) ** -0.5, group=4),
        grid=(t // tm,),
        in_specs=[
            pl.BlockSpec((tm, q_rank), lambda i: (i, 0)),
            pl.BlockSpec((tm, kv_rank), lambda i: (i, 1)),
            pl.BlockSpec((tm, LANES), lambda i: (i, kr_col // LANES)),
            pl.BlockSpec((1, q_rank), lambda i: (0, 0)),
            pl.BlockSpec((1, kv_rank), lambda i: (0, 0)),
            _resident_spec(wqt_bf16.shape),
            _resident_spec(wkv_bf16.shape),
            pl.BlockSpec((tm, LANES), lambda i: (i, 0)),
            pl.BlockSpec((tm, LANES), lambda i: (i, 0)),
            pl.BlockSpec((MLA_ROPE, tm), lambda i: (0, i)),
            pl.BlockSpec((MLA_ROPE, tm), lambda i: (0, i)),
        ],
        out_specs=[
            pl.BlockSpec((None, heads * MLA_HEAD_PAD, tm), lambda i: (i, 0, 0)),
            pl.BlockSpec((tm, heads * MLA_HEAD_PAD), lambda i: (i, 0)),
            pl.BlockSpec((tm, heads * MLA_V), lambda i: (i, 0)),
        ],
        out_shape=[
            jax.ShapeDtypeStruct((t // tm, heads * MLA_HEAD_PAD, tm), BF16),
            jax.ShapeDtypeStruct((t, heads * MLA_HEAD_PAD), BF16),
            jax.ShapeDtypeStruct((t, heads * MLA_V), BF16),
        ],
        compiler_params=_params("parallel"),
        name="mla_up_proj",
    )(proj, proj, proj, q_norm_g.reshape(1, q_rank), kv_norm_g.reshape(1, kv_rank),
      wqt_bf16, wkv_bf16, cos, sin, cos_t, sin_t)


def _attn_kernel(qt_ref, k_ref, v_ref, sg_ref, o_ref, vt_ref, s_ref, acc_ref, m_ref, *, tk):
    tq = 2 * tk
    nq = qt_ref.shape[0] // 2
    all_q = slice(0, tq)
    lower_q = slice(0, tk)
    upper_q = slice(tk, tq)

    for j in range(vt_ref.shape[0]):
        vt_ref[j, :MLA_V, :] = v_ref[j * tk:(j + 1) * tk, :].astype(F32).T.astype(BF16)
        vt_ref[j, MLA_V:, :] = jnp.ones((ONES_ROWS, tk), BF16)

    def scores(t, slab):
        kt = k_ref[pl.ds(pl.multiple_of(t * tk, tk), tk), :]
        return jnp.dot(kt, qt_ref[slab], preferred_element_type=F32)

    def put_scores(slot, t, qi):
        s_ref[slot, :, lower_q] = scores(t, 2 * qi)
        s_ref[slot, :, upper_q] = scores(t, 2 * qi + 1)

    def consume(t, slot, qs, diagonal):
        st = s_ref[slot, :, qs]
        if diagonal:
            key = lax.broadcasted_iota(jnp.int32, st.shape, 0)
            qry = lax.broadcasted_iota(jnp.int32, st.shape, 1)
            st = jnp.where(key <= qry, st, NEG_BIG)
        m_old = m_ref[:, qs]
        m_new = jnp.maximum(m_old, jnp.max(st, axis=0, keepdims=True))
        p = jnp.exp2(st - m_new).astype(BF16)
        acc_ref[:, qs] = jnp.exp2(m_old - m_new) * acc_ref[:, qs] + jnp.dot(
            vt_ref[t], p, preferred_element_type=F32)
        m_ref[:, qs] = m_new

    def finish(q0, qs):
        acc = acc_ref[:, qs]
        o = (acc[:MLA_V, :] * (1.0 / acc[MLA_V:MLA_V + 1, :])).T
        rows = pl.ds(pl.multiple_of(q0 + qs.start, tk), qs.stop - qs.start)
        o_ref[rows, :] = (sg_ref[rows, :].astype(F32) * o).astype(BF16)

    put_scores(0, 0, 0)

    def query_block(qi, carry):
        q0 = qi * tq
        m_ref[...] = jnp.full(m_ref.shape, NEG_BIG, F32)
        acc_ref[...] = jnp.zeros_like(acc_ref)

        def pair(u, c):
            t = 2 * u
            put_scores(1, t + 1, qi)
            consume(t, 0, all_q, False)
            put_scores(0, t + 2, qi)
            consume(t + 1, 1, all_q, False)
            return c

        lax.fori_loop(0, qi, pair, 0)
        s_ref[1, :, upper_q] = scores(2 * qi + 1, 2 * qi + 1)
        consume(2 * qi, 0, all_q, True)
        finish(q0, lower_q)
        consume(2 * qi + 1, 1, upper_q, True)
        put_scores(0, 0, jnp.minimum(qi + 1, nq - 1))
        finish(q0, upper_q)
        return carry

    lax.fori_loop(0, nq, query_block, 0)


def _attention(qt, k, v, proj, gate_col, batch, seq, heads, tk):
    t = k.shape[0]
    assert seq % (2 * tk) == 0 and gate_col % MLA_V == 0 and tk % LANES == 0
    g_blk0 = gate_col // MLA_V
    return pl.pallas_call(
        functools.partial(_attn_kernel, tk=tk),
        grid=(batch, heads),
        in_specs=[
            pl.BlockSpec((seq // tk, MLA_HEAD_PAD, tk), lambda b, h: (b, h, 0)),
            pl.BlockSpec((seq, MLA_HEAD_PAD), lambda b, h: (b, h)),
            pl.BlockSpec((seq, MLA_V), lambda b, h: (b, h)),
            pl.BlockSpec((seq, MLA_V), lambda b, h: (b, g_blk0 + h)),
        ],
        out_specs=pl.BlockSpec((seq, MLA_V), lambda b, h: (b, h)),
        out_shape=jax.ShapeDtypeStruct((t, heads * MLA_V), BF16),
        scratch_shapes=[
            pltpu.VMEM((seq // tk, MLA_V + ONES_ROWS, tk), BF16),
            pltpu.VMEM((2, tk, 2 * tk), F32),
            pltpu.VMEM((MLA_V + ONES_ROWS, 2 * tk), F32),
            pltpu.VMEM((1, 2 * tk), F32),
        ],
        compiler_params=_params("parallel", "parallel"),
        name="mla_attention",
    )(qt, k, v, proj)


def _rotate_half_cols(w):
    half = w.shape[-1] // 2
    return jnp.concatenate([-w[..., half:], w[..., :half]], axis=-1)


def _mla_weights(w_in, w_uq, w_ukv, q_rank, kv_rank, heads):
    d = w_in.shape[0]
    cq = w_in[:, :q_rank]
    ckv = w_in[:, q_rank:q_rank + kv_rank]
    kr = w_in[:, q_rank + kv_rank:q_rank + kv_rank + MLA_ROPE]
    gate = w_in[:, q_rank + kv_rank + MLA_ROPE:]
    gate_col = q_rank + kv_rank
    kr_col = gate_col + gate.shape[1]
    pad = jnp.zeros((d, -(kr_col + 2 * MLA_ROPE) % MXU_TILE), w_in.dtype)
    w_in_p = jnp.concatenate([cq, ckv, gate, kr, _rotate_half_cols(kr), pad],
                             axis=1).astype(BF16)
    uq = w_uq.reshape(q_rank, heads, MLA_NOPE + MLA_ROPE)
    rope = uq[..., MLA_NOPE:]
    uq_p = jnp.concatenate([uq[..., :MLA_NOPE], rope, _rotate_half_cols(rope)], axis=-1)
    uq_p = uq_p.reshape(q_rank, heads * MLA_HEAD_PAD).T.astype(BF16)
    ukv = w_ukv.reshape(kv_rank, heads, MLA_NOPE + MLA_V)
    ukv_p = jnp.concatenate([ukv[..., :MLA_NOPE].reshape(kv_rank, heads * MLA_NOPE),
                             ukv[..., MLA_NOPE:].reshape(kv_rank, heads * MLA_V)],
                            axis=1).astype(BF16)
    return w_in_p, uq_p, ukv_p, gate_col, kr_col


def kernel(x, c, positions, ada_w, ada_b, norm_g, ret_w_in, ret_gn_g, ret_w_out, mla_w_in,
           mla_q_norm_g, mla_w_uq, mla_kv_norm_g, mla_w_ukv, mla_w_out, final_norm_g):
    batch, seq, d = x.shape
    t = batch * seq
    depth = ada_w.shape[0]
    assert depth % 2 == 0, "the last layer must be a latent-attention layer (final norm fusion)"
    ret_dk = d // RET_HEADS
    ret_dv = ret_w_out.shape[1] // RET_HEADS
    q_rank = mla_q_norm_g.shape[1]
    kv_rank = mla_kv_norm_g.shape[1]

    mod = _modulation(c, ada_w, ada_b)

    pos = positions.reshape(t, 1).astype(F32)
    cos_r, sin_r = _rope_tables(jnp.broadcast_to(pos, (t, LANES)),
                                _inv_freq(ret_dk).reshape(1, LANES))
    n_freq = MLA_ROPE // 2
    per_row = LANES // n_freq
    pos4 = jnp.repeat(positions.reshape(t // per_row, per_row).astype(F32), n_freq, axis=1)
    cos4, sin4 = _rope_tables(pos4, jnp.tile(_inv_freq(MLA_ROPE), per_row).reshape(1, LANES))
    pad = jnp.zeros((t, LANES - MLA_ROPE), F32)
    cos_tok = cos4.reshape(t, n_freq)
    sin_tok = sin4.reshape(t, n_freq)
    cos_m = jnp.concatenate([cos_tok, cos_tok, pad], axis=1)
    sin_m = jnp.concatenate([sin_tok, sin_tok, pad], axis=1)
    cos_t = jnp.concatenate([cos_tok.T, cos_tok.T], axis=0)
    sin_t = jnp.concatenate([sin_tok.T, sin_tok.T], axis=0)
    attn_tk = min(seq // 2, 512)

    x2d = x.reshape(t, d)
    for layer in range(depth):
        j = layer // 2
        if layer % 2 == 0:
            proj = _ret_inproj(x2d, norm_g[layer], mod, layer, cos_r, sin_r,
                               ret_w_in[j].astype(BF16), seq, 2 * d,
                               2 * d + RET_HEADS * ret_dv, ret_dk)
            y = _retention(proj, ret_gn_g[j], batch, seq, RET_HEADS, ret_dk, ret_dv)
            x2d = _outproj(y, ret_w_out[j].astype(BF16), x2d, mod, layer, seq)
        else:
            w_in_p, uq_p, ukv_p, gate_col, kr_col = _mla_weights(
                mla_w_in[j], mla_w_uq[j], mla_w_ukv[j], q_rank, kv_rank, MLA_HEADS)
            proj = _mla_inproj(x2d, norm_g[layer], mod, layer, w_in_p, seq,
                               (gate_col, kr_col))
            qt, k, v = _mla_up(proj, mla_q_norm_g[j], mla_kv_norm_g[j], uq_p, ukv_p,
                               cos_m, sin_m, cos_t, sin_t, q_rank, kv_rank, kr_col,
                               MLA_HEADS, attn_tk)
            o = _attention(qt, k, v, proj, gate_col, batch, seq, MLA_HEADS, attn_tk)
            final_g = final_norm_g if layer == depth - 1 else None
            x2d = _outproj(o, mla_w_out[j].astype(BF16), x2d, mod, layer, seq, final_g)
    return x2d.reshape(batch, seq, d)
```

```python
import functools

import jax
import jax.numpy as jnp
import numpy as np
from jax import lax
from jax.experimental import pallas as pl
from jax.experimental.pallas import tpu as pltpu

F32 = jnp.float32
BF16 = jnp.bfloat16

EPS = 1e-6
ROPE_BASE = 10000.0
NEG_BIG = -1e30
LOG2_E = 1.4426950408889634

LANES = 128
SUBLANES = 8
MXU_TILE = 256
OUT_CHUNK = 2 * MXU_TILE
VMEM_LIMIT_BYTES = 56 * 2**20

RET_HEADS = 8
RET_CHUNK = 128
MLA_HEADS = 16
MLA_NOPE = 128
MLA_ROPE = 64
MLA_V = 128
MLA_HEAD_PAD = 256
ONES_ROWS = 16


def _params(*semantics):
    return pltpu.CompilerParams(dimension_semantics=semantics,
                                vmem_limit_bytes=VMEM_LIMIT_BYTES)


def _silu(x):
    return x * jax.nn.sigmoid(x)


def _mod_kernel(c_ref, w_ref, b_ref, o_ref):
    act = _silu(c_ref[...]).astype(BF16)
    o_ref[...] = jnp.dot(act, w_ref[...].astype(BF16),
                         preferred_element_type=F32) + b_ref[...]


def _modulation(c, ada_w, ada_b):
    depth, d, n3 = ada_w.shape
    b = c.shape[0]
    assert b <= SUBLANES
    tn = 768
    assert n3 % tn == 0
    c_pad = jnp.zeros((SUBLANES, d), F32).at[:b].set(c)
    out = pl.pallas_call(
        _mod_kernel,
        grid=(depth, n3 // tn),
        in_specs=[
            pl.BlockSpec((SUBLANES, d), lambda i, n: (0, 0)),
            pl.BlockSpec((None, d, tn), lambda i, n: (i, 0, n)),
            pl.BlockSpec((None, 1, tn), lambda i, n: (i, 0, n)),
        ],
        out_specs=pl.BlockSpec((None, SUBLANES, tn), lambda i, n: (i, 0, n)),
        out_shape=jax.ShapeDtypeStruct((depth, SUBLANES, n3), F32),
        compiler_params=_params("parallel", "parallel"),
        name="adaln_modulation",
    )(c_pad, ada_w, ada_b.reshape(depth, 1, n3))
    return out[:, :b].reshape(depth, b, 1, n3)


def _rope_table_kernel(pos_ref, freq_ref, cos_ref, sin_ref):
    ang = pos_ref[...] * freq_ref[...]
    cos_ref[...] = jnp.cos(ang)
    sin_ref[...] = jnp.sin(ang)


def _rope_tables(pos_lanes, freq_lanes):
    rows = pos_lanes.shape[0]
    tr = min(rows, 1024)
    assert rows % tr == 0
    spec = pl.BlockSpec((tr, LANES), lambda i: (i, 0))
    return pl.pallas_call(
        _rope_table_kernel,
        grid=(rows // tr,),
        in_specs=[spec, pl.BlockSpec((1, LANES), lambda i: (0, 0))],
        out_specs=[spec, spec],
        out_shape=[jax.ShapeDtypeStruct((rows, LANES), F32)] * 2,
        compiler_params=_params("parallel"),
        name="rope_tables",
    )(pos_lanes, freq_lanes)


def _rope_table_t_kernel(pos_ref, freq_ref, cos_ref, sin_ref):
    ang = freq_ref[...] * pos_ref[...]
    n = ang.shape[0]
    cos = jnp.cos(ang)
    sin = jnp.sin(ang)
    cos_ref[:n, :] = cos
    cos_ref[n:, :] = cos
    sin_ref[:n, :] = sin
    sin_ref[n:, :] = sin


def _rope_tables_t(pos_row, freq_col):
    t = pos_row.shape[1]
    n = freq_col.shape[0]
    tc = min(t, 4096)
    assert t % tc == 0
    out_spec = pl.BlockSpec((2 * n, tc), lambda i: (0, i))
    return pl.pallas_call(
        _rope_table_t_kernel,
        grid=(t // tc,),
        in_specs=[pl.BlockSpec((1, tc), lambda i: (0, i)),
                  pl.BlockSpec((n, 1), lambda i: (0, 0))],
        out_specs=[out_spec, out_spec],
        out_shape=[jax.ShapeDtypeStruct((2 * n, t), F32)] * 2,
        compiler_params=_params("parallel"),
        name="rope_tables_t",
    )(pos_row, freq_col)


def _inv_freq(d):
    return ROPE_BASE ** (-jnp.arange(0, d, 2, dtype=F32) / d)


def _adaln_rmsnorm(x_ref, g_ref, shift_ref, scale_ref):
    x = x_ref[...]
    y = x * lax.rsqrt(jnp.mean(x * x, axis=-1, keepdims=True) + EPS)
    return (y * (g_ref[...] * (1.0 + scale_ref[...])) + shift_ref[...]).astype(BF16)


def _mod_specs(layer, rows_per_batch_block, d):
    def spec(part):
        return pl.BlockSpec((None, None, 1, d),
                            lambda i, *_: (layer, i // rows_per_batch_block, 0, part))
    return spec


def _resident_spec(shape):
    return pl.BlockSpec(shape, lambda *_: (0,) * len(shape), pipeline_mode=pl.Buffered(1))


def _ret_inproj_kernel(x_ref, g_ref, shift_ref, scale_ref, cos_ref, sin_ref, w_ref,
                       o_ref, h_ref, *, rope_blocks, gate_block0, head_dim):
    n = pl.program_id(1)

    @pl.when(n == 0)
    def _():
        h_ref[...] = _adaln_rmsnorm(x_ref, g_ref, shift_ref, scale_ref)

    half = head_dim // 2
    heads = range(0, o_ref.shape[1], head_dim)

    def project(c0):
        return jnp.dot(h_ref[...], w_ref[:, c0:c0 + head_dim], preferred_element_type=F32)

    @pl.when(n < rope_blocks)
    def _():
        for c0 in heads:
            acc = project(c0)
            x1 = acc[:, :half]
            x2 = acc[:, half:]
            cos = cos_ref[...]
            sin = sin_ref[...]
            o_ref[:, c0:c0 + half] = (x1 * cos - x2 * sin).astype(BF16)
            o_ref[:, c0 + half:c0 + head_dim] = (x2 * cos + x1 * sin).astype(BF16)

    @pl.when((n >= rope_blocks) & (n < gate_block0))
    def _():
        for c0 in heads:
            o_ref[:, c0:c0 + head_dim] = project(c0).astype(BF16)

    @pl.when(n >= gate_block0)
    def _():
        for c0 in heads:
            o_ref[:, c0:c0 + head_dim] = _silu(project(c0)).astype(BF16)


def _ret_inproj(x2d, norm_g, mod, layer, cos, sin, w_bf16, seq, rope_cols, gate_col, head_dim):
    t, d = x2d.shape
    n_out = w_bf16.shape[1]
    tm = min(seq, 1024)
    tn = 2048
    assert seq % tm == 0 and n_out % tn == 0 and rope_cols % tn == 0 and tn % head_dim == 0
    assert head_dim // 2 == LANES and gate_col % tn == 0
    mspec = _mod_specs(layer, seq // tm, d)
    return pl.pallas_call(
        functools.partial(_ret_inproj_kernel, rope_blocks=rope_cols // tn,
                          gate_block0=gate_col // tn, head_dim=head_dim),
        grid=(t // tm, n_out // tn),
        in_specs=[
            pl.BlockSpec((tm, d), lambda i, n: (i, 0)),
            pl.BlockSpec((1, d), lambda i, n: (0, 0)),
            mspec(0), mspec(1),
            pl.BlockSpec((tm, LANES), lambda i, n: (i, 0)),
            pl.BlockSpec((tm, LANES), lambda i, n: (i, 0)),
            pl.BlockSpec((d, tn), lambda i, n: (0, n)),
        ],
        out_specs=pl.BlockSpec((tm, tn), lambda i, n: (i, n)),
        out_shape=jax.ShapeDtypeStruct((t, n_out), BF16),
        scratch_shapes=[pltpu.VMEM((tm, d), BF16)],
        compiler_params=_params("parallel", "arbitrary"),
        name="ret_in_proj",
    )(x2d, norm_g.reshape(1, d), mod, mod, cos, sin, w_bf16)


def _retention_kernel(decay_ref, q_ref, k_ref, v_ref, sg_ref, gn_ref, dmat_ref, xi_ref,
                      zeta_ref, o_ref, state_ref, *, chunk):
    @pl.when(pl.program_id(2) == 0)
    def _():
        state_ref[...] = jnp.zeros_like(state_ref)

    decay = decay_ref[pl.program_id(1)]
    for c in range(q_ref.shape[0] // chunk):
        rows = slice(c * chunk, (c + 1) * chunk)
        q = q_ref[rows, :]
        k = k_ref[rows, :]
        v = v_ref[rows, :]
        s = lax.dot_general(q, k, (((1,), (1,)), ((), ())), preferred_element_type=F32)
        kz = (k.astype(F32) * zeta_ref[...]).astype(BF16)
        update = lax.dot_general(kz, v, (((0,), (0,)), ((), ())), preferred_element_type=F32)
        s = (s * dmat_ref[...]).astype(BF16)
        state = state_ref[...]
        y = (jnp.dot(s, v, preferred_element_type=F32)
             + jnp.dot(q, state.astype(BF16), preferred_element_type=F32) * xi_ref[...])
        state_ref[...] = state * decay + update
        yc = y - jnp.mean(y, axis=-1, keepdims=True)
        var = jnp.mean(yc * yc, axis=-1, keepdims=True)
        yn = yc * lax.rsqrt(var + EPS) * gn_ref[...]
        o_ref[rows, :] = (sg_ref[rows, :].astype(F32) * yn).astype(BF16)


def _retention_constants(heads, chunk, dk, dv):
    f32 = np.float32
    log_gamma = np.log1p(-np.exp2(f32(-5.0) - np.arange(heads, dtype=f32))).astype(f32)
    idx = np.arange(chunk, dtype=f32)
    diff = idx[:, None] - idx[None, :]
    k_scale = f32(dk ** -0.5)
    d_intra = np.where(diff >= 0, np.exp(log_gamma[:, None, None] * np.maximum(diff, 0)),
                       0).astype(f32) * k_scale
    xi = np.exp(log_gamma[:, None] * (idx + 1)).astype(f32)
    zeta = np.exp(log_gamma[:, None] * (chunk - 1 - idx)).astype(f32) * k_scale
    chunk_decay = np.exp(log_gamma * chunk).astype(f32)
    bcast = lambda a, n: np.ascontiguousarray(np.broadcast_to(a[:, :, None], (heads, chunk, n)))
    return d_intra, bcast(xi, dv), bcast(zeta, dk), chunk_decay


def _retention(proj, gn_g, batch, seq, heads, dk, dv):
    t = proj.shape[0]
    width = heads * dv
    chunk = RET_CHUNK
    sb = min(seq, 1024)
    assert seq % sb == 0 and sb % chunk == 0
    nsb = seq // sb
    d_intra, xi, zeta, chunk_decay = _retention_constants(heads, chunk, dk, dv)
    k_blk0 = heads
    v_blk0 = 2 * heads * dk // dv
    g_blk0 = v_blk0 + heads
    row = lambda b, h, s: b * nsb + s
    return pl.pallas_call(
        functools.partial(_retention_kernel, chunk=chunk),
        grid=(batch, heads, nsb),
        in_specs=[
            pl.BlockSpec(memory_space=pltpu.SMEM),
            pl.BlockSpec((sb, dk), lambda b, h, s: (row(b, h, s), h)),
            pl.BlockSpec((sb, dk), lambda b, h, s: (row(b, h, s), k_blk0 + h)),
            pl.BlockSpec((sb, dv), lambda b, h, s: (row(b, h, s), v_blk0 + h)),
            pl.BlockSpec((sb, dv), lambda b, h, s: (row(b, h, s), g_blk0 + h)),
            pl.BlockSpec((1, dv), lambda b, h, s: (0, h)),
            pl.BlockSpec((None, chunk, chunk), lambda b, h, s: (h, 0, 0)),
            pl.BlockSpec((None, chunk, dv), lambda b, h, s: (h, 0, 0)),
            pl.BlockSpec((None, chunk, dk), lambda b, h, s: (h, 0, 0)),
        ],
        out_specs=pl.BlockSpec((sb, dv), lambda b, h, s: (row(b, h, s), h)),
        out_shape=jax.ShapeDtypeStruct((t, width), BF16),
        scratch_shapes=[pltpu.VMEM((dk, dv), F32)],
        compiler_params=_params("parallel", "parallel", "arbitrary"),
        name="retention",
    )(chunk_decay, proj, proj, proj, proj, gn_g.reshape(1, width), d_intra, xi, zeta)


def _residual_chunks(a_ref, w_ref, x_ref, gate_ref, o_ref, want_ssq):
    ssq = None
    for c0 in range(0, o_ref.shape[1], OUT_CHUNK):
        cols = slice(c0, c0 + OUT_CHUNK)
        y = jnp.dot(a_ref[...], w_ref[:, cols], preferred_element_type=F32)
        z = x_ref[:, cols] + gate_ref[:, cols] * y
        o_ref[:, cols] = z
        if want_ssq:
            part = jnp.sum(z * z, axis=-1, keepdims=True)
            ssq = part if ssq is None else ssq + part
    return ssq


def _outproj_kernel(a_ref, w_ref, x_ref, gate_ref, o_ref):
    _residual_chunks(a_ref, w_ref, x_ref, gate_ref, o_ref, False)


def _outproj_final_kernel(a_ref, w_ref, x_ref, gate_ref, fg_ref, o_ref):
    ssq = _residual_chunks(a_ref, w_ref, x_ref, gate_ref, o_ref, True)
    inv = lax.rsqrt(ssq * (1.0 / o_ref.shape[1]) + EPS)
    o_ref[...] = o_ref[...] * inv * fg_ref[...]


def _outproj(a, w_bf16, x2d, mod, layer, seq, final_g=None):
    t, k = a.shape
    d = w_bf16.shape[1]
    tm = min(seq, 512)
    assert seq % tm == 0 and d % OUT_CHUNK == 0
    in_specs = [
        pl.BlockSpec((tm, k), lambda i: (i, 0)),
        _resident_spec(w_bf16.shape),
        pl.BlockSpec((tm, d), lambda i: (i, 0)),
        _mod_specs(layer, seq // tm, d)(2),
    ]
    args = [a, w_bf16, x2d, mod]
    kern = _outproj_kernel
    if final_g is not None:
        in_specs.append(pl.BlockSpec((1, d), lambda i: (0, 0)))
        args.append(final_g.reshape(1, d))
        kern = _outproj_final_kernel
    return pl.pallas_call(
        kern,
        grid=(t // tm,),
        in_specs=in_specs,
        out_specs=pl.BlockSpec((tm, d), lambda i: (i, 0)),
        out_shape=jax.ShapeDtypeStruct((t, d), F32),
        compiler_params=_params("parallel"),
        name="out_proj_final" if final_g is not None else "out_proj",
    )(*args)


def _mla_inproj_kernel(x_ref, g_ref, shift_ref, scale_ref, w_ref, o_ref, h_ref, *, gate_cols):
    h_ref[...] = _adaln_rmsnorm(x_ref, g_ref, shift_ref, scale_ref)
    for c0 in range(0, o_ref.shape[1], MXU_TILE):
        acc = jnp.dot(h_ref[...], w_ref[:, c0:c0 + MXU_TILE], preferred_element_type=F32)
        if gate_cols[0] <= c0 < gate_cols[1]:
            acc = _silu(acc)
        o_ref[:, c0:c0 + MXU_TILE] = acc.astype(BF16)


def _mla_inproj(x2d, norm_g, mod, layer, w_bf16, seq, gate_cols):
    t, d = x2d.shape
    n_out = w_bf16.shape[1]
    tm = min(seq, 512)
    assert seq % tm == 0 and n_out % MXU_TILE == 0
    assert gate_cols[0] % MXU_TILE == 0 and gate_cols[1] % MXU_TILE == 0
    mspec = _mod_specs(layer, seq // tm, d)
    return pl.pallas_call(
        functools.partial(_mla_inproj_kernel, gate_cols=gate_cols),
        grid=(t // tm,),
        in_specs=[
            pl.BlockSpec((tm, d), lambda i: (i, 0)),
            pl.BlockSpec((1, d), lambda i: (0, 0)),
            mspec(0), mspec(1),
            _resident_spec(w_bf16.shape),
        ],
        out_specs=pl.BlockSpec((tm, n_out), lambda i: (i, 0)),
        out_shape=jax.ShapeDtypeStruct((t, n_out), BF16),
        scratch_shapes=[pltpu.VMEM((tm, d), BF16)],
        compiler_params=_params("parallel"),
        name="mla_in_proj",
    )(x2d, norm_g.reshape(1, d), mod, mod, w_bf16)


def _mla_up_kernel(cq_ref, ckv_ref, kr_ref, qg_ref, kvg_ref, wqt_ref, wkv_ref, cost_ref,
                   sint_ref, qt_ref, k_ref, v_ref, *, heads, q_scale, group):
    def rms(x_ref, g_ref):
        x = x_ref[...].astype(F32)
        return x * lax.rsqrt(jnp.mean(x * x, axis=-1, keepdims=True) + EPS) * g_ref[...]

    cqn_t = rms(cq_ref, qg_ref).T.astype(BF16)
    cos_t = cost_ref[...]
    sin_t = sint_ref[...]
    gw = group * MLA_HEAD_PAD
    for g0 in range(0, heads, group):
        qt = jnp.dot(wqt_ref[g0 * MLA_HEAD_PAD:g0 * MLA_HEAD_PAD + gw, :], cqn_t,
                     preferred_element_type=F32)
        for j in range(group):
            r0 = j * MLA_HEAD_PAD
            o0 = (g0 + j) * MLA_HEAD_PAD
            r1 = r0 + MLA_NOPE
            qt_ref[o0:o0 + MLA_NOPE, :] = (qt[r0:r1, :] * q_scale).astype(BF16)
            rope = (qt[r1:r1 + MLA_ROPE, :] * cos_t
                    + qt[r1 + MLA_ROPE:r1 + 2 * MLA_ROPE, :] * sin_t)
            qt_ref[o0 + MLA_NOPE:o0 + MLA_NOPE + MLA_ROPE, :] = (rope * q_scale).astype(BF16)
            qt_ref[o0 + MLA_NOPE + MLA_ROPE:o0 + MLA_HEAD_PAD, :] = jnp.zeros(
                (MLA_HEAD_PAD - MLA_NOPE - MLA_ROPE, qt.shape[1]), BF16)

    ckvn = rms(ckv_ref, kvg_ref).astype(BF16)
    kr_t = kr_ref[...].astype(F32).T
    kr_rope_t = kr_t[:MLA_ROPE, :] * cos_t + kr_t[MLA_ROPE:, :] * sin_t
    kr = jnp.concatenate([kr_rope_t, jnp.zeros_like(kr_rope_t)], axis=0).T.astype(BF16)
    nope_w = heads * MLA_NOPE
    for g0 in range(0, heads, group):
        kn = jnp.dot(ckvn, wkv_ref[:, g0 * MLA_NOPE:(g0 + group) * MLA_NOPE],
                     preferred_element_type=F32)
        for j in range(group):
            o0 = (g0 + j) * MLA_HEAD_PAD
            k_ref[:, o0:o0 + LANES] = kn[:, j * MLA_NOPE:(j + 1) * MLA_NOPE].astype(BF16)
            k_ref[:, o0 + LANES:o0 + 2 * LANES] = kr
    v_ref[...] = jnp.dot(ckvn, wkv_ref[:, nope_w:], preferred_element_type=F32).astype(BF16)


def _mla_up(proj, q_norm_g, kv_norm_g, wqt_bf16, wkv_bf16, cos_t, sin_t, q_rank, kv_rank,
            kr_col, heads, tm):
    t = proj.shape[0]
    assert t % tm == 0 and q_rank == kv_rank and kr_col % LANES == 0
    return pl.pallas_call(
        functools.partial(_mla_up_kernel, heads=heads,
                          q_scale=LOG2_E * (MLA_NOPE + MLA_ROPE) ** -0.5, group=4),
        grid=(t // tm,),
        in_specs=[
            pl.BlockSpec((tm, q_rank), lambda i: (i, 0)),
            pl.BlockSpec((tm, kv_rank), lambda i: (i, 1)),
            pl.BlockSpec((tm, LANES), lambda i: (i, kr_col // LANES)),
            pl.BlockSpec((1, q_rank), lambda i: (0, 0)),
            pl.BlockSpec((1, kv_rank), lambda i: (0, 0)),
            _resident_spec(wqt_bf16.shape),
            _resident_spec(wkv_bf16.shape),
            pl.BlockSpec((MLA_ROPE, tm), lambda i: (0, i)),
            pl.BlockSpec((MLA_ROPE, tm), lambda i: (0, i)),
        ],
        out_specs=[
            pl.BlockSpec((None, heads * MLA_HEAD_PAD, tm), lambda i: (i, 0, 0)),
            pl.BlockSpec((tm, heads * MLA_HEAD_PAD), lambda i: (i, 0)),
            pl.BlockSpec((tm, heads * MLA_V), lambda i: (i, 0)),
        ],
        out_shape=[
            jax.ShapeDtypeStruct((t // tm, heads * MLA_HEAD_PAD, tm), BF16),
            jax.ShapeDtypeStruct((t, heads * MLA_HEAD_PAD), BF16),
            jax.ShapeDtypeStruct((t, heads * MLA_V), BF16),
        ],
        compiler_params=_params("parallel"),
        name="mla_up_proj",
    )(proj, proj, proj, q_norm_g.reshape(1, q_rank), kv_norm_g.reshape(1, kv_rank),
      wqt_bf16, wkv_bf16, cos_t, sin_t)


def _attn_kernel(qt_ref, k_ref, v_ref, sg_ref, o_ref, vt_ref, s_ref, smax_ref, acc_ref, m_ref,
                 *, tk):
    tq = 2 * tk
    nq = qt_ref.shape[0] // 2
    all_q = slice(0, tq)
    lower_q = slice(0, tk)
    upper_q = slice(tk, tq)

    for j in range(vt_ref.shape[0]):
        vt_ref[j, :MLA_V, :] = v_ref[j * tk:(j + 1) * tk, :].astype(F32).T.astype(BF16)
        vt_ref[j, MLA_V:, :] = jnp.ones((ONES_ROWS, tk), BF16)

    def scores(t, slab):
        kt = k_ref[pl.ds(pl.multiple_of(t * tk, tk), tk), :]
        return jnp.dot(kt, qt_ref[slab], preferred_element_type=F32)

    def put_scores(slot, t, qi):
        for qs, slab in ((lower_q, 2 * qi), (upper_q, 2 * qi + 1)):
            st = scores(t, slab)
            s_ref[slot, :, qs] = st
            smax_ref[slot, :, qs] = jnp.max(st, axis=0, keepdims=True)

    def consume(t, slot, qs, diagonal):
        st = s_ref[slot, :, qs]
        if diagonal:
            key = lax.broadcasted_iota(jnp.int32, st.shape, 0)
            qry = lax.broadcasted_iota(jnp.int32, st.shape, 1)
            st = jnp.where(key <= qry, st, NEG_BIG)
            tile_max = jnp.max(st, axis=0, keepdims=True)
        else:
            tile_max = smax_ref[slot, :, qs]
        m_old = m_ref[:, qs]
        m_new = jnp.maximum(m_old, tile_max)
        p = jnp.exp2(st - m_new).astype(BF16)
        acc_ref[:, qs] = jnp.exp2(m_old - m_new) * acc_ref[:, qs] + jnp.dot(
            vt_ref[t], p, preferred_element_type=F32)
        m_ref[:, qs] = m_new

    def finish(q0, qs):
        acc = acc_ref[:, qs]
        o = (acc[:MLA_V, :] * (1.0 / acc[MLA_V:MLA_V + 1, :])).T
        rows = pl.ds(pl.multiple_of(q0 + qs.start, tk), qs.stop - qs.start)
        o_ref[rows, :] = (sg_ref[rows, :].astype(F32) * o).astype(BF16)

    put_scores(0, 0, 0)

    def query_block(qi, carry):
        q0 = qi * tq
        m_ref[...] = jnp.full(m_ref.shape, NEG_BIG, F32)
        acc_ref[...] = jnp.zeros_like(acc_ref)

        def pair(t):
            put_scores(1, t + 1, qi)
            consume(t, 0, all_q, False)
            put_scores(0, t + 2, qi)
            consume(t + 1, 1, all_q, False)

        def quad(u, c):
            pair(4 * u)
            pair(4 * u + 2)
            return c

        lax.fori_loop(0, qi // 2, quad, 0)

        @pl.when(qi % 2 == 1)
        def _():
            pair(2 * (qi - 1))

        s_ref[1, :, upper_q] = scores(2 * qi + 1, 2 * qi + 1)
        consume(2 * qi, 0, all_q, True)
        finish(q0, lower_q)
        consume(2 * qi + 1, 1, upper_q, True)
        put_scores(0, 0, jnp.minimum(qi + 1, nq - 1))
        finish(q0, upper_q)
        return carry

    lax.fori_loop(0, nq, query_block, 0)


def _attention(qt, k, v, proj, gate_col, batch, seq, heads, tk):
    t = k.shape[0]
    assert seq % (2 * tk) == 0 and gate_col % MLA_V == 0 and tk % LANES == 0
    g_blk0 = gate_col // MLA_V
    return pl.pallas_call(
        functools.partial(_attn_kernel, tk=tk),
        grid=(batch, heads),
        in_specs=[
            pl.BlockSpec((seq // tk, MLA_HEAD_PAD, tk), lambda b, h: (b, h, 0)),
            pl.BlockSpec((seq, MLA_HEAD_PAD), lambda b, h: (b, h)),
            pl.BlockSpec((seq, MLA_V), lambda b, h: (b, h)),
            pl.BlockSpec((seq, MLA_V), lambda b, h: (b, g_blk0 + h)),
        ],
        out_specs=pl.BlockSpec((seq, MLA_V), lambda b, h: (b, h)),
        out_shape=jax.ShapeDtypeStruct((t, heads * MLA_V), BF16),
        scratch_shapes=[
            pltpu.VMEM((seq // tk, MLA_V + ONES_ROWS, tk), BF16),
            pltpu.VMEM((2, tk, 2 * tk), F32),
            pltpu.VMEM((2, 1, 2 * tk), F32),
            pltpu.VMEM((MLA_V + ONES_ROWS, 2 * tk), F32),
            pltpu.VMEM((1, 2 * tk), F32),
        ],
        compiler_params=_params("parallel", "parallel"),
        name="mla_attention",
    )(qt, k, v, proj)


def _rotate_half_cols(w):
    half = w.shape[-1] // 2
    return jnp.concatenate([-w[..., half:], w[..., :half]], axis=-1)


def _mla_weights(w_in, w_uq, w_ukv, q_rank, kv_rank, heads):
    d = w_in.shape[0]
    cq = w_in[:, :q_rank]
    ckv = w_in[:, q_rank:q_rank + kv_rank]
    kr = w_in[:, q_rank + kv_rank:q_rank + kv_rank + MLA_ROPE]
    gate = w_in[:, q_rank + kv_rank + MLA_ROPE:]
    gate_col = q_rank + kv_rank
    kr_col = gate_col + gate.shape[1]
    pad = jnp.zeros((d, -(kr_col + 2 * MLA_ROPE) % MXU_TILE), w_in.dtype)
    w_in_p = jnp.concatenate([cq, ckv, gate, kr, _rotate_half_cols(kr), pad],
                             axis=1).astype(BF16)
    uq = w_uq.reshape(q_rank, heads, MLA_NOPE + MLA_ROPE)
    rope = uq[..., MLA_NOPE:]
    uq_p = jnp.concatenate([uq[..., :MLA_NOPE], rope, _rotate_half_cols(rope)], axis=-1)
    uq_p = uq_p.reshape(q_rank, heads * MLA_HEAD_PAD).T.astype(BF16)
    ukv = w_ukv.reshape(kv_rank, heads, MLA_NOPE + MLA_V)
    ukv_p = jnp.concatenate([ukv[..., :MLA_NOPE].reshape(kv_rank, heads * MLA_NOPE),
                             ukv[..., MLA_NOPE:].reshape(kv_rank, heads * MLA_V)],
                            axis=1).astype(BF16)
    return w_in_p, uq_p, ukv_p, gate_col, kr_col


def kernel(x, c, positions, ada_w, ada_b, norm_g, ret_w_in, ret_gn_g, ret_w_out, mla_w_in,
           mla_q_norm_g, mla_w_uq, mla_kv_norm_g, mla_w_ukv, mla_w_out, final_norm_g):
    batch, seq, d = x.shape
    t = batch * seq
    depth = ada_w.shape[0]
    assert depth % 2 == 0, "the last layer must be a latent-attention layer (final norm fusion)"
    ret_dk = d // RET_HEADS
    ret_dv = ret_w_out.shape[1] // RET_HEADS
    q_rank = mla_q_norm_g.shape[1]
    kv_rank = mla_kv_norm_g.shape[1]

    mod = _modulation(c, ada_w, ada_b)

    pos = positions.reshape(t, 1).astype(F32)
    cos_r, sin_r = _rope_tables(jnp.broadcast_to(pos, (t, LANES)),
                                _inv_freq(ret_dk).reshape(1, LANES))
    cos_t, sin_t = _rope_tables_t(positions.reshape(1, t).astype(F32),
                                  _inv_freq(MLA_ROPE).reshape(MLA_ROPE // 2, 1))
    attn_tk = min(seq // 2, 512)

    x2d = x.reshape(t, d)
    for layer in range(depth):
        j = layer // 2
        if layer % 2 == 0:
            proj = _ret_inproj(x2d, norm_g[layer], mod, layer, cos_r, sin_r,
                               ret_w_in[j].astype(BF16), seq, 2 * d,
                               2 * d + RET_HEADS * ret_dv, ret_dk)
            y = _retention(proj, ret_gn_g[j], batch, seq, RET_HEADS, ret_dk, ret_dv)
            x2d = _outproj(y, ret_w_out[j].astype(BF16), x2d, mod, layer, seq)
        else:
            w_in_p, uq_p, ukv_p, gate_col, kr_col = _mla_weights(
                mla_w_in[j], mla_w_uq[j], mla_w_ukv[j], q_rank, kv_rank, MLA_HEADS)
            proj = _mla_inproj(x2d, norm_g[layer], mod, layer, w_in_p, seq,
                               (gate_col, kr_col))
            qt, k, v = _mla_up(proj, mla_q_norm_g[j], mla_kv_norm_g[j], uq_p, ukv_p,
                               cos_t, sin_t, q_rank, kv_rank, kr_col, MLA_HEADS, attn_tk)
            o = _attention(qt, k, v, proj, gate_col, batch, seq, MLA_HEADS, attn_tk)
            final_g = final_norm_g if layer == depth - 1 else None
            x2d = _outproj(o, mla_w_out[j].astype(BF16), x2d, mod, layer, seq, final_g)
    return x2d.reshape(batch, seq, d)
```

```python
import functools

import jax
import jax.numpy as jnp
import numpy as np
from jax import lax
from jax.experimental import pallas as pl
from jax.experimental.pallas import tpu as pltpu

F32 = jnp.float32
BF16 = jnp.bfloat16

EPS = 1e-6
ROPE_BASE = 10000.0
NEG_BIG = -1e30
LOG2_E = 1.4426950408889634

LANES = 128
SUBLANES = 8
MXU_TILE = 256
OUT_CHUNK = 2 * MXU_TILE
VMEM_LIMIT_BYTES = 56 * 2**20

RET_HEADS = 8
RET_CHUNK = 256
MLA_HEADS = 16
MLA_NOPE = 128
MLA_ROPE = 64
MLA_V = 128
MLA_HEAD_PAD = 256
ONES_ROWS = 16


def _params(*semantics):
    return pltpu.CompilerParams(dimension_semantics=semantics,
                                vmem_limit_bytes=VMEM_LIMIT_BYTES)


def _silu(x):
    return x * jax.nn.sigmoid(x)


def _mod_kernel(c_ref, w_ref, b_ref, o_ref):
    act = _silu(c_ref[...]).astype(BF16)
    o_ref[...] = jnp.dot(act, w_ref[...].astype(BF16),
                         preferred_element_type=F32) + b_ref[...]


def _modulation(c, ada_w, ada_b):
    depth, d, n3 = ada_w.shape
    b = c.shape[0]
    assert b <= SUBLANES
    tn = 768
    assert n3 % tn == 0
    c_pad = jnp.zeros((SUBLANES, d), F32).at[:b].set(c)
    out = pl.pallas_call(
        _mod_kernel,
        grid=(depth, n3 // tn),
        in_specs=[
            pl.BlockSpec((SUBLANES, d), lambda i, n: (0, 0)),
            pl.BlockSpec((None, d, tn), lambda i, n: (i, 0, n)),
            pl.BlockSpec((None, 1, tn), lambda i, n: (i, 0, n)),
        ],
        out_specs=pl.BlockSpec((None, SUBLANES, tn), lambda i, n: (i, 0, n)),
        out_shape=jax.ShapeDtypeStruct((depth, SUBLANES, n3), F32),
        compiler_params=_params("parallel", "parallel"),
        name="adaln_modulation",
    )(c_pad, ada_w, ada_b.reshape(depth, 1, n3))
    return out[:, :b].reshape(depth, b, 1, n3)


def _rope_table_kernel(pos_ref, freq_ref, cos_ref, sin_ref):
    ang = pos_ref[...] * freq_ref[...]
    cos_ref[...] = jnp.cos(ang)
    sin_ref[...] = jnp.sin(ang)


def _rope_tables(pos_lanes, freq_lanes):
    rows = pos_lanes.shape[0]
    tr = min(rows, 1024)
    assert rows % tr == 0
    spec = pl.BlockSpec((tr, LANES), lambda i: (i, 0))
    return pl.pallas_call(
        _rope_table_kernel,
        grid=(rows // tr,),
        in_specs=[spec, pl.BlockSpec((1, LANES), lambda i: (0, 0))],
        out_specs=[spec, spec],
        out_shape=[jax.ShapeDtypeStruct((rows, LANES), F32)] * 2,
        compiler_params=_params("parallel"),
        name="rope_tables",
    )(pos_lanes, freq_lanes)


def _rope_table_t_kernel(pos_ref, freq_ref, cos_ref, sin_ref):
    ang = freq_ref[...] * pos_ref[...]
    n = ang.shape[0]
    cos = jnp.cos(ang)
    sin = jnp.sin(ang)
    cos_ref[:n, :] = cos
    cos_ref[n:, :] = cos
    sin_ref[:n, :] = sin
    sin_ref[n:, :] = sin


def _rope_tables_t(pos_row, freq_col):
    t = pos_row.shape[1]
    n = freq_col.shape[0]
    tc = min(t, 4096)
    assert t % tc == 0
    out_spec = pl.BlockSpec((2 * n, tc), lambda i: (0, i))
    return pl.pallas_call(
        _rope_table_t_kernel,
        grid=(t // tc,),
        in_specs=[pl.BlockSpec((1, tc), lambda i: (0, i)),
                  pl.BlockSpec((n, 1), lambda i: (0, 0))],
        out_specs=[out_spec, out_spec],
        out_shape=[jax.ShapeDtypeStruct((2 * n, t), F32)] * 2,
        compiler_params=_params("parallel"),
        name="rope_tables_t",
    )(pos_row, freq_col)


def _inv_freq(d):
    return ROPE_BASE ** (-jnp.arange(0, d, 2, dtype=F32) / d)


def _adaln_rmsnorm(x_ref, g_ref, shift_ref, scale_ref):
    x = x_ref[...]
    y = x * lax.rsqrt(jnp.mean(x * x, axis=-1, keepdims=True) + EPS)
    return (y * (g_ref[...] * (1.0 + scale_ref[...])) + shift_ref[...]).astype(BF16)


def _mod_specs(layer, rows_per_batch_block, d):
    def spec(part):
        return pl.BlockSpec((None, None, 1, d),
                            lambda i, *_: (layer, i // rows_per_batch_block, 0, part))
    return spec


def _resident_spec(shape):
    return pl.BlockSpec(shape, lambda *_: (0,) * len(shape), pipeline_mode=pl.Buffered(1))


def _ret_inproj_kernel(x_ref, g_ref, shift_ref, scale_ref, cos_ref, sin_ref, w_ref,
                       o_ref, h_ref, *, rope_blocks, gate_block0, head_dim):
    n = pl.program_id(1)

    @pl.when(n == 0)
    def _():
        h_ref[...] = _adaln_rmsnorm(x_ref, g_ref, shift_ref, scale_ref)

    half = head_dim // 2
    heads = range(0, o_ref.shape[1], head_dim)

    def project(c0):
        return jnp.dot(h_ref[...], w_ref[:, c0:c0 + head_dim], preferred_element_type=F32)

    @pl.when(n < rope_blocks)
    def _():
        for c0 in heads:
            acc = project(c0)
            x1 = acc[:, :half]
            x2 = acc[:, half:]
            cos = cos_ref[...]
            sin = sin_ref[...]
            o_ref[:, c0:c0 + half] = (x1 * cos - x2 * sin).astype(BF16)
            o_ref[:, c0 + half:c0 + head_dim] = (x2 * cos + x1 * sin).astype(BF16)

    @pl.when((n >= rope_blocks) & (n < gate_block0))
    def _():
        for c0 in heads:
            o_ref[:, c0:c0 + head_dim] = project(c0).astype(BF16)

    @pl.when(n >= gate_block0)
    def _():
        for c0 in heads:
            o_ref[:, c0:c0 + head_dim] = _silu(project(c0)).astype(BF16)


def _ret_inproj(x2d, norm_g, mod, layer, cos, sin, w_bf16, seq, rope_cols, gate_col, head_dim):
    t, d = x2d.shape
    n_out = w_bf16.shape[1]
    tm = min(seq, 1024)
    tn = 2048
    assert seq % tm == 0 and n_out % tn == 0 and rope_cols % tn == 0 and tn % head_dim == 0
    assert head_dim // 2 == LANES and gate_col % tn == 0
    mspec = _mod_specs(layer, seq // tm, d)
    return pl.pallas_call(
        functools.partial(_ret_inproj_kernel, rope_blocks=rope_cols // tn,
                          gate_block0=gate_col // tn, head_dim=head_dim),
        grid=(t // tm, n_out // tn),
        in_specs=[
            pl.BlockSpec((tm, d), lambda i, n: (i, 0)),
            pl.BlockSpec((1, d), lambda i, n: (0, 0)),
            mspec(0), mspec(1),
            pl.BlockSpec((tm, LANES), lambda i, n: (i, 0)),
            pl.BlockSpec((tm, LANES), lambda i, n: (i, 0)),
            pl.BlockSpec((d, tn), lambda i, n: (0, n)),
        ],
        out_specs=pl.BlockSpec((tm, tn), lambda i, n: (i, n)),
        out_shape=jax.ShapeDtypeStruct((t, n_out), BF16),
        scratch_shapes=[pltpu.VMEM((tm, d), BF16)],
        compiler_params=_params("parallel", "arbitrary"),
        name="ret_in_proj",
    )(x2d, norm_g.reshape(1, d), mod, mod, cos, sin, w_bf16)


def _retention_kernel(decay_ref, q_ref, k_ref, v_ref, sg_ref, gn_ref, dmat_ref, xi_ref,
                      zeta_ref, o_ref, state_ref, *, chunk):
    @pl.when(pl.program_id(2) == 0)
    def _():
        state_ref[...] = jnp.zeros_like(state_ref)

    decay = decay_ref[pl.program_id(1)]
    for c in range(q_ref.shape[0] // chunk):
        rows = slice(c * chunk, (c + 1) * chunk)
        q = q_ref[rows, :]
        k = k_ref[rows, :]
        v = v_ref[rows, :]
        s = lax.dot_general(q, k, (((1,), (1,)), ((), ())), preferred_element_type=F32)
        kz = (k.astype(F32) * zeta_ref[...]).astype(BF16)
        update = lax.dot_general(kz, v, (((0,), (0,)), ((), ())), preferred_element_type=F32)
        s = (s * dmat_ref[...]).astype(BF16)
        state = state_ref[...]
        y = (jnp.dot(s, v, preferred_element_type=F32)
             + jnp.dot(q, state.astype(BF16), preferred_element_type=F32) * xi_ref[...])
        state_ref[...] = state * decay + update
        yc = y - jnp.mean(y, axis=-1, keepdims=True)
        var = jnp.mean(yc * yc, axis=-1, keepdims=True)
        yn = yc * lax.rsqrt(var + EPS) * gn_ref[...]
        o_ref[rows, :] = sg_ref[rows, :] * yn.astype(BF16)


def _retention_constants(heads, chunk, dk, dv):
    f32 = np.float32
    log_gamma = np.log1p(-np.exp2(f32(-5.0) - np.arange(heads, dtype=f32))).astype(f32)
    idx = np.arange(chunk, dtype=f32)
    diff = idx[:, None] - idx[None, :]
    k_scale = f32(dk ** -0.5)
    d_intra = np.where(diff >= 0, np.exp(log_gamma[:, None, None] * np.maximum(diff, 0)),
                       0).astype(f32) * k_scale
    xi = np.exp(log_gamma[:, None] * (idx + 1)).astype(f32)
    zeta = np.exp(log_gamma[:, None] * (chunk - 1 - idx)).astype(f32) * k_scale
    chunk_decay = np.exp(log_gamma * chunk).astype(f32)
    bcast = lambda a, n: np.ascontiguousarray(np.broadcast_to(a[:, :, None], (heads, chunk, n)))
    return d_intra, bcast(xi, dv), bcast(zeta, dk), chunk_decay


def _retention(proj, gn_g, batch, seq, heads, dk, dv):
    t = proj.shape[0]
    width = heads * dv
    chunk = RET_CHUNK
    sb = min(seq, 1024)
    assert seq % sb == 0 and sb % chunk == 0
    nsb = seq // sb
    d_intra, xi, zeta, chunk_decay = _retention_constants(heads, chunk, dk, dv)
    k_blk0 = heads
    v_blk0 = 2 * heads * dk // dv
    g_blk0 = v_blk0 + heads
    row = lambda b, h, s: b * nsb + s
    return pl.pallas_call(
        functools.partial(_retention_kernel, chunk=chunk),
        grid=(batch, heads, nsb),
        in_specs=[
            pl.BlockSpec(memory_space=pltpu.SMEM),
            pl.BlockSpec((sb, dk), lambda b, h, s: (row(b, h, s), h)),
            pl.BlockSpec((sb, dk), lambda b, h, s: (row(b, h, s), k_blk0 + h)),
            pl.BlockSpec((sb, dv), lambda b, h, s: (row(b, h, s), v_blk0 + h)),
            pl.BlockSpec((sb, dv), lambda b, h, s: (row(b, h, s), g_blk0 + h)),
            pl.BlockSpec((1, dv), lambda b, h, s: (0, h)),
            pl.BlockSpec((None, chunk, chunk), lambda b, h, s: (h, 0, 0)),
            pl.BlockSpec((None, chunk, dv), lambda b, h, s: (h, 0, 0)),
            pl.BlockSpec((None, chunk, dk), lambda b, h, s: (h, 0, 0)),
        ],
        out_specs=pl.BlockSpec((sb, dv), lambda b, h, s: (row(b, h, s), h)),
        out_shape=jax.ShapeDtypeStruct((t, width), BF16),
        scratch_shapes=[pltpu.VMEM((dk, dv), F32)],
        compiler_params=_params("parallel", "parallel", "arbitrary"),
        name="retention",
    )(chunk_decay, proj, proj, proj, proj, gn_g.reshape(1, width), d_intra, xi, zeta)


def _residual_chunks(a_ref, w_ref, x_ref, gate_ref, o_ref, want_ssq):
    ssq = None
    for c0 in range(0, o_ref.shape[1], OUT_CHUNK):
        cols = slice(c0, c0 + OUT_CHUNK)
        y = jnp.dot(a_ref[...], w_ref[:, cols], preferred_element_type=F32)
        z = x_ref[:, cols] + gate_ref[:, cols] * y
        o_ref[:, cols] = z
        if want_ssq:
            part = jnp.sum(z * z, axis=-1, keepdims=True)
            ssq = part if ssq is None else ssq + part
    return ssq


def _outproj_kernel(a_ref, w_ref, x_ref, gate_ref, o_ref):
    _residual_chunks(a_ref, w_ref, x_ref, gate_ref, o_ref, False)


def _outproj_final_kernel(a_ref, w_ref, x_ref, gate_ref, fg_ref, o_ref):
    ssq = _residual_chunks(a_ref, w_ref, x_ref, gate_ref, o_ref, True)
    inv = lax.rsqrt(ssq * (1.0 / o_ref.shape[1]) + EPS)
    o_ref[...] = o_ref[...] * inv * fg_ref[...]


def _outproj(a, w_bf16, x2d, mod, layer, seq, final_g=None):
    t, k = a.shape
    d = w_bf16.shape[1]
    tm = min(seq, 512)
    assert seq % tm == 0 and d % OUT_CHUNK == 0
    in_specs = [
        pl.BlockSpec((tm, k), lambda i: (i, 0)),
        _resident_spec(w_bf16.shape),
        pl.BlockSpec((tm, d), lambda i: (i, 0)),
        _mod_specs(layer, seq // tm, d)(2),
    ]
    args = [a, w_bf16, x2d, mod]
    kern = _outproj_kernel
    if final_g is not None:
        in_specs.append(pl.BlockSpec((1, d), lambda i: (0, 0)))
        args.append(final_g.reshape(1, d))
        kern = _outproj_final_kernel
    return pl.pallas_call(
        kern,
        grid=(t // tm,),
        in_specs=in_specs,
        out_specs=pl.BlockSpec((tm, d), lambda i: (i, 0)),
        out_shape=jax.ShapeDtypeStruct((t, d), F32),
        compiler_params=_params("parallel"),
        name="out_proj_final" if final_g is not None else "out_proj",
    )(*args)


def _mla_inproj_kernel(x_ref, g_ref, shift_ref, scale_ref, w_ref, o_ref, h_ref, *, gate_cols):
    h_ref[...] = _adaln_rmsnorm(x_ref, g_ref, shift_ref, scale_ref)
    for c0 in range(0, o_ref.shape[1], MXU_TILE):
        acc = jnp.dot(h_ref[...], w_ref[:, c0:c0 + MXU_TILE], preferred_element_type=F32)
        if gate_cols[0] <= c0 < gate_cols[1]:
            acc = _silu(acc)
        o_ref[:, c0:c0 + MXU_TILE] = acc.astype(BF16)


def _mla_inproj(x2d, norm_g, mod, layer, w_bf16, seq, gate_cols):
    t, d = x2d.shape
    n_out = w_bf16.shape[1]
    tm = min(seq, 512)
    assert seq % tm == 0 and n_out % MXU_TILE == 0
    assert gate_cols[0] % MXU_TILE == 0 and gate_cols[1] % MXU_TILE == 0
    mspec = _mod_specs(layer, seq // tm, d)
    return pl.pallas_call(
        functools.partial(_mla_inproj_kernel, gate_cols=gate_cols),
        grid=(t // tm,),
        in_specs=[
            pl.BlockSpec((tm, d), lambda i: (i, 0)),
            pl.BlockSpec((1, d), lambda i: (0, 0)),
            mspec(0), mspec(1),
            _resident_spec(w_bf16.shape),
        ],
        out_specs=pl.BlockSpec((tm, n_out), lambda i: (i, 0)),
        out_shape=jax.ShapeDtypeStruct((t, n_out), BF16),
        scratch_shapes=[pltpu.VMEM((tm, d), BF16)],
        compiler_params=_params("parallel"),
        name="mla_in_proj",
    )(x2d, norm_g.reshape(1, d), mod, mod, w_bf16)


def _mla_up_kernel(cq_ref, ckv_ref, kr_ref, qg_ref, kvg_ref, wqt_ref, wk_ref, wvt_ref, cost_ref,
                   sint_ref, qt_ref, k_ref, vt_ref, *, heads, q_scale, group):
    def rms(x_ref, g_ref):
        x = x_ref[...].astype(F32)
        return x * lax.rsqrt(jnp.mean(x * x, axis=-1, keepdims=True) + EPS) * g_ref[...]

    cqn_t = rms(cq_ref, qg_ref).T.astype(BF16)
    cos_t = cost_ref[...]
    sin_t = sint_ref[...]
    gw = group * MLA_HEAD_PAD
    for g0 in range(0, heads, group):
        qt = jnp.dot(wqt_ref[g0 * MLA_HEAD_PAD:g0 * MLA_HEAD_PAD + gw, :], cqn_t,
                     preferred_element_type=F32)
        for j in range(group):
            r0 = j * MLA_HEAD_PAD
            o0 = (g0 + j) * MLA_HEAD_PAD
            r1 = r0 + MLA_NOPE
            qt_ref[o0:o0 + MLA_NOPE, :] = (qt[r0:r1, :] * q_scale).astype(BF16)
            rope = (qt[r1:r1 + MLA_ROPE, :] * cos_t
                    + qt[r1 + MLA_ROPE:r1 + 2 * MLA_ROPE, :] * sin_t)
            qt_ref[o0 + MLA_NOPE:o0 + MLA_NOPE + MLA_ROPE, :] = (rope * q_scale).astype(BF16)
            qt_ref[o0 + MLA_NOPE + MLA_ROPE:o0 + MLA_HEAD_PAD, :] = jnp.zeros(
                (MLA_HEAD_PAD - MLA_NOPE - MLA_ROPE, qt.shape[1]), BF16)

    ckvn_f32 = rms(ckv_ref, kvg_ref)
    ckvn = ckvn_f32.astype(BF16)
    kr_t = kr_ref[...].astype(F32).T
    kr_rope_t = kr_t[:MLA_ROPE, :] * cos_t + kr_t[MLA_ROPE:, :] * sin_t
    kr = jnp.concatenate([kr_rope_t, jnp.zeros_like(kr_rope_t)], axis=0).T.astype(BF16)
    for g0 in range(0, heads, group):
        kn = jnp.dot(ckvn, wk_ref[:, g0 * MLA_NOPE:(g0 + group) * MLA_NOPE],
                     preferred_element_type=F32)
        for j in range(group):
            o0 = (g0 + j) * MLA_HEAD_PAD
            k_ref[:, o0:o0 + LANES] = kn[:, j * MLA_NOPE:(j + 1) * MLA_NOPE].astype(BF16)
            k_ref[:, o0 + LANES:o0 + 2 * LANES] = kr
    ckvn_t = ckvn_f32.T.astype(BF16)
    gv = group * MLA_V
    for r0 in range(0, heads * MLA_V, gv):
        vt_ref[r0:r0 + gv, :] = jnp.dot(wvt_ref[r0:r0 + gv, :], ckvn_t,
                                        preferred_element_type=F32).astype(BF16)


def _mla_up(proj, q_norm_g, kv_norm_g, wqt_bf16, wk_bf16, wvt_bf16, cos_t, sin_t, q_rank,
            kv_rank, kr_col, heads, tm):
    t = proj.shape[0]
    assert t % tm == 0 and q_rank == kv_rank and kr_col % LANES == 0
    return pl.pallas_call(
        functools.partial(_mla_up_kernel, heads=heads,
                          q_scale=LOG2_E * (MLA_NOPE + MLA_ROPE) ** -0.5, group=4),
        grid=(t // tm,),
        in_specs=[
            pl.BlockSpec((tm, q_rank), lambda i: (i, 0)),
            pl.BlockSpec((tm, kv_rank), lambda i: (i, 1)),
            pl.BlockSpec((tm, LANES), lambda i: (i, kr_col // LANES)),
            pl.BlockSpec((1, q_rank), lambda i: (0, 0)),
            pl.BlockSpec((1, kv_rank), lambda i: (0, 0)),
            _resident_spec(wqt_bf16.shape),
            _resident_spec(wk_bf16.shape),
            _resident_spec(wvt_bf16.shape),
            pl.BlockSpec((MLA_ROPE, tm), lambda i: (0, i)),
            pl.BlockSpec((MLA_ROPE, tm), lambda i: (0, i)),
        ],
        out_specs=[
            pl.BlockSpec((None, heads * MLA_HEAD_PAD, tm), lambda i: (i, 0, 0)),
            pl.BlockSpec((tm, heads * MLA_HEAD_PAD), lambda i: (i, 0)),
            pl.BlockSpec((None, heads * MLA_V, tm), lambda i: (i, 0, 0)),
        ],
        out_shape=[
            jax.ShapeDtypeStruct((t // tm, heads * MLA_HEAD_PAD, tm), BF16),
            jax.ShapeDtypeStruct((t, heads * MLA_HEAD_PAD), BF16),
            jax.ShapeDtypeStruct((t // tm, heads * MLA_V, tm), BF16),
        ],
        compiler_params=_params("parallel"),
        name="mla_up_proj",
    )(proj, proj, proj, q_norm_g.reshape(1, q_rank), kv_norm_g.reshape(1, kv_rank),
      wqt_bf16, wk_bf16, wvt_bf16, cos_t, sin_t)


def _attn_kernel(qt_ref, k_ref, vt_ref, sg_ref, o_ref, s_ref, smax_ref, acc_ref, m_ref, *, tk):
    tq = 2 * tk
    nq = qt_ref.shape[0] // 2
    all_q = slice(0, tq)
    lower_q = slice(0, tk)
    upper_q = slice(tk, tq)

    ones_rows = jnp.ones((ONES_ROWS, tk), BF16)

    def scores(t, slab):
        kt = k_ref[pl.ds(pl.multiple_of(t * tk, tk), tk), :]
        return jnp.dot(kt, qt_ref[slab], preferred_element_type=F32)

    def put_scores(slot, t, qi):
        for qs, slab in ((lower_q, 2 * qi), (upper_q, 2 * qi + 1)):
            st = scores(t, slab)
            s_ref[slot, :, qs] = st
            smax_ref[slot, :, qs] = jnp.max(st, axis=0, keepdims=True)

    def consume(t, slot, qs, diagonal):
        st = s_ref[slot, :, qs]
        if diagonal:
            key = lax.broadcasted_iota(jnp.int32, st.shape, 0)
            qry = lax.broadcasted_iota(jnp.int32, st.shape, 1)
            st = jnp.where(key <= qry, st, NEG_BIG)
            tile_max = jnp.max(st, axis=0, keepdims=True)
        else:
            tile_max = smax_ref[slot, :, qs]
        m_old = m_ref[:, qs]
        m_new = jnp.maximum(m_old, tile_max)
        p = jnp.exp2(st - m_new).astype(BF16)
        vt_ones = jnp.concatenate([vt_ref[t], ones_rows], axis=0)
        acc_ref[:, qs] = jnp.exp2(m_old - m_new) * acc_ref[:, qs] + jnp.dot(
            vt_ones, p, preferred_element_type=F32)
        m_ref[:, qs] = m_new

    def finish(q0, qs):
        acc = acc_ref[:, qs]
        o = (acc[:MLA_V, :] * (1.0 / acc[MLA_V:MLA_V + 1, :])).T
        rows = pl.ds(pl.multiple_of(q0 + qs.start, tk), qs.stop - qs.start)
        o_ref[rows, :] = (sg_ref[rows, :].astype(F32) * o).astype(BF16)

    put_scores(0, 0, 0)

    def query_block(qi, carry):
        q0 = qi * tq
        m_ref[...] = jnp.full(m_ref.shape, NEG_BIG, F32)
        acc_ref[...] = jnp.zeros_like(acc_ref)

        def pair(t):
            put_scores(1, t + 1, qi)
            consume(t, 0, all_q, False)
            put_scores(0, t + 2, qi)
            consume(t + 1, 1, all_q, False)

        def quad(u, c):
            pair(4 * u)
            pair(4 * u + 2)
            return c

        lax.fori_loop(0, qi // 2, quad, 0)

        @pl.when(qi % 2 == 1)
        def _():
            pair(2 * (qi - 1))

        s_ref[1, :, upper_q] = scores(2 * qi + 1, 2 * qi + 1)
        consume(2 * qi, 0, lower_q, True)
        consume(2 * qi, 0, upper_q, False)
        finish(q0, lower_q)
        consume(2 * qi + 1, 1, upper_q, True)
        put_scores(0, 0, jnp.minimum(qi + 1, nq - 1))
        finish(q0, upper_q)
        return carry

    lax.fori_loop(0, nq, query_block, 0)


def _attention(qt, k, vt, proj, gate_col, batch, seq, heads, tk):
    t = k.shape[0]
    assert seq % (2 * tk) == 0 and gate_col % MLA_V == 0 and tk % LANES == 0
    g_blk0 = gate_col // MLA_V
    return pl.pallas_call(
        functools.partial(_attn_kernel, tk=tk),
        grid=(batch, heads),
        in_specs=[
            pl.BlockSpec((seq // tk, MLA_HEAD_PAD, tk), lambda b, h: (b, h, 0)),
            pl.BlockSpec((seq, MLA_HEAD_PAD), lambda b, h: (b, h)),
            pl.BlockSpec((seq // tk, MLA_V, tk), lambda b, h: (b, h, 0)),
            pl.BlockSpec((seq, MLA_V), lambda b, h: (b, g_blk0 + h)),
        ],
        out_specs=pl.BlockSpec((seq, MLA_V), lambda b, h: (b, h)),
        out_shape=jax.ShapeDtypeStruct((t, heads * MLA_V), BF16),
        scratch_shapes=[
            pltpu.VMEM((2, tk, 2 * tk), F32),
            pltpu.VMEM((2, 1, 2 * tk), F32),
            pltpu.VMEM((MLA_V + ONES_ROWS, 2 * tk), F32),
            pltpu.VMEM((1, 2 * tk), F32),
        ],
        compiler_params=_params("parallel", "parallel"),
        name="mla_attention",
    )(qt, k, vt, proj)


def _rotate_half_cols(w):
    half = w.shape[-1] // 2
    return jnp.concatenate([-w[..., half:], w[..., :half]], axis=-1)


def _mla_weights(w_in, w_uq, w_ukv, q_rank, kv_rank, heads):
    d = w_in.shape[0]
    cq = w_in[:, :q_rank]
    ckv = w_in[:, q_rank:q_rank + kv_rank]
    kr = w_in[:, q_rank + kv_rank:q_rank + kv_rank + MLA_ROPE]
    gate = w_in[:, q_rank + kv_rank + MLA_ROPE:]
    gate_col = q_rank + kv_rank
    kr_col = gate_col + gate.shape[1]
    pad = jnp.zeros((d, -(kr_col + 2 * MLA_ROPE) % MXU_TILE), w_in.dtype)
    w_in_p = jnp.concatenate([cq, ckv, gate, kr, _rotate_half_cols(kr), pad],
                             axis=1).astype(BF16)
    uq = w_uq.reshape(q_rank, heads, MLA_NOPE + MLA_ROPE)
    rope = uq[..., MLA_NOPE:]
    uq_p = jnp.concatenate([uq[..., :MLA_NOPE], rope, _rotate_half_cols(rope)], axis=-1)
    uq_p = uq_p.reshape(q_rank, heads * MLA_HEAD_PAD).T.astype(BF16)
    ukv = w_ukv.reshape(kv_rank, heads, MLA_NOPE + MLA_V)
    uk_p = ukv[..., :MLA_NOPE].reshape(kv_rank, heads * MLA_NOPE).astype(BF16)
    uv_p = ukv[..., MLA_NOPE:].reshape(kv_rank, heads * MLA_V).T.astype(BF16)
    return w_in_p, uq_p, uk_p, uv_p, gate_col, kr_col


def kernel(x, c, positions, ada_w, ada_b, norm_g, ret_w_in, ret_gn_g, ret_w_out, mla_w_in,
           mla_q_norm_g, mla_w_uq, mla_kv_norm_g, mla_w_ukv, mla_w_out, final_norm_g):
    batch, seq, d = x.shape
    t = batch * seq
    depth = ada_w.shape[0]
    assert depth % 2 == 0, "the last layer must be a latent-attention layer (final norm fusion)"
    ret_dk = d // RET_HEADS
    ret_dv = ret_w_out.shape[1] // RET_HEADS
    q_rank = mla_q_norm_g.shape[1]
    kv_rank = mla_kv_norm_g.shape[1]

    mod = _modulation(c, ada_w, ada_b)

    pos = positions.reshape(t, 1).astype(F32)
    cos_r, sin_r = _rope_tables(jnp.broadcast_to(pos, (t, LANES)),
                                _inv_freq(ret_dk).reshape(1, LANES))
    cos_t, sin_t = _rope_tables_t(positions.reshape(1, t).astype(F32),
                                  _inv_freq(MLA_ROPE).reshape(MLA_ROPE // 2, 1))
    attn_tk = min(seq // 2, 512)

    x2d = x.reshape(t, d)
    for layer in range(depth):
        j = layer // 2
        if layer % 2 == 0:
            proj = _ret_inproj(x2d, norm_g[layer], mod, layer, cos_r, sin_r,
                               ret_w_in[j].astype(BF16), seq, 2 * d,
                               2 * d + RET_HEADS * ret_dv, ret_dk)
            y = _retention(proj, ret_gn_g[j], batch, seq, RET_HEADS, ret_dk, ret_dv)
            x2d = _outproj(y, ret_w_out[j].astype(BF16), x2d, mod, layer, seq)
        else:
            w_in_p, uq_p, uk_p, uv_p, gate_col, kr_col = _mla_weights(
                mla_w_in[j], mla_w_uq[j], mla_w_ukv[j], q_rank, kv_rank, MLA_HEADS)
            proj = _mla_inproj(x2d, norm_g[layer], mod, layer, w_in_p, seq,
                               (gate_col, kr_col))
            qt, k, vt = _mla_up(proj, mla_q_norm_g[j], mla_kv_norm_g[j], uq_p, uk_p, uv_p,
                                cos_t, sin_t, q_rank, kv_rank, kr_col, MLA_HEADS, attn_tk)
            o = _attention(qt, k, vt, proj, gate_col, batch, seq, MLA_HEADS, attn_tk)
            final_g = final_norm_g if layer == depth - 1 else None
            x2d = _outproj(o, mla_w_out[j].astype(BF16), x2d, mod, layer, seq, final_g)
    return x2d.reshape(batch, seq, d)
```

```python
import functools

import jax
import jax.numpy as jnp
import numpy as np
from jax import lax
from jax.experimental import pallas as pl
from jax.experimental.pallas import tpu as pltpu

F32 = jnp.float32
BF16 = jnp.bfloat16

EPS = 1e-6
ROPE_BASE = 10000.0
NEG_BIG = -1e30
LOG2_E = 1.4426950408889634

LANES = 128
SUBLANES = 8
MXU_TILE = 256
OUT_CHUNK = 2 * MXU_TILE
VMEM_LIMIT_BYTES = 56 * 2**20

RET_HEADS = 8
RET_CHUNK = 256
MLA_HEADS = 16
MLA_NOPE = 128
MLA_ROPE = 64
MLA_V = 128
MLA_HEAD_PAD = 256
ONES_ROWS = 16
ATTN_SLABS = 2
ATTN_LOOP_TILES = 4


def _params(*semantics):
    return pltpu.CompilerParams(dimension_semantics=semantics,
                                vmem_limit_bytes=VMEM_LIMIT_BYTES)


def _silu(x):
    h = 0.5 * x
    return h + h * jnp.tanh(h)


def _mod_kernel(c_ref, w_ref, b_ref, o_ref):
    act = _silu(c_ref[...]).astype(BF16)
    o_ref[...] = jnp.dot(act, w_ref[...].astype(BF16),
                         preferred_element_type=F32) + b_ref[...]


def _modulation(c, ada_w, ada_b):
    depth, d, n3 = ada_w.shape
    b = c.shape[0]
    assert b <= SUBLANES
    tn = 768
    assert n3 % tn == 0
    c_pad = jnp.zeros((SUBLANES, d), F32).at[:b].set(c)
    out = pl.pallas_call(
        _mod_kernel,
        grid=(depth, n3 // tn),
        in_specs=[
            pl.BlockSpec((SUBLANES, d), lambda i, n: (0, 0)),
            pl.BlockSpec((None, d, tn), lambda i, n: (i, 0, n)),
            pl.BlockSpec((None, 1, tn), lambda i, n: (i, 0, n)),
        ],
        out_specs=pl.BlockSpec((None, SUBLANES, tn), lambda i, n: (i, 0, n)),
        out_shape=jax.ShapeDtypeStruct((depth, SUBLANES, n3), F32),
        compiler_params=_params("parallel", "parallel"),
        name="adaln_modulation",
    )(c_pad, ada_w, ada_b.reshape(depth, 1, n3))
    return out[:, :b].reshape(depth, b, 1, n3)


def _rope_table_kernel(pos_ref, freq_ref, cos_ref, sin_ref):
    ang = pos_ref[...] * freq_ref[...]
    cos_ref[...] = jnp.cos(ang)
    sin_ref[...] = jnp.sin(ang)


def _rope_tables(pos_lanes, freq_lanes):
    rows = pos_lanes.shape[0]
    tr = min(rows, 1024)
    assert rows % tr == 0
    spec = pl.BlockSpec((tr, LANES), lambda i: (i, 0))
    return pl.pallas_call(
        _rope_table_kernel,
        grid=(rows // tr,),
        in_specs=[spec, pl.BlockSpec((1, LANES), lambda i: (0, 0))],
        out_specs=[spec, spec],
        out_shape=[jax.ShapeDtypeStruct((rows, LANES), F32)] * 2,
        compiler_params=_params("parallel"),
        name="rope_tables",
    )(pos_lanes, freq_lanes)


def _rope_table_t_kernel(pos_ref, freq_ref, cos_ref, sin_ref):
    ang = freq_ref[...] * pos_ref[...]
    n = ang.shape[0]
    cos = jnp.cos(ang)
    sin = jnp.sin(ang)
    cos_ref[:n, :] = cos
    cos_ref[n:, :] = cos
    sin_ref[:n, :] = sin
    sin_ref[n:, :] = sin


def _rope_tables_t(pos_row, freq_col):
    t = pos_row.shape[1]
    n = freq_col.shape[0]
    tc = min(t, 4096)
    assert t % tc == 0
    out_spec = pl.BlockSpec((2 * n, tc), lambda i: (0, i))
    return pl.pallas_call(
        _rope_table_t_kernel,
        grid=(t // tc,),
        in_specs=[pl.BlockSpec((1, tc), lambda i: (0, i)),
                  pl.BlockSpec((n, 1), lambda i: (0, 0))],
        out_specs=[out_spec, out_spec],
        out_shape=[jax.ShapeDtypeStruct((2 * n, t), F32)] * 2,
        compiler_params=_params("parallel"),
        name="rope_tables_t",
    )(pos_row, freq_col)


def _inv_freq(d):
    return ROPE_BASE ** (-jnp.arange(0, d, 2, dtype=F32) / d)


def _adaln_rmsnorm(x_ref, g_ref, shift_ref, scale_ref):
    x = x_ref[...]
    y = x * lax.rsqrt(jnp.mean(x * x, axis=-1, keepdims=True) + EPS)
    return (y * (g_ref[...] * (1.0 + scale_ref[...])) + shift_ref[...]).astype(BF16)


def _mod_specs(layer, rows_per_batch_block, d):
    def spec(part):
        return pl.BlockSpec((None, None, 1, d),
                            lambda i, *_: (layer, i // rows_per_batch_block, 0, part))
    return spec


def _resident_spec(shape):
    return pl.BlockSpec(shape, lambda *_: (0,) * len(shape), pipeline_mode=pl.Buffered(1))


def _ret_inproj_kernel(x_ref, g_ref, shift_ref, scale_ref, cos_ref, sin_ref, w_ref,
                       o_ref, h_ref, *, rope_blocks, gate_block0, head_dim):
    n = pl.program_id(1)

    @pl.when(n == 0)
    def _():
        h_ref[...] = _adaln_rmsnorm(x_ref, g_ref, shift_ref, scale_ref)

    half = head_dim // 2
    tm = o_ref.shape[0]
    pieces = [(c0, r0) for c0 in range(0, o_ref.shape[1], head_dim)
              for r0 in range(0, tm, tm // 2)]

    def project(c0, r0):
        return jnp.dot(h_ref[r0:r0 + tm // 2, :], w_ref[:, c0:c0 + head_dim],
                       preferred_element_type=F32)

    @pl.when(n < rope_blocks)
    def _():
        for c0, r0 in pieces:
            rows = slice(r0, r0 + tm // 2)
            acc = project(c0, r0)
            x1 = acc[:, :half]
            x2 = acc[:, half:]
            cos = cos_ref[rows, :]
            sin = sin_ref[rows, :]
            o_ref[rows, c0:c0 + half] = (x1 * cos - x2 * sin).astype(BF16)
            o_ref[rows, c0 + half:c0 + head_dim] = (x2 * cos + x1 * sin).astype(BF16)

    @pl.when((n >= rope_blocks) & (n < gate_block0))
    def _():
        for c0, r0 in pieces:
            o_ref[r0:r0 + tm // 2, c0:c0 + head_dim] = project(c0, r0).astype(BF16)

    @pl.when(n >= gate_block0)
    def _():
        for c0, r0 in pieces:
            o_ref[r0:r0 + tm // 2, c0:c0 + head_dim] = _silu(project(c0, r0)).astype(BF16)


def _ret_inproj(x2d, norm_g, mod, layer, cos, sin, w_bf16, seq, rope_cols, gate_col, head_dim):
    t, d = x2d.shape
    n_out = w_bf16.shape[1]
    tm = min(seq, 1024)
    tn = 2048
    assert seq % tm == 0 and n_out % tn == 0 and rope_cols % tn == 0 and tn % head_dim == 0
    assert head_dim // 2 == LANES and gate_col % tn == 0
    mspec = _mod_specs(layer, seq // tm, d)
    return pl.pallas_call(
        functools.partial(_ret_inproj_kernel, rope_blocks=rope_cols // tn,
                          gate_block0=gate_col // tn, head_dim=head_dim),
        grid=(t // tm, n_out // tn),
        in_specs=[
            pl.BlockSpec((tm, d), lambda i, n: (i, 0)),
            pl.BlockSpec((1, d), lambda i, n: (0, 0)),
            mspec(0), mspec(1),
            pl.BlockSpec((tm, LANES), lambda i, n: (i, 0)),
            pl.BlockSpec((tm, LANES), lambda i, n: (i, 0)),
            pl.BlockSpec((d, tn), lambda i, n: (0, n)),
        ],
        out_specs=pl.BlockSpec((tm, tn), lambda i, n: (i, n)),
        out_shape=jax.ShapeDtypeStruct((t, n_out), BF16),
        scratch_shapes=[pltpu.VMEM((tm, d), BF16)],
        compiler_params=_params("parallel", "arbitrary"),
        name="ret_in_proj",
    )(x2d, norm_g.reshape(1, d), mod, mod, cos, sin, w_bf16)


def _retention_kernel(decay_ref, q_ref, k_ref, v_ref, sg_ref, gn_ref, dmat_ref, xi_ref,
                      zeta_ref, o_ref, state_ref, *, chunk):
    @pl.when(pl.program_id(2) == 0)
    def _():
        state_ref[...] = jnp.zeros_like(state_ref)

    decay = decay_ref[pl.program_id(1)]
    for c in range(q_ref.shape[0] // chunk):
        rows = slice(c * chunk, (c + 1) * chunk)
        q = q_ref[rows, :]
        k = k_ref[rows, :]
        v = v_ref[rows, :]
        s = lax.dot_general(q, k, (((1,), (1,)), ((), ())), preferred_element_type=F32)
        kz = (k.astype(F32) * zeta_ref[...]).astype(BF16)
        update = lax.dot_general(kz, v, (((0,), (0,)), ((), ())), preferred_element_type=F32)
        s = (s * dmat_ref[...]).astype(BF16)
        state = state_ref[...]
        y = (jnp.dot(s, v, preferred_element_type=F32)
             + jnp.dot(q, state.astype(BF16), preferred_element_type=F32) * xi_ref[...])
        state_ref[...] = state * decay + update
        yc = y - jnp.mean(y, axis=-1, keepdims=True)
        var = jnp.mean(yc * yc, axis=-1, keepdims=True)
        yn = yc * lax.rsqrt(var + EPS) * gn_ref[...]
        o_ref[rows, :] = sg_ref[rows, :] * yn.astype(BF16)


def _retention_constants(heads, chunk, dk, dv):
    f32 = np.float32
    log_gamma = np.log1p(-np.exp2(f32(-5.0) - np.arange(heads, dtype=f32))).astype(f32)
    idx = np.arange(chunk, dtype=f32)
    diff = idx[:, None] - idx[None, :]
    k_scale = f32(dk ** -0.5)
    d_intra = np.where(diff >= 0, np.exp(log_gamma[:, None, None] * np.maximum(diff, 0)),
                       0).astype(f32) * k_scale
    xi = np.exp(log_gamma[:, None] * (idx + 1)).astype(f32)
    zeta = np.exp(log_gamma[:, None] * (chunk - 1 - idx)).astype(f32) * k_scale
    chunk_decay = np.exp(log_gamma * chunk).astype(f32)
    bcast = lambda a, n: np.ascontiguousarray(np.broadcast_to(a[:, :, None], (heads, chunk, n)))
    return d_intra, bcast(xi, dv), bcast(zeta, dk), chunk_decay


def _retention(proj, gn_g, batch, seq, heads, dk, dv):
    t = proj.shape[0]
    width = heads * dv
    chunk = RET_CHUNK
    sb = min(seq, 1024)
    assert seq % sb == 0 and sb % chunk == 0
    nsb = seq // sb
    d_intra, xi, zeta, chunk_decay = _retention_constants(heads, chunk, dk, dv)
    k_blk0 = heads
    v_blk0 = 2 * heads * dk // dv
    g_blk0 = v_blk0 + heads
    row = lambda b, h, s: b * nsb + s
    return pl.pallas_call(
        functools.partial(_retention_kernel, chunk=chunk),
        grid=(batch, heads, nsb),
        in_specs=[
            pl.BlockSpec(memory_space=pltpu.SMEM),
            pl.BlockSpec((sb, dk), lambda b, h, s: (row(b, h, s), h)),
            pl.BlockSpec((sb, dk), lambda b, h, s: (row(b, h, s), k_blk0 + h)),
            pl.BlockSpec((sb, dv), lambda b, h, s: (row(b, h, s), v_blk0 + h)),
            pl.BlockSpec((sb, dv), lambda b, h, s: (row(b, h, s), g_blk0 + h)),
            pl.BlockSpec((1, dv), lambda b, h, s: (0, h)),
            pl.BlockSpec((None, chunk, chunk), lambda b, h, s: (h, 0, 0)),
            pl.BlockSpec((None, chunk, dv), lambda b, h, s: (h, 0, 0)),
            pl.BlockSpec((None, chunk, dk), lambda b, h, s: (h, 0, 0)),
        ],
        out_specs=pl.BlockSpec((sb, dv), lambda b, h, s: (row(b, h, s), h)),
        out_shape=jax.ShapeDtypeStruct((t, width), BF16),
        scratch_shapes=[pltpu.VMEM((dk, dv), F32)],
        compiler_params=_params("parallel", "parallel", "arbitrary"),
        name="retention",
    )(chunk_decay, proj, proj, proj, proj, gn_g.reshape(1, width), d_intra, xi, zeta)


def _residual_chunks(a_ref, w_ref, x_ref, gate_ref, o_ref, want_ssq):
    ssq = None
    for c0 in range(0, o_ref.shape[1], OUT_CHUNK):
        cols = slice(c0, c0 + OUT_CHUNK)
        y = jnp.dot(a_ref[...], w_ref[:, cols], preferred_element_type=F32)
        z = x_ref[:, cols] + gate_ref[:, cols] * y
        o_ref[:, cols] = z
        if want_ssq:
            part = jnp.sum(z * z, axis=-1, keepdims=True)
            ssq = part if ssq is None else ssq + part
    return ssq


def _outproj_kernel(a_ref, w_ref, x_ref, gate_ref, o_ref):
    _residual_chunks(a_ref, w_ref, x_ref, gate_ref, o_ref, False)


def _outproj_final_kernel(a_ref, w_ref, x_ref, gate_ref, fg_ref, o_ref):
    ssq = _residual_chunks(a_ref, w_ref, x_ref, gate_ref, o_ref, True)
    inv = lax.rsqrt(ssq * (1.0 / o_ref.shape[1]) + EPS)
    o_ref[...] = o_ref[...] * inv * fg_ref[...]


def _outproj(a, w_bf16, x2d, mod, layer, seq, final_g=None):
    t, k = a.shape
    d = w_bf16.shape[1]
    tm = min(seq, 512)
    assert seq % tm == 0 and d % OUT_CHUNK == 0
    in_specs = [
        pl.BlockSpec((tm, k), lambda i: (i, 0)),
        _resident_spec(w_bf16.shape),
        pl.BlockSpec((tm, d), lambda i: (i, 0)),
        _mod_specs(layer, seq // tm, d)(2),
    ]
    args = [a, w_bf16, x2d, mod]
    kern = _outproj_kernel
    if final_g is not None:
        in_specs.append(pl.BlockSpec((1, d), lambda i: (0, 0)))
        args.append(final_g.reshape(1, d))
        kern = _outproj_final_kernel
    return pl.pallas_call(
        kern,
        grid=(t // tm,),
        in_specs=in_specs,
        out_specs=pl.BlockSpec((tm, d), lambda i: (i, 0)),
        out_shape=jax.ShapeDtypeStruct((t, d), F32),
        compiler_params=_params("parallel"),
        name="out_proj_final" if final_g is not None else "out_proj",
    )(*args)


def _mla_inproj_kernel(x_ref, g_ref, shift_ref, scale_ref, w_ref, o_ref, h_ref, *, gate_cols):
    h_ref[...] = _adaln_rmsnorm(x_ref, g_ref, shift_ref, scale_ref)
    for c0 in range(0, o_ref.shape[1], MXU_TILE):
        acc = jnp.dot(h_ref[...], w_ref[:, c0:c0 + MXU_TILE], preferred_element_type=F32)
        if gate_cols[0] <= c0 < gate_cols[1]:
            acc = _silu(acc)
        o_ref[:, c0:c0 + MXU_TILE] = acc.astype(BF16)


def _mla_inproj(x2d, norm_g, mod, layer, w_bf16, seq, gate_cols):
    t, d = x2d.shape
    n_out = w_bf16.shape[1]
    tm = min(seq, 512)
    assert seq % tm == 0 and n_out % MXU_TILE == 0
    assert gate_cols[0] % MXU_TILE == 0 and gate_cols[1] % MXU_TILE == 0
    mspec = _mod_specs(layer, seq // tm, d)
    return pl.pallas_call(
        functools.partial(_mla_inproj_kernel, gate_cols=gate_cols),
        grid=(t // tm,),
        in_specs=[
            pl.BlockSpec((tm, d), lambda i: (i, 0)),
            pl.BlockSpec((1, d), lambda i: (0, 0)),
            mspec(0), mspec(1),
            _resident_spec(w_bf16.shape),
        ],
        out_specs=pl.BlockSpec((tm, n_out), lambda i: (i, 0)),
        out_shape=jax.ShapeDtypeStruct((t, n_out), BF16),
        scratch_shapes=[pltpu.VMEM((tm, d), BF16)],
        compiler_params=_params("parallel"),
        name="mla_in_proj",
    )(x2d, norm_g.reshape(1, d), mod, mod, w_bf16)


def _mla_up_kernel(cq_ref, ckv_ref, kr_ref, qg_ref, kvg_ref, wqt_ref, wk_ref, wvt_ref, cost_ref,
                   sint_ref, qt_ref, k_ref, vt_ref, *, heads, q_scale, group):
    def rms(x_ref, g_ref):
        x = x_ref[...].astype(F32)
        return x * lax.rsqrt(jnp.mean(x * x, axis=-1, keepdims=True) + EPS) * g_ref[...]

    cqn_t = rms(cq_ref, qg_ref).T.astype(BF16)
    cos_t = cost_ref[...]
    sin_t = sint_ref[...]
    gw = group * MLA_HEAD_PAD
    for g0 in range(0, heads, group):
        qt = jnp.dot(wqt_ref[g0 * MLA_HEAD_PAD:g0 * MLA_HEAD_PAD + gw, :], cqn_t,
                     preferred_element_type=F32)
        for j in range(group):
            r0 = j * MLA_HEAD_PAD
            o0 = (g0 + j) * MLA_HEAD_PAD
            r1 = r0 + MLA_NOPE
            qt_ref[o0:o0 + MLA_NOPE, :] = (qt[r0:r1, :] * q_scale).astype(BF16)
            rope = (qt[r1:r1 + MLA_ROPE, :] * cos_t
                    + qt[r1 + MLA_ROPE:r1 + 2 * MLA_ROPE, :] * sin_t)
            qt_ref[o0 + MLA_NOPE:o0 + MLA_NOPE + MLA_ROPE, :] = (rope * q_scale).astype(BF16)
            qt_ref[o0 + MLA_NOPE + MLA_ROPE:o0 + MLA_HEAD_PAD, :] = jnp.zeros(
                (MLA_HEAD_PAD - MLA_NOPE - MLA_ROPE, qt.shape[1]), BF16)

    ckvn_f32 = rms(ckv_ref, kvg_ref)
    ckvn = ckvn_f32.astype(BF16)
    kr_t = kr_ref[...].astype(F32).T
    kr_rope_t = kr_t[:MLA_ROPE, :] * cos_t + kr_t[MLA_ROPE:, :] * sin_t
    kr = jnp.concatenate([kr_rope_t, jnp.zeros_like(kr_rope_t)], axis=0).T.astype(BF16)
    for g0 in range(0, heads, group):
        kn = jnp.dot(ckvn, wk_ref[:, g0 * MLA_NOPE:(g0 + group) * MLA_NOPE],
                     preferred_element_type=F32)
        for j in range(group):
            o0 = (g0 + j) * MLA_HEAD_PAD
            k_ref[:, o0:o0 + LANES] = kn[:, j * MLA_NOPE:(j + 1) * MLA_NOPE].astype(BF16)
            k_ref[:, o0 + LANES:o0 + 2 * LANES] = kr
    ckvn_t = ckvn_f32.T.astype(BF16)
    gv = group * MLA_V
    for r0 in range(0, heads * MLA_V, gv):
        vt_ref[r0:r0 + gv, :] = jnp.dot(wvt_ref[r0:r0 + gv, :], ckvn_t,
                                        preferred_element_type=F32).astype(BF16)


def _mla_up(proj, q_norm_g, kv_norm_g, wqt_bf16, wk_bf16, wvt_bf16, cos_t, sin_t, q_rank,
            kv_rank, kr_col, heads, tm):
    t = proj.shape[0]
    assert t % tm == 0 and q_rank == kv_rank and kr_col % LANES == 0
    return pl.pallas_call(
        functools.partial(_mla_up_kernel, heads=heads,
                          q_scale=LOG2_E * (MLA_NOPE + MLA_ROPE) ** -0.5, group=4),
        grid=(t // tm,),
        in_specs=[
            pl.BlockSpec((tm, q_rank), lambda i: (i, 0)),
            pl.BlockSpec((tm, kv_rank), lambda i: (i, 1)),
            pl.BlockSpec((tm, LANES), lambda i: (i, kr_col // LANES)),
            pl.BlockSpec((1, q_rank), lambda i: (0, 0)),
            pl.BlockSpec((1, kv_rank), lambda i: (0, 0)),
            _resident_spec(wqt_bf16.shape),
            _resident_spec(wk_bf16.shape),
            _resident_spec(wvt_bf16.shape),
            pl.BlockSpec((MLA_ROPE, tm), lambda i: (0, i)),
            pl.BlockSpec((MLA_ROPE, tm), lambda i: (0, i)),
        ],
        out_specs=[
            pl.BlockSpec((None, heads * MLA_HEAD_PAD, tm), lambda i: (i, 0, 0)),
            pl.BlockSpec((tm, heads * MLA_HEAD_PAD), lambda i: (i, 0)),
            pl.BlockSpec((None, heads * MLA_V, tm), lambda i: (i, 0, 0)),
        ],
        out_shape=[
            jax.ShapeDtypeStruct((t // tm, heads * MLA_HEAD_PAD, tm), BF16),
            jax.ShapeDtypeStruct((t, heads * MLA_HEAD_PAD), BF16),
            jax.ShapeDtypeStruct((t // tm, heads * MLA_V, tm), BF16),
        ],
        compiler_params=_params("parallel"),
        name="mla_up_proj",
    )(proj, proj, proj, q_norm_g.reshape(1, q_rank), kv_norm_g.reshape(1, kv_rank),
      wqt_bf16, wk_bf16, wvt_bf16, cos_t, sin_t)


def _attn_kernel(qt_ref, k_ref, vt_ref, sg_ref, o_ref, vt1_ref, s_ref, smax_ref, acc_ref, m_ref,
                 *, tk):
    nslab = ATTN_SLABS
    tq = nslab * tk
    nq = qt_ref.shape[0] // nslab
    all_q = slice(0, tq)

    def slabs(a, b):
        return slice(a * tk, b * tk)

    for j in range(vt_ref.shape[0]):
        vt1_ref[j, :MLA_V, :] = vt_ref[j]
        vt1_ref[j, MLA_V:, :] = jnp.ones((ONES_ROWS, tk), BF16)

    def put_scores(slot, t, qi, first_slab=0):
        kt = k_ref[pl.ds(pl.multiple_of(t * tk, tk), tk), :]
        for j in range(first_slab, nslab):
            st = jnp.dot(kt, qt_ref[nslab * qi + j], preferred_element_type=F32)
            s_ref[slot, :, slabs(j, j + 1)] = st
            smax_ref[slot, :, slabs(j, j + 1)] = jnp.max(st, axis=0, keepdims=True)

    def consume(t, slot, qs, diagonal):
        st = s_ref[slot, :, qs]
        if diagonal:
            key = lax.broadcasted_iota(jnp.int32, st.shape, 0)
            qry = lax.broadcasted_iota(jnp.int32, st.shape, 1)
            st = jnp.where(key <= qry, st, NEG_BIG)
            tile_max = jnp.max(st, axis=0, keepdims=True)
        else:
            tile_max = smax_ref[slot, :, qs]
        m_old = m_ref[:, qs]
        m_new = jnp.maximum(m_old, tile_max)
        p = jnp.exp2(st - m_new).astype(BF16)
        acc_ref[:, qs] = jnp.exp2(m_old - m_new) * acc_ref[:, qs] + jnp.dot(
            vt1_ref[t], p, preferred_element_type=F32)
        m_ref[:, qs] = m_new

    def finish(q0, qs):
        acc = acc_ref[:, qs]
        o = (acc[:MLA_V, :] * (1.0 / acc[MLA_V:MLA_V + 1, :])).T
        rows = pl.ds(pl.multiple_of(q0 + qs.start, tk), qs.stop - qs.start)
        o_ref[rows, :] = (sg_ref[rows, :].astype(F32) * o).astype(BF16)

    put_scores(0, 0, 0)

    def query_block(qi, carry):
        q0 = qi * tq
        m_ref[...] = jnp.full(m_ref.shape, NEG_BIG, F32)
        acc_ref[...] = jnp.zeros_like(acc_ref)

        def full_tiles(t0, n):
            for i in range(n):
                put_scores((i + 1) % 2, t0 + i + 1, qi)
                consume(t0 + i, i % 2, all_q, False)

        def loop_body(u, c):
            full_tiles(ATTN_LOOP_TILES * u, ATTN_LOOP_TILES)
            return c

        base = nslab * qi
        lax.fori_loop(0, base // ATTN_LOOP_TILES, loop_body, 0)
        if ATTN_LOOP_TILES == 2 * nslab:

            @pl.when(base % ATTN_LOOP_TILES != 0)
            def _():
                full_tiles(base - nslab, nslab)

        for j in range(nslab):
            last = j == nslab - 1
            if not last:
                put_scores((j + 1) % 2, base + j + 1, qi, first_slab=j + 1)
            consume(base + j, j % 2, slabs(j, j + 1), True)
            if last:
                put_scores(0, 0, jnp.minimum(qi + 1, nq - 1))
            else:
                consume(base + j, j % 2, slabs(j + 1, nslab), False)
            finish(q0, slabs(j, j + 1))
        return carry

    lax.fori_loop(0, nq, query_block, 0)


def _attention(qt, k, vt, proj, gate_col, batch, seq, heads, tk):
    t = k.shape[0]
    tq = ATTN_SLABS * tk
    assert seq % tq == 0 and gate_col % MLA_V == 0 and tk % LANES == 0 and ATTN_SLABS % 2 == 0
    assert ATTN_LOOP_TILES in (ATTN_SLABS, 2 * ATTN_SLABS)
    g_blk0 = gate_col // MLA_V
    return pl.pallas_call(
        functools.partial(_attn_kernel, tk=tk),
        grid=(batch, heads),
        in_specs=[
            pl.BlockSpec((seq // tk, MLA_HEAD_PAD, tk), lambda b, h: (b, h, 0)),
            pl.BlockSpec((seq, MLA_HEAD_PAD), lambda b, h: (b, h)),
            pl.BlockSpec((seq // tk, MLA_V, tk), lambda b, h: (b, h, 0)),
            pl.BlockSpec((seq, MLA_V), lambda b, h: (b, g_blk0 + h)),
        ],
        out_specs=pl.BlockSpec((seq, MLA_V), lambda b, h: (b, h)),
        out_shape=jax.ShapeDtypeStruct((t, heads * MLA_V), BF16),
        scratch_shapes=[
            pltpu.VMEM((seq // tk, MLA_V + ONES_ROWS, tk), BF16),
            pltpu.VMEM((2, tk, tq), F32),
            pltpu.VMEM((2, 1, tq), F32),
            pltpu.VMEM((MLA_V + ONES_ROWS, tq), F32),
            pltpu.VMEM((1, tq), F32),
        ],
        compiler_params=_params("parallel", "parallel"),
        name="mla_attention",
    )(qt, k, vt, proj)


def _rotate_half_cols(w):
    half = w.shape[-1] // 2
    return jnp.concatenate([-w[..., half:], w[..., :half]], axis=-1)


def _mla_weights(w_in, w_uq, w_ukv, q_rank, kv_rank, heads):
    d = w_in.shape[0]
    cq = w_in[:, :q_rank]
    ckv = w_in[:, q_rank:q_rank + kv_rank]
    kr = w_in[:, q_rank + kv_rank:q_rank + kv_rank + MLA_ROPE]
    gate = w_in[:, q_rank + kv_rank + MLA_ROPE:]
    gate_col = q_rank + kv_rank
    kr_col = gate_col + gate.shape[1]
    pad = jnp.zeros((d, -(kr_col + 2 * MLA_ROPE) % MXU_TILE), w_in.dtype)
    w_in_p = jnp.concatenate([cq, ckv, gate, kr, _rotate_half_cols(kr), pad],
                             axis=1).astype(BF16)
    uq = w_uq.reshape(q_rank, heads, MLA_NOPE + MLA_ROPE)
    rope = uq[..., MLA_NOPE:]
    uq_p = jnp.concatenate([uq[..., :MLA_NOPE], rope, _rotate_half_cols(rope)], axis=-1)
    uq_p = uq_p.reshape(q_rank, heads * MLA_HEAD_PAD).T.astype(BF16)
    ukv = w_ukv.reshape(kv_rank, heads, MLA_NOPE + MLA_V)
    uk_p = ukv[..., :MLA_NOPE].reshape(kv_rank, heads * MLA_NOPE).astype(BF16)
    uv_p = ukv[..., MLA_NOPE:].reshape(kv_rank, heads * MLA_V).T.astype(BF16)
    return w_in_p, uq_p, uk_p, uv_p, gate_col, kr_col


def kernel(x, c, positions, ada_w, ada_b, norm_g, ret_w_in, ret_gn_g, ret_w_out, mla_w_in,
           mla_q_norm_g, mla_w_uq, mla_kv_norm_g, mla_w_ukv, mla_w_out, final_norm_g):
    batch, seq, d = x.shape
    t = batch * seq
    depth = ada_w.shape[0]
    assert depth % 2 == 0, "the last layer must be a latent-attention layer (final norm fusion)"
    ret_dk = d // RET_HEADS
    ret_dv = ret_w_out.shape[1] // RET_HEADS
    q_rank = mla_q_norm_g.shape[1]
    kv_rank = mla_kv_norm_g.shape[1]

    mod = _modulation(c, ada_w, ada_b)

    pos = positions.reshape(t, 1).astype(F32)
    cos_r, sin_r = _rope_tables(jnp.broadcast_to(pos, (t, LANES)),
                                _inv_freq(ret_dk).reshape(1, LANES))
    cos_t, sin_t = _rope_tables_t(positions.reshape(1, t).astype(F32),
                                  _inv_freq(MLA_ROPE).reshape(MLA_ROPE // 2, 1))
    attn_tk = min(seq // ATTN_SLABS, 512)

    x2d = x.reshape(t, d)
    for layer in range(depth):
        j = layer // 2
        if layer % 2 == 0:
            proj = _ret_inproj(x2d, norm_g[layer], mod, layer, cos_r, sin_r,
                               ret_w_in[j].astype(BF16), seq, 2 * d,
                               2 * d + RET_HEADS * ret_dv, ret_dk)
            y = _retention(proj, ret_gn_g[j], batch, seq, RET_HEADS, ret_dk, ret_dv)
            x2d = _outproj(y, ret_w_out[j].astype(BF16), x2d, mod, layer, seq)
        else:
            w_in_p, uq_p, uk_p, uv_p, gate_col, kr_col = _mla_weights(
                mla_w_in[j], mla_w_uq[j], mla_w_ukv[j], q_rank, kv_rank, MLA_HEADS)
            proj = _mla_inproj(x2d, norm_g[layer], mod, layer, w_in_p, seq,
                               (gate_col, kr_col))
            qt, k, vt = _mla_up(proj, mla_q_norm_g[j], mla_kv_norm_g[j], uq_p, uk_p, uv_p,
                                cos_t, sin_t, q_rank, kv_rank, kr_col, MLA_HEADS, attn_tk)
            o = _attention(qt, k, vt, proj, gate_col, batch, seq, MLA_HEADS, attn_tk)
            final_g = final_norm_g if layer == depth - 1 else None
            x2d = _outproj(o, mla_w_out[j].astype(BF16), x2d, mod, layer, seq, final_g)
    return x2d.reshape(batch, seq, d)
```

```python
import functools

import jax
import jax.numpy as jnp
import numpy as np
from jax import lax
from jax.experimental import pallas as pl
from jax.experimental.pallas import tpu as pltpu

F32 = jnp.float32
BF16 = jnp.bfloat16

EPS = 1e-6
ROPE_BASE = 10000.0
NEG_BIG = -1e30
LOG2_E = 1.4426950408889634

LANES = 128
SUBLANES = 8
MXU_TILE = 256
OUT_CHUNK = 2 * MXU_TILE
VMEM_LIMIT_BYTES = 56 * 2**20

RET_HEADS = 8
RET_CHUNK = 256
RET_UNROLL = 2
MLA_HEADS = 16
MLA_NOPE = 128
MLA_ROPE = 64
MLA_V = 128
MLA_HEAD_PAD = 256
ONES_ROWS = 16
ATTN_SLABS = 2
ATTN_LOOP_TILES = 4


def _params(*semantics):
    return pltpu.CompilerParams(dimension_semantics=semantics,
                                vmem_limit_bytes=VMEM_LIMIT_BYTES)


def _silu(x):
    h = 0.5 * x
    return h + h * jnp.tanh(h)


def _mod_kernel(c_ref, w_ref, b_ref, o_ref):
    act = _silu(c_ref[...]).astype(BF16)
    o_ref[...] = jnp.dot(act, w_ref[...].astype(BF16),
                         preferred_element_type=F32) + b_ref[...]


def _modulation(c, ada_w, ada_b):
    depth, d, n3 = ada_w.shape
    b = c.shape[0]
    assert b <= SUBLANES
    tn = 768
    assert n3 % tn == 0
    c_pad = jnp.zeros((SUBLANES, d), F32).at[:b].set(c)
    out = pl.pallas_call(
        _mod_kernel,
        grid=(depth, n3 // tn),
        in_specs=[
            pl.BlockSpec((SUBLANES, d), lambda i, n: (0, 0)),
            pl.BlockSpec((None, d, tn), lambda i, n: (i, 0, n)),
            pl.BlockSpec((None, 1, tn), lambda i, n: (i, 0, n)),
        ],
        out_specs=pl.BlockSpec((None, SUBLANES, tn), lambda i, n: (i, 0, n)),
        out_shape=jax.ShapeDtypeStruct((depth, SUBLANES, n3), F32),
        compiler_params=_params("parallel", "parallel"),
        name="adaln_modulation",
    )(c_pad, ada_w, ada_b.reshape(depth, 1, n3))
    return out[:, :b].reshape(depth, b, 1, n3)


def _rope_table_kernel(pos_ref, freq_ref, cos_ref, sin_ref):
    ang = pos_ref[...] * freq_ref[...]
    cos_ref[...] = jnp.cos(ang)
    sin_ref[...] = jnp.sin(ang)


def _rope_tables(pos_lanes, freq_lanes):
    rows = pos_lanes.shape[0]
    tr = min(rows, 1024)
    assert rows % tr == 0
    spec = pl.BlockSpec((tr, LANES), lambda i: (i, 0))
    return pl.pallas_call(
        _rope_table_kernel,
        grid=(rows // tr,),
        in_specs=[spec, pl.BlockSpec((1, LANES), lambda i: (0, 0))],
        out_specs=[spec, spec],
        out_shape=[jax.ShapeDtypeStruct((rows, LANES), F32)] * 2,
        compiler_params=_params("parallel"),
        name="rope_tables",
    )(pos_lanes, freq_lanes)


def _rope_table_t_kernel(pos_ref, freq_ref, cos_ref, sin_ref):
    ang = freq_ref[...] * pos_ref[...]
    n = ang.shape[0]
    cos = jnp.cos(ang)
    sin = jnp.sin(ang)
    cos_ref[:n, :] = cos
    cos_ref[n:, :] = cos
    sin_ref[:n, :] = sin
    sin_ref[n:, :] = sin


def _rope_tables_t(pos_row, freq_col):
    t = pos_row.shape[1]
    n = freq_col.shape[0]
    tc = min(t, 4096)
    assert t % tc == 0
    out_spec = pl.BlockSpec((2 * n, tc), lambda i: (0, i))
    return pl.pallas_call(
        _rope_table_t_kernel,
        grid=(t // tc,),
        in_specs=[pl.BlockSpec((1, tc), lambda i: (0, i)),
                  pl.BlockSpec((n, 1), lambda i: (0, 0))],
        out_specs=[out_spec, out_spec],
        out_shape=[jax.ShapeDtypeStruct((2 * n, t), F32)] * 2,
        compiler_params=_params("parallel"),
        name="rope_tables_t",
    )(pos_row, freq_col)


def _inv_freq(d):
    return ROPE_BASE ** (-jnp.arange(0, d, 2, dtype=F32) / d)


def _adaln_rmsnorm(x_ref, g_ref, shift_ref, scale_ref):
    x = x_ref[...]
    y = x * lax.rsqrt(jnp.mean(x * x, axis=-1, keepdims=True) + EPS)
    return (y * (g_ref[...] * (1.0 + scale_ref[...])) + shift_ref[...]).astype(BF16)


def _mod_specs(layer, rows_per_batch_block, d):
    def spec(part):
        return pl.BlockSpec((None, None, 1, d),
                            lambda i, *_: (layer, i // rows_per_batch_block, 0, part))
    return spec


def _resident_spec(shape):
    return pl.BlockSpec(shape, lambda *_: (0,) * len(shape), pipeline_mode=pl.Buffered(1))


def _ret_inproj_kernel(x_ref, g_ref, shift_ref, scale_ref, cos_ref, sin_ref, w_ref,
                       o_ref, h_ref, *, rope_blocks, gate_block0, head_dim):
    n = pl.program_id(1)

    @pl.when(n == 0)
    def _():
        h_ref[...] = _adaln_rmsnorm(x_ref, g_ref, shift_ref, scale_ref)

    half = head_dim // 2
    tm = o_ref.shape[0]
    pieces = [(c0, r0) for c0 in range(0, o_ref.shape[1], head_dim)
              for r0 in range(0, tm, tm // 2)]

    def project(c0, r0):
        return jnp.dot(h_ref[r0:r0 + tm // 2, :], w_ref[:, c0:c0 + head_dim],
                       preferred_element_type=F32)

    @pl.when(n < rope_blocks)
    def _():
        for c0, r0 in pieces:
            rows = slice(r0, r0 + tm // 2)
            acc = project(c0, r0)
            x1 = acc[:, :half]
            x2 = acc[:, half:]
            cos = cos_ref[rows, :]
            sin = sin_ref[rows, :]
            o_ref[rows, c0:c0 + half] = (x1 * cos - x2 * sin).astype(BF16)
            o_ref[rows, c0 + half:c0 + head_dim] = (x2 * cos + x1 * sin).astype(BF16)

    @pl.when((n >= rope_blocks) & (n < gate_block0))
    def _():
        for c0, r0 in pieces:
            o_ref[r0:r0 + tm // 2, c0:c0 + head_dim] = project(c0, r0).astype(BF16)

    @pl.when(n >= gate_block0)
    def _():
        for c0, r0 in pieces:
            o_ref[r0:r0 + tm // 2, c0:c0 + head_dim] = _silu(project(c0, r0)).astype(BF16)


def _ret_inproj(x2d, norm_g, mod, layer, cos, sin, w_bf16, seq, rope_cols, gate_col, head_dim):
    t, d = x2d.shape
    n_out = w_bf16.shape[1]
    tm = min(seq, 1024)
    tn = 2048
    assert seq % tm == 0 and n_out % tn == 0 and rope_cols % tn == 0 and tn % head_dim == 0
    assert head_dim // 2 == LANES and gate_col % tn == 0
    mspec = _mod_specs(layer, seq // tm, d)
    return pl.pallas_call(
        functools.partial(_ret_inproj_kernel, rope_blocks=rope_cols // tn,
                          gate_block0=gate_col // tn, head_dim=head_dim),
        grid=(t // tm, n_out // tn),
        in_specs=[
            pl.BlockSpec((tm, d), lambda i, n: (i, 0)),
            pl.BlockSpec((1, d), lambda i, n: (0, 0)),
            mspec(0), mspec(1),
            pl.BlockSpec((tm, LANES), lambda i, n: (i, 0)),
            pl.BlockSpec((tm, LANES), lambda i, n: (i, 0)),
            pl.BlockSpec((d, tn), lambda i, n: (0, n)),
        ],
        out_specs=pl.BlockSpec((tm, tn), lambda i, n: (i, n)),
        out_shape=jax.ShapeDtypeStruct((t, n_out), BF16),
        scratch_shapes=[pltpu.VMEM((tm, d), BF16)],
        compiler_params=_params("parallel", "arbitrary"),
        name="ret_in_proj",
    )(x2d, norm_g.reshape(1, d), mod, mod, cos, sin, w_bf16)


def _retention_kernel(decay_ref, q_ref, k_ref, v_ref, sg_ref, gn_ref, dmat_ref, xi_ref,
                      zeta_ref, o_ref, state_ref, *, chunk):
    @pl.when(pl.program_id(2) == 0)
    def _():
        state_ref[...] = jnp.zeros_like(state_ref)

    decay = decay_ref[pl.program_id(1)]
    def one_chunk(r0):
        rows = pl.ds(r0, chunk)
        q = q_ref[rows, :]
        k = k_ref[rows, :]
        v = v_ref[rows, :]
        s = lax.dot_general(q, k, (((1,), (1,)), ((), ())), preferred_element_type=F32)
        kz = (k.astype(F32) * zeta_ref[...]).astype(BF16)
        update = lax.dot_general(kz, v, (((0,), (0,)), ((), ())), preferred_element_type=F32)
        s = (s * dmat_ref[...]).astype(BF16)
        state = state_ref[...]
        y = (jnp.dot(s, v, preferred_element_type=F32)
             + jnp.dot(q, state.astype(BF16), preferred_element_type=F32) * xi_ref[...])
        state_ref[...] = state * decay + update
        yc = y - jnp.mean(y, axis=-1, keepdims=True)
        var = jnp.mean(yc * yc, axis=-1, keepdims=True)
        yn = yc * lax.rsqrt(var + EPS) * gn_ref[...]
        o_ref[rows, :] = sg_ref[rows, :] * yn.astype(BF16)

    def body(it, carry):
        for c in range(RET_UNROLL):
            one_chunk(pl.multiple_of((it * RET_UNROLL + c) * chunk, chunk))
        return carry

    lax.fori_loop(0, q_ref.shape[0] // (chunk * RET_UNROLL), body, 0)


def _retention_constants(heads, chunk, dk, dv):
    f32 = np.float32
    log_gamma = np.log1p(-np.exp2(f32(-5.0) - np.arange(heads, dtype=f32))).astype(f32)
    idx = np.arange(chunk, dtype=f32)
    diff = idx[:, None] - idx[None, :]
    k_scale = f32(dk ** -0.5)
    d_intra = np.where(diff >= 0, np.exp(log_gamma[:, None, None] * np.maximum(diff, 0)),
                       0).astype(f32) * k_scale
    xi = np.exp(log_gamma[:, None] * (idx + 1)).astype(f32)
    zeta = np.exp(log_gamma[:, None] * (chunk - 1 - idx)).astype(f32) * k_scale
    chunk_decay = np.exp(log_gamma * chunk).astype(f32)
    bcast = lambda a, n: np.ascontiguousarray(np.broadcast_to(a[:, :, None], (heads, chunk, n)))
    return d_intra, bcast(xi, dv), bcast(zeta, dk), chunk_decay


def _retention(proj, gn_g, batch, seq, heads, dk, dv):
    t = proj.shape[0]
    width = heads * dv
    chunk = RET_CHUNK
    sb = min(seq, 1024)
    assert seq % sb == 0 and sb % chunk == 0
    nsb = seq // sb
    d_intra, xi, zeta, chunk_decay = _retention_constants(heads, chunk, dk, dv)
    k_blk0 = heads
    v_blk0 = 2 * heads * dk // dv
    g_blk0 = v_blk0 + heads
    row = lambda b, h, s: b * nsb + s
    return pl.pallas_call(
        functools.partial(_retention_kernel, chunk=chunk),
        grid=(batch, heads, nsb),
        in_specs=[
            pl.BlockSpec(memory_space=pltpu.SMEM),
            pl.BlockSpec((sb, dk), lambda b, h, s: (row(b, h, s), h)),
            pl.BlockSpec((sb, dk), lambda b, h, s: (row(b, h, s), k_blk0 + h)),
            pl.BlockSpec((sb, dv), lambda b, h, s: (row(b, h, s), v_blk0 + h)),
            pl.BlockSpec((sb, dv), lambda b, h, s: (row(b, h, s), g_blk0 + h)),
            pl.BlockSpec((1, dv), lambda b, h, s: (0, h)),
            pl.BlockSpec((None, chunk, chunk), lambda b, h, s: (h, 0, 0)),
            pl.BlockSpec((None, chunk, dv), lambda b, h, s: (h, 0, 0)),
            pl.BlockSpec((None, chunk, dk), lambda b, h, s: (h, 0, 0)),
        ],
        out_specs=pl.BlockSpec((sb, dv), lambda b, h, s: (row(b, h, s), h)),
        out_shape=jax.ShapeDtypeStruct((t, width), BF16),
        scratch_shapes=[pltpu.VMEM((dk, dv), F32)],
        compiler_params=_params("parallel", "parallel", "arbitrary"),
        name="retention",
    )(chunk_decay, proj, proj, proj, proj, gn_g.reshape(1, width), d_intra, xi, zeta)


def _residual_chunks(a_ref, w_ref, x_ref, gate_ref, o_ref, want_ssq):
    ssq = None
    for c0 in range(0, o_ref.shape[1], OUT_CHUNK):
        cols = slice(c0, c0 + OUT_CHUNK)
        y = jnp.dot(a_ref[...], w_ref[:, cols], preferred_element_type=F32)
        z = x_ref[:, cols] + gate_ref[:, cols] * y
        o_ref[:, cols] = z
        if want_ssq:
            part = jnp.sum(z * z, axis=-1, keepdims=True)
            ssq = part if ssq is None else ssq + part
    return ssq


def _outproj_kernel(a_ref, w_ref, x_ref, gate_ref, o_ref):
    _residual_chunks(a_ref, w_ref, x_ref, gate_ref, o_ref, False)


def _outproj_final_kernel(a_ref, w_ref, x_ref, gate_ref, fg_ref, o_ref):
    ssq = _residual_chunks(a_ref, w_ref, x_ref, gate_ref, o_ref, True)
    inv = lax.rsqrt(ssq * (1.0 / o_ref.shape[1]) + EPS)
    o_ref[...] = o_ref[...] * inv * fg_ref[...]


def _outproj(a, w_bf16, x2d, mod, layer, seq, final_g=None):
    t, k = a.shape
    d = w_bf16.shape[1]
    tm = min(seq, 512)
    assert seq % tm == 0 and d % OUT_CHUNK == 0
    in_specs = [
        pl.BlockSpec((tm, k), lambda i: (i, 0)),
        _resident_spec(w_bf16.shape),
        pl.BlockSpec((tm, d), lambda i: (i, 0)),
        _mod_specs(layer, seq // tm, d)(2),
    ]
    args = [a, w_bf16, x2d, mod]
    kern = _outproj_kernel
    if final_g is not None:
        in_specs.append(pl.BlockSpec((1, d), lambda i: (0, 0)))
        args.append(final_g.reshape(1, d))
        kern = _outproj_final_kernel
    return pl.pallas_call(
        kern,
        grid=(t // tm,),
        in_specs=in_specs,
        out_specs=pl.BlockSpec((tm, d), lambda i: (i, 0)),
        out_shape=jax.ShapeDtypeStruct((t, d), F32),
        compiler_params=_params("parallel"),
        name="out_proj_final" if final_g is not None else "out_proj",
    )(*args)


def _mla_inproj_kernel(x_ref, g_ref, shift_ref, scale_ref, w_ref, o_ref, h_ref, *, gate_cols):
    h_ref[...] = _adaln_rmsnorm(x_ref, g_ref, shift_ref, scale_ref)
    for c0 in range(0, o_ref.shape[1], MXU_TILE):
        acc = jnp.dot(h_ref[...], w_ref[:, c0:c0 + MXU_TILE], preferred_element_type=F32)
        if gate_cols[0] <= c0 < gate_cols[1]:
            acc = _silu(acc)
        o_ref[:, c0:c0 + MXU_TILE] = acc.astype(BF16)


def _mla_inproj(x2d, norm_g, mod, layer, w_bf16, seq, gate_cols):
    t, d = x2d.shape
    n_out = w_bf16.shape[1]
    tm = min(seq, 512)
    assert seq % tm == 0 and n_out % MXU_TILE == 0
    assert gate_cols[0] % MXU_TILE == 0 and gate_cols[1] % MXU_TILE == 0
    mspec = _mod_specs(layer, seq // tm, d)
    return pl.pallas_call(
        functools.partial(_mla_inproj_kernel, gate_cols=gate_cols),
        grid=(t // tm,),
        in_specs=[
            pl.BlockSpec((tm, d), lambda i: (i, 0)),
            pl.BlockSpec((1, d), lambda i: (0, 0)),
            mspec(0), mspec(1),
            _resident_spec(w_bf16.shape),
        ],
        out_specs=pl.BlockSpec((tm, n_out), lambda i: (i, 0)),
        out_shape=jax.ShapeDtypeStruct((t, n_out), BF16),
        scratch_shapes=[pltpu.VMEM((tm, d), BF16)],
        compiler_params=_params("parallel"),
        name="mla_in_proj",
    )(x2d, norm_g.reshape(1, d), mod, mod, w_bf16)


def _mla_up_kernel(cq_ref, ckv_ref, kr_ref, qg_ref, kvg_ref, wqt_ref, wk_ref, wvt_ref, cost_ref,
                   sint_ref, qt_ref, k_ref, vt_ref, *, heads, q_scale, group):
    def rms(x_ref, g_ref):
        x = x_ref[...].astype(F32)
        return x * lax.rsqrt(jnp.mean(x * x, axis=-1, keepdims=True) + EPS) * g_ref[...]

    cqn_t = rms(cq_ref, qg_ref).T.astype(BF16)
    cos_t = cost_ref[...]
    sin_t = sint_ref[...]
    gw = group * MLA_HEAD_PAD
    for g0 in range(0, heads, group):
        qt = jnp.dot(wqt_ref[g0 * MLA_HEAD_PAD:g0 * MLA_HEAD_PAD + gw, :], cqn_t,
                     preferred_element_type=F32)
        for j in range(group):
            r0 = j * MLA_HEAD_PAD
            o0 = (g0 + j) * MLA_HEAD_PAD
            r1 = r0 + MLA_NOPE
            qt_ref[o0:o0 + MLA_NOPE, :] = (qt[r0:r1, :] * q_scale).astype(BF16)
            rope = (qt[r1:r1 + MLA_ROPE, :] * cos_t
                    + qt[r1 + MLA_ROPE:r1 + 2 * MLA_ROPE, :] * sin_t)
            qt_ref[o0 + MLA_NOPE:o0 + MLA_NOPE + MLA_ROPE, :] = (rope * q_scale).astype(BF16)
            qt_ref[o0 + MLA_NOPE + MLA_ROPE:o0 + MLA_HEAD_PAD, :] = jnp.zeros(
                (MLA_HEAD_PAD - MLA_NOPE - MLA_ROPE, qt.shape[1]), BF16)

    ckvn_f32 = rms(ckv_ref, kvg_ref)
    ckvn = ckvn_f32.astype(BF16)
    kr_t = kr_ref[...].astype(F32).T
    kr_rope_t = kr_t[:MLA_ROPE, :] * cos_t + kr_t[MLA_ROPE:, :] * sin_t
    kr = jnp.concatenate([kr_rope_t, jnp.zeros_like(kr_rope_t)], axis=0).T.astype(BF16)
    for g0 in range(0, heads, group):
        kn = jnp.dot(ckvn, wk_ref[:, g0 * MLA_NOPE:(g0 + group) * MLA_NOPE],
                     preferred_element_type=F32)
        for j in range(group):
            o0 = (g0 + j) * MLA_HEAD_PAD
            k_ref[:, o0:o0 + LANES] = kn[:, j * MLA_NOPE:(j + 1) * MLA_NOPE].astype(BF16)
            k_ref[:, o0 + LANES:o0 + 2 * LANES] = kr
    ckvn_t = ckvn_f32.T.astype(BF16)
    gv = group * MLA_V
    for r0 in range(0, heads * MLA_V, gv):
        vt_ref[r0:r0 + gv, :] = jnp.dot(wvt_ref[r0:r0 + gv, :], ckvn_t,
                                        preferred_element_type=F32).astype(BF16)


def _mla_up(proj, q_norm_g, kv_norm_g, wqt_bf16, wk_bf16, wvt_bf16, cos_t, sin_t, q_rank,
            kv_rank, kr_col, heads, tm):
    t = proj.shape[0]
    assert t % tm == 0 and q_rank == kv_rank and kr_col % LANES == 0
    return pl.pallas_call(
        functools.partial(_mla_up_kernel, heads=heads,
                          q_scale=LOG2_E * (MLA_NOPE + MLA_ROPE) ** -0.5, group=4),
        grid=(t // tm,),
        in_specs=[
            pl.BlockSpec((tm, q_rank), lambda i: (i, 0)),
            pl.BlockSpec((tm, kv_rank), lambda i: (i, 1)),
            pl.BlockSpec((tm, LANES), lambda i: (i, kr_col // LANES)),
            pl.BlockSpec((1, q_rank), lambda i: (0, 0)),
            pl.BlockSpec((1, kv_rank), lambda i: (0, 0)),
            _resident_spec(wqt_bf16.shape),
            _resident_spec(wk_bf16.shape),
            _resident_spec(wvt_bf16.shape),
            pl.BlockSpec((MLA_ROPE, tm), lambda i: (0, i)),
            pl.BlockSpec((MLA_ROPE, tm), lambda i: (0, i)),
        ],
        out_specs=[
            pl.BlockSpec((None, heads * MLA_HEAD_PAD, tm), lambda i: (i, 0, 0)),
            pl.BlockSpec((tm, heads * MLA_HEAD_PAD), lambda i: (i, 0)),
            pl.BlockSpec((None, heads * MLA_V, tm), lambda i: (i, 0, 0)),
        ],
        out_shape=[
            jax.ShapeDtypeStruct((t // tm, heads * MLA_HEAD_PAD, tm), BF16),
            jax.ShapeDtypeStruct((t, heads * MLA_HEAD_PAD), BF16),
            jax.ShapeDtypeStruct((t // tm, heads * MLA_V, tm), BF16),
        ],
        compiler_params=_params("parallel"),
        name="mla_up_proj",
    )(proj, proj, proj, q_norm_g.reshape(1, q_rank), kv_norm_g.reshape(1, kv_rank),
      wqt_bf16, wk_bf16, wvt_bf16, cos_t, sin_t)


def _attn_kernel(qt_ref, k_ref, vt_ref, sg_ref, o_ref, vt1_ref, s_ref, smax_ref, acc_ref, m_ref,
                 *, tk):
    nslab = ATTN_SLABS
    tq = nslab * tk
    nq = qt_ref.shape[0] // nslab
    all_q = slice(0, tq)

    def slabs(a, b):
        return slice(a * tk, b * tk)

    for j in range(vt_ref.shape[0]):
        vt1_ref[j, :MLA_V, :] = vt_ref[j]
        vt1_ref[j, MLA_V:, :] = jnp.ones((ONES_ROWS, tk), BF16)

    def put_scores(slot, t, qi, first_slab=0):
        kt = k_ref[pl.ds(pl.multiple_of(t * tk, tk), tk), :]
        for j in range(first_slab, nslab):
            st = jnp.dot(kt, qt_ref[nslab * qi + j], preferred_element_type=F32)
            s_ref[slot, :, slabs(j, j + 1)] = st
            smax_ref[slot, :, slabs(j, j + 1)] = jnp.max(st, axis=0, keepdims=True)

    def consume(t, slot, qs, diagonal):
        st = s_ref[slot, :, qs]
        if diagonal:
            key = lax.broadcasted_iota(jnp.int32, st.shape, 0)
            qry = lax.broadcasted_iota(jnp.int32, st.shape, 1)
            st = jnp.where(key <= qry, st, NEG_BIG)
            tile_max = jnp.max(st, axis=0, keepdims=True)
        else:
            tile_max = smax_ref[slot, :, qs]
        m_old = m_ref[:, qs]
        m_new = jnp.maximum(m_old, tile_max)
        p = jnp.exp2(st - m_new).astype(BF16)
        acc_ref[:, qs] = jnp.exp2(m_old - m_new) * acc_ref[:, qs] + jnp.dot(
            vt1_ref[t], p, preferred_element_type=F32)
        m_ref[:, qs] = m_new

    def finish(q0, qs):
        acc = acc_ref[:, qs]
        o = (acc[:MLA_V, :] * (1.0 / acc[MLA_V:MLA_V + 1, :])).T
        rows = pl.ds(pl.multiple_of(q0 + qs.start, tk), qs.stop - qs.start)
        o_ref[rows, :] = (sg_ref[rows, :].astype(F32) * o).astype(BF16)

    put_scores(0, 0, 0)

    def query_block(qi, carry):
        q0 = qi * tq
        m_ref[...] = jnp.full(m_ref.shape, NEG_BIG, F32)
        acc_ref[...] = jnp.zeros_like(acc_ref)

        def full_tiles(t0, n):
            for i in range(n):
                put_scores((i + 1) % 2, t0 + i + 1, qi)
                consume(t0 + i, i % 2, all_q, False)

        def loop_body(u, c):
            full_tiles(ATTN_LOOP_TILES * u, ATTN_LOOP_TILES)
            return c

        base = nslab * qi
        lax.fori_loop(0, base // ATTN_LOOP_TILES, loop_body, 0)
        if ATTN_LOOP_TILES == 2 * nslab:

            @pl.when(base % ATTN_LOOP_TILES != 0)
            def _():
                full_tiles(base - nslab, nslab)

        for j in range(nslab):
            last = j == nslab - 1
            if not last:
                put_scores((j + 1) % 2, base + j + 1, qi, first_slab=j + 1)
            consume(base + j, j % 2, slabs(j, j + 1), True)
            if last:
                put_scores(0, 0, jnp.minimum(qi + 1, nq - 1))
            else:
                consume(base + j, j % 2, slabs(j + 1, nslab), False)
            finish(q0, slabs(j, j + 1))
        return carry

    lax.fori_loop(0, nq, query_block, 0)


def _attention(qt, k, vt, proj, gate_col, batch, seq, heads, tk):
    t = k.shape[0]
    tq = ATTN_SLABS * tk
    assert seq % tq == 0 and gate_col % MLA_V == 0 and tk % LANES == 0 and ATTN_SLABS % 2 == 0
    assert ATTN_LOOP_TILES in (ATTN_SLABS, 2 * ATTN_SLABS)
    g_blk0 = gate_col // MLA_V
    return pl.pallas_call(
        functools.partial(_attn_kernel, tk=tk),
        grid=(batch, heads),
        in_specs=[
            pl.BlockSpec((seq // tk, MLA_HEAD_PAD, tk), lambda b, h: (b, h, 0)),
            pl.BlockSpec((seq, MLA_HEAD_PAD), lambda b, h: (b, h)),
            pl.BlockSpec((seq // tk, MLA_V, tk), lambda b, h: (b, h, 0)),
            pl.BlockSpec((seq, MLA_V), lambda b, h: (b, g_blk0 + h)),
        ],
        out_specs=pl.BlockSpec((seq, MLA_V), lambda b, h: (b, h)),
        out_shape=jax.ShapeDtypeStruct((t, heads * MLA_V), BF16),
        scratch_shapes=[
            pltpu.VMEM((seq // tk, MLA_V + ONES_ROWS, tk), BF16),
            pltpu.VMEM((2, tk, tq), F32),
            pltpu.VMEM((2, 1, tq), F32),
            pltpu.VMEM((MLA_V + ONES_ROWS, tq), F32),
            pltpu.VMEM((1, tq), F32),
        ],
        compiler_params=_params("parallel", "parallel"),
        name="mla_attention",
    )(qt, k, vt, proj)


def _rotate_half_cols(w):
    half = w.shape[-1] // 2
    return jnp.concatenate([-w[..., half:], w[..., :half]], axis=-1)


def _mla_weights(w_in, w_uq, w_ukv, q_rank, kv_rank, heads):
    d = w_in.shape[0]
    cq = w_in[:, :q_rank]
    ckv = w_in[:, q_rank:q_rank + kv_rank]
    kr = w_in[:, q_rank + kv_rank:q_rank + kv_rank + MLA_ROPE]
    gate = w_in[:, q_rank + kv_rank + MLA_ROPE:]
    gate_col = q_rank + kv_rank
    kr_col = gate_col + gate.shape[1]
    pad = jnp.zeros((d, -(kr_col + 2 * MLA_ROPE) % MXU_TILE), w_in.dtype)
    w_in_p = jnp.concatenate([cq, ckv, gate, kr, _rotate_half_cols(kr), pad],
                             axis=1).astype(BF16)
    uq = w_uq.reshape(q_rank, heads, MLA_NOPE + MLA_ROPE)
    rope = uq[..., MLA_NOPE:]
    uq_p = jnp.concatenate([uq[..., :MLA_NOPE], rope, _rotate_half_cols(rope)], axis=-1)
    uq_p = uq_p.reshape(q_rank, heads * MLA_HEAD_PAD).T.astype(BF16)
    ukv = w_ukv.reshape(kv_rank, heads, MLA_NOPE + MLA_V)
    uk_p = ukv[..., :MLA_NOPE].reshape(kv_rank, heads * MLA_NOPE).astype(BF16)
    uv_p = ukv[..., MLA_NOPE:].reshape(kv_rank, heads * MLA_V).T.astype(BF16)
    return w_in_p, uq_p, uk_p, uv_p, gate_col, kr_col


def kernel(x, c, positions, ada_w, ada_b, norm_g, ret_w_in, ret_gn_g, ret_w_out, mla_w_in,
           mla_q_norm_g, mla_w_uq, mla_kv_norm_g, mla_w_ukv, mla_w_out, final_norm_g):
    batch, seq, d = x.shape
    t = batch * seq
    depth = ada_w.shape[0]
    assert depth % 2 == 0, "the last layer must be a latent-attention layer (final norm fusion)"
    ret_dk = d // RET_HEADS
    ret_dv = ret_w_out.shape[1] // RET_HEADS
    q_rank = mla_q_norm_g.shape[1]
    kv_rank = mla_kv_norm_g.shape[1]

    mod = _modulation(c, ada_w, ada_b)

    pos = positions.reshape(t, 1).astype(F32)
    cos_r, sin_r = _rope_tables(jnp.broadcast_to(pos, (t, LANES)),
                                _inv_freq(ret_dk).reshape(1, LANES))
    cos_t, sin_t = _rope_tables_t(positions.reshape(1, t).astype(F32),
                                  _inv_freq(MLA_ROPE).reshape(MLA_ROPE // 2, 1))
    attn_tk = min(seq // ATTN_SLABS, 512)

    x2d = x.reshape(t, d)
    for layer in range(depth):
        j = layer // 2
        if layer % 2 == 0:
            proj = _ret_inproj(x2d, norm_g[layer], mod, layer, cos_r, sin_r,
                               ret_w_in[j].astype(BF16), seq, 2 * d,
                               2 * d + RET_HEADS * ret_dv, ret_dk)
            y = _retention(proj, ret_gn_g[j], batch, seq, RET_HEADS, ret_dk, ret_dv)
            x2d = _outproj(y, ret_w_out[j].astype(BF16), x2d, mod, layer, seq)
        else:
            w_in_p, uq_p, uk_p, uv_p, gate_col, kr_col = _mla_weights(
                mla_w_in[j], mla_w_uq[j], mla_w_ukv[j], q_rank, kv_rank, MLA_HEADS)
            proj = _mla_inproj(x2d, norm_g[layer], mod, layer, w_in_p, seq,
                               (gate_col, kr_col))
            qt, k, vt = _mla_up(proj, mla_q_norm_g[j], mla_kv_norm_g[j], uq_p, uk_p, uv_p,
                                cos_t, sin_t, q_rank, kv_rank, kr_col, MLA_HEADS, attn_tk)
            o = _attention(qt, k, vt, proj, gate_col, batch, seq, MLA_HEADS, attn_tk)
            final_g = final_norm_g if layer == depth - 1 else None
            x2d = _outproj(o, mla_w_out[j].astype(BF16), x2d, mod, layer, seq, final_g)
    return x2d.reshape(batch, seq, d)
```

```python
import functools

import jax
import jax.numpy as jnp
import numpy as np
from jax import lax
from jax.experimental import pallas as pl
from jax.experimental.pallas import tpu as pltpu

F32 = jnp.float32
BF16 = jnp.bfloat16

EPS = 1e-6
ROPE_BASE = 10000.0
NEG_BIG = -1e30
LOG2_E = 1.4426950408889634

LANES = 128
SUBLANES = 8
MXU_TILE = 256
OUT_CHUNK = 2 * MXU_TILE
VMEM_LIMIT_BYTES = 56 * 2**20

RET_HEADS = 8
RET_CHUNK = 256
MLA_HEADS = 16
MLA_NOPE = 128
MLA_ROPE = 64
MLA_V = 128
MLA_HEAD_PAD = 256
ONES_ROWS = 16
ATTN_SLABS = 2
ATTN_LOOP_TILES = 4


def _params(*semantics):
    return pltpu.CompilerParams(dimension_semantics=semantics,
                                vmem_limit_bytes=VMEM_LIMIT_BYTES)


def _silu(x):
    h = 0.5 * x
    return h + h * jnp.tanh(h)


def _mod_kernel(c_ref, w_ref, b_ref, o_ref):
    act = _silu(c_ref[...]).astype(BF16)
    o_ref[...] = jnp.dot(act, w_ref[...].astype(BF16),
                         preferred_element_type=F32) + b_ref[...]


def _modulation(c, ada_w, ada_b):
    depth, d, n3 = ada_w.shape
    b = c.shape[0]
    assert b <= SUBLANES
    tn = 768
    assert n3 % tn == 0
    c_pad = jnp.zeros((SUBLANES, d), F32).at[:b].set(c)
    out = pl.pallas_call(
        _mod_kernel,
        grid=(depth, n3 // tn),
        in_specs=[
            pl.BlockSpec((SUBLANES, d), lambda i, n: (0, 0)),
            pl.BlockSpec((None, d, tn), lambda i, n: (i, 0, n)),
            pl.BlockSpec((None, 1, tn), lambda i, n: (i, 0, n)),
        ],
        out_specs=pl.BlockSpec((None, SUBLANES, tn), lambda i, n: (i, 0, n)),
        out_shape=jax.ShapeDtypeStruct((depth, SUBLANES, n3), F32),
        compiler_params=_params("parallel", "parallel"),
        name="adaln_modulation",
    )(c_pad, ada_w, ada_b.reshape(depth, 1, n3))
    return out[:, :b].reshape(depth, b, 1, n3)


def _rope_table_kernel(pos_ref, freq_ref, cos_ref, sin_ref):
    ang = pos_ref[...] * freq_ref[...]
    cos_ref[...] = jnp.cos(ang)
    sin_ref[...] = jnp.sin(ang)


def _rope_tables(pos_lanes, freq_lanes):
    rows = pos_lanes.shape[0]
    tr = min(rows, 1024)
    assert rows % tr == 0
    spec = pl.BlockSpec((tr, LANES), lambda i: (i, 0))
    return pl.pallas_call(
        _rope_table_kernel,
        grid=(rows // tr,),
        in_specs=[spec, pl.BlockSpec((1, LANES), lambda i: (0, 0))],
        out_specs=[spec, spec],
        out_shape=[jax.ShapeDtypeStruct((rows, LANES), F32)] * 2,
        compiler_params=_params("parallel"),
        name="rope_tables",
    )(pos_lanes, freq_lanes)


def _rope_table_t_kernel(pos_ref, freq_ref, cos_ref, sin_ref):
    ang = freq_ref[...] * pos_ref[...]
    n = ang.shape[0]
    cos = jnp.cos(ang)
    sin = jnp.sin(ang)
    cos_ref[:n, :] = cos
    cos_ref[n:, :] = cos
    sin_ref[:n, :] = sin
    sin_ref[n:, :] = sin


def _rope_tables_t(pos_row, freq_col):
    t = pos_row.shape[1]
    n = freq_col.shape[0]
    tc = min(t, 4096)
    assert t % tc == 0
    out_spec = pl.BlockSpec((2 * n, tc), lambda i: (0, i))
    return pl.pallas_call(
        _rope_table_t_kernel,
        grid=(t // tc,),
        in_specs=[pl.BlockSpec((1, tc), lambda i: (0, i)),
                  pl.BlockSpec((n, 1), lambda i: (0, 0))],
        out_specs=[out_spec, out_spec],
        out_shape=[jax.ShapeDtypeStruct((2 * n, t), F32)] * 2,
        compiler_params=_params("parallel"),
        name="rope_tables_t",
    )(pos_row, freq_col)


def _inv_freq(d):
    return ROPE_BASE ** (-jnp.arange(0, d, 2, dtype=F32) / d)


def _adaln_rmsnorm(x_ref, g_ref, shift_ref, scale_ref, rows=slice(None)):
    x = x_ref[rows, :]
    y = x * lax.rsqrt(jnp.mean(x * x, axis=-1, keepdims=True) + EPS)
    return (y * (g_ref[...] * (1.0 + scale_ref[...])) + shift_ref[...]).astype(BF16)


def _mod_specs(layer, rows_per_batch_block, d):
    def spec(part):
        return pl.BlockSpec((None, None, 1, d),
                            lambda i, *_: (layer, i // rows_per_batch_block, 0, part))
    return spec


def _resident_spec(shape):
    return pl.BlockSpec(shape, lambda *_: (0,) * len(shape), pipeline_mode=pl.Buffered(1))


def _ret_inproj_kernel(x_ref, g_ref, shift_ref, scale_ref, cos_ref, sin_ref, w_ref,
                       o_ref, h_ref, *, rope_blocks, gate_block0, head_dim):
    n = pl.program_id(1)
    half = head_dim // 2
    tm = o_ref.shape[0]
    chunks = range(0, o_ref.shape[1], head_dim)
    halves = range(0, tm, tm // 2)
    pieces = [(c0, r0) for c0 in chunks for r0 in halves]

    def project(c0, r0):
        return jnp.dot(h_ref[r0:r0 + tm // 2, :], w_ref[:, c0:c0 + head_dim],
                       preferred_element_type=F32)

    def rope_piece(c0, r0):
        rows = slice(r0, r0 + tm // 2)
        acc = project(c0, r0)
        x1 = acc[:, :half]
        x2 = acc[:, half:]
        cos = cos_ref[rows, :]
        sin = sin_ref[rows, :]
        o_ref[rows, c0:c0 + half] = (x1 * cos - x2 * sin).astype(BF16)
        o_ref[rows, c0 + half:c0 + head_dim] = (x2 * cos + x1 * sin).astype(BF16)

    @pl.when(n == 0)
    def _():
        for r0 in halves:
            rows = slice(r0, r0 + tm // 2)
            h_ref[rows, :] = _adaln_rmsnorm(x_ref, g_ref, shift_ref, scale_ref, rows)
            for c0 in chunks:
                rope_piece(c0, r0)

    @pl.when((n > 0) & (n < rope_blocks))
    def _():
        for c0, r0 in pieces:
            rope_piece(c0, r0)

    @pl.when((n >= rope_blocks) & (n < gate_block0))
    def _():
        for c0, r0 in pieces:
            o_ref[r0:r0 + tm // 2, c0:c0 + head_dim] = project(c0, r0).astype(BF16)

    @pl.when(n >= gate_block0)
    def _():
        for c0, r0 in pieces:
            o_ref[r0:r0 + tm // 2, c0:c0 + head_dim] = _silu(project(c0, r0)).astype(BF16)


def _ret_inproj(x2d, norm_g, mod, layer, cos, sin, w_bf16, seq, rope_cols, gate_col, head_dim):
    t, d = x2d.shape
    n_out = w_bf16.shape[1]
    tm = min(seq, 1024)
    tn = 2048
    assert seq % tm == 0 and n_out % tn == 0 and rope_cols % tn == 0 and tn % head_dim == 0
    assert head_dim // 2 == LANES and gate_col % tn == 0
    mspec = _mod_specs(layer, seq // tm, d)
    return pl.pallas_call(
        functools.partial(_ret_inproj_kernel, rope_blocks=rope_cols // tn,
                          gate_block0=gate_col // tn, head_dim=head_dim),
        grid=(t // tm, n_out // tn),
        in_specs=[
            pl.BlockSpec((tm, d), lambda i, n: (i, 0)),
            pl.BlockSpec((1, d), lambda i, n: (0, 0)),
            mspec(0), mspec(1),
            pl.BlockSpec((tm, LANES), lambda i, n: (i, 0)),
            pl.BlockSpec((tm, LANES), lambda i, n: (i, 0)),
            pl.BlockSpec((d, tn), lambda i, n: (0, n)),
        ],
        out_specs=pl.BlockSpec((tm, tn), lambda i, n: (i, n)),
        out_shape=jax.ShapeDtypeStruct((t, n_out), BF16),
        scratch_shapes=[pltpu.VMEM((tm, d), BF16)],
        compiler_params=_params("parallel", "arbitrary"),
        name="ret_in_proj",
    )(x2d, norm_g.reshape(1, d), mod, mod, cos, sin, w_bf16)


def _retention_kernel(decay_ref, q_ref, k_ref, v_ref, sg_ref, gn_ref, dmat_ref, xi_ref,
                      zeta_ref, o_ref, state_ref, *, chunk):
    @pl.when(pl.program_id(2) == 0)
    def _():
        state_ref[...] = jnp.zeros_like(state_ref)

    decay = decay_ref[pl.program_id(1)]
    for c in range(q_ref.shape[0] // chunk):
        rows = slice(c * chunk, (c + 1) * chunk)
        q = q_ref[rows, :]
        k = k_ref[rows, :]
        v = v_ref[rows, :]
        s = lax.dot_general(q, k, (((1,), (1,)), ((), ())), preferred_element_type=F32)
        kz = (k.astype(F32) * zeta_ref[...]).astype(BF16)
        update = lax.dot_general(kz, v, (((0,), (0,)), ((), ())), preferred_element_type=F32)
        s = (s * dmat_ref[...]).astype(BF16)
        state = state_ref[...]
        y = (jnp.dot(s, v, preferred_element_type=F32)
             + jnp.dot(q, state.astype(BF16), preferred_element_type=F32) * xi_ref[...])
        state_ref[...] = state * decay + update
        yc = y - jnp.mean(y, axis=-1, keepdims=True)
        var = jnp.mean(yc * yc, axis=-1, keepdims=True)
        yn = yc * lax.rsqrt(var + EPS) * gn_ref[...]
        o_ref[rows, :] = sg_ref[rows, :] * yn.astype(BF16)


def _retention_constants(heads, chunk, dk, dv):
    f32 = np.float32
    log_gamma = np.log1p(-np.exp2(f32(-5.0) - np.arange(heads, dtype=f32))).astype(f32)
    idx = np.arange(chunk, dtype=f32)
    diff = idx[:, None] - idx[None, :]
    k_scale = f32(dk ** -0.5)
    d_intra = np.where(diff >= 0, np.exp(log_gamma[:, None, None] * np.maximum(diff, 0)),
                       0).astype(f32) * k_scale
    xi = np.exp(log_gamma[:, None] * (idx + 1)).astype(f32)
    zeta = np.exp(log_gamma[:, None] * (chunk - 1 - idx)).astype(f32) * k_scale
    chunk_decay = np.exp(log_gamma * chunk).astype(f32)
    bcast = lambda a, n: np.ascontiguousarray(np.broadcast_to(a[:, :, None], (heads, chunk, n)))
    return d_intra, bcast(xi, dv), bcast(zeta, dk), chunk_decay


def _retention(proj, gn_g, batch, seq, heads, dk, dv):
    t = proj.shape[0]
    width = heads * dv
    chunk = RET_CHUNK
    sb = min(seq, 1024)
    assert seq % sb == 0 and sb % chunk == 0
    nsb = seq // sb
    d_intra, xi, zeta, chunk_decay = _retention_constants(heads, chunk, dk, dv)
    k_blk0 = heads
    v_blk0 = 2 * heads * dk // dv
    g_blk0 = v_blk0 + heads
    row = lambda b, h, s: b * nsb + s
    return pl.pallas_call(
        functools.partial(_retention_kernel, chunk=chunk),
        grid=(batch, heads, nsb),
        in_specs=[
            pl.BlockSpec(memory_space=pltpu.SMEM),
            pl.BlockSpec((sb, dk), lambda b, h, s: (row(b, h, s), h)),
            pl.BlockSpec((sb, dk), lambda b, h, s: (row(b, h, s), k_blk0 + h)),
            pl.BlockSpec((sb, dv), lambda b, h, s: (row(b, h, s), v_blk0 + h)),
            pl.BlockSpec((sb, dv), lambda b, h, s: (row(b, h, s), g_blk0 + h)),
            pl.BlockSpec((1, dv), lambda b, h, s: (0, h)),
            pl.BlockSpec((None, chunk, chunk), lambda b, h, s: (h, 0, 0)),
            pl.BlockSpec((None, chunk, dv), lambda b, h, s: (h, 0, 0)),
            pl.BlockSpec((None, chunk, dk), lambda b, h, s: (h, 0, 0)),
        ],
        out_specs=pl.BlockSpec((sb, dv), lambda b, h, s: (row(b, h, s), h)),
        out_shape=jax.ShapeDtypeStruct((t, width), BF16),
        scratch_shapes=[pltpu.VMEM((dk, dv), F32)],
        compiler_params=_params("parallel", "parallel", "arbitrary"),
        name="retention",
    )(chunk_decay, proj, proj, proj, proj, gn_g.reshape(1, width), d_intra, xi, zeta)


def _residual_chunks(a_ref, w_ref, x_ref, gate_ref, o_ref, rows, want_ssq):
    ssq = None
    for c0 in range(0, o_ref.shape[1], OUT_CHUNK):
        cols = slice(c0, c0 + OUT_CHUNK)
        y = jnp.dot(a_ref[rows, :], w_ref[:, cols], preferred_element_type=F32)
        z = x_ref[rows, cols] + gate_ref[:, cols] * y
        o_ref[rows, cols] = z
        if want_ssq:
            part = jnp.sum(z * z, axis=-1, keepdims=True)
            ssq = part if ssq is None else ssq + part
    return ssq


def _row_halves(ref):
    half = ref.shape[0] // 2
    return (slice(0, half), slice(half, 2 * half))


def _outproj_kernel(a_ref, w_ref, x_ref, gate_ref, o_ref):
    for rows in _row_halves(o_ref):
        _residual_chunks(a_ref, w_ref, x_ref, gate_ref, o_ref, rows, False)


def _outproj_final_kernel(a_ref, w_ref, x_ref, gate_ref, fg_ref, o_ref):
    for rows in _row_halves(o_ref):
        ssq = _residual_chunks(a_ref, w_ref, x_ref, gate_ref, o_ref, rows, True)
        inv = lax.rsqrt(ssq * (1.0 / o_ref.shape[1]) + EPS)
        o_ref[rows, :] = o_ref[rows, :] * inv * fg_ref[...]


def _outproj(a, w_bf16, x2d, mod, layer, seq, final_g=None):
    t, k = a.shape
    d = w_bf16.shape[1]
    tm = min(seq, 512)
    assert seq % tm == 0 and d % OUT_CHUNK == 0
    in_specs = [
        pl.BlockSpec((tm, k), lambda i: (i, 0)),
        _resident_spec(w_bf16.shape),
        pl.BlockSpec((tm, d), lambda i: (i, 0)),
        _mod_specs(layer, seq // tm, d)(2),
    ]
    args = [a, w_bf16, x2d, mod]
    kern = _outproj_kernel
    if final_g is not None:
        in_specs.append(pl.BlockSpec((1, d), lambda i: (0, 0)))
        args.append(final_g.reshape(1, d))
        kern = _outproj_final_kernel
    return pl.pallas_call(
        kern,
        grid=(t // tm,),
        in_specs=in_specs,
        out_specs=pl.BlockSpec((tm, d), lambda i: (i, 0)),
        out_shape=jax.ShapeDtypeStruct((t, d), F32),
        compiler_params=_params("parallel"),
        name="out_proj_final" if final_g is not None else "out_proj",
    )(*args)


def _mla_inproj_kernel(x_ref, g_ref, shift_ref, scale_ref, w_ref, o_ref, h_ref, *, gate_cols):
    for rows in _row_halves(o_ref):
        h_ref[rows, :] = _adaln_rmsnorm(x_ref, g_ref, shift_ref, scale_ref, rows)
        for c0 in range(0, o_ref.shape[1], MXU_TILE):
            acc = jnp.dot(h_ref[rows, :], w_ref[:, c0:c0 + MXU_TILE],
                          preferred_element_type=F32)
            if gate_cols[0] <= c0 < gate_cols[1]:
                acc = _silu(acc)
            o_ref[rows, c0:c0 + MXU_TILE] = acc.astype(BF16)


def _mla_inproj(x2d, norm_g, mod, layer, w_bf16, seq, gate_cols):
    t, d = x2d.shape
    n_out = w_bf16.shape[1]
    tm = min(seq, 512)
    assert seq % tm == 0 and n_out % MXU_TILE == 0
    assert gate_cols[0] % MXU_TILE == 0 and gate_cols[1] % MXU_TILE == 0
    mspec = _mod_specs(layer, seq // tm, d)
    return pl.pallas_call(
        functools.partial(_mla_inproj_kernel, gate_cols=gate_cols),
        grid=(t // tm,),
        in_specs=[
            pl.BlockSpec((tm, d), lambda i: (i, 0)),
            pl.BlockSpec((1, d), lambda i: (0, 0)),
            mspec(0), mspec(1),
            _resident_spec(w_bf16.shape),
        ],
        out_specs=pl.BlockSpec((tm, n_out), lambda i: (i, 0)),
        out_shape=jax.ShapeDtypeStruct((t, n_out), BF16),
        scratch_shapes=[pltpu.VMEM((tm, d), BF16)],
        compiler_params=_params("parallel"),
        name="mla_in_proj",
    )(x2d, norm_g.reshape(1, d), mod, mod, w_bf16)


def _mla_up_kernel(cq_ref, ckv_ref, kr_ref, qg_ref, kvg_ref, wqt_ref, wk_ref, wvt_ref, cost_ref,
                   sint_ref, qt_ref, k_ref, vt_ref, *, heads, q_scale, group):
    def rms(x_ref, g_ref):
        x = x_ref[...].astype(F32)
        return x * lax.rsqrt(jnp.mean(x * x, axis=-1, keepdims=True) + EPS) * g_ref[...]

    cqn_t = rms(cq_ref, qg_ref).T.astype(BF16)
    cos_t = cost_ref[...]
    sin_t = sint_ref[...]
    gw = group * MLA_HEAD_PAD
    for g0 in range(0, heads, group):
        qt = jnp.dot(wqt_ref[g0 * MLA_HEAD_PAD:g0 * MLA_HEAD_PAD + gw, :], cqn_t,
                     preferred_element_type=F32)
        for j in range(group):
            r0 = j * MLA_HEAD_PAD
            o0 = (g0 + j) * MLA_HEAD_PAD
            r1 = r0 + MLA_NOPE
            qt_ref[o0:o0 + MLA_NOPE, :] = (qt[r0:r1, :] * q_scale).astype(BF16)
            rope = (qt[r1:r1 + MLA_ROPE, :] * cos_t
                    + qt[r1 + MLA_ROPE:r1 + 2 * MLA_ROPE, :] * sin_t)
            qt_ref[o0 + MLA_NOPE:o0 + MLA_NOPE + MLA_ROPE, :] = (rope * q_scale).astype(BF16)
            qt_ref[o0 + MLA_NOPE + MLA_ROPE:o0 + MLA_HEAD_PAD, :] = jnp.zeros(
                (MLA_HEAD_PAD - MLA_NOPE - MLA_ROPE, qt.shape[1]), BF16)

    ckvn_f32 = rms(ckv_ref, kvg_ref)
    ckvn = ckvn_f32.astype(BF16)
    kr_t = kr_ref[...].astype(F32).T
    kr_rope_t = kr_t[:MLA_ROPE, :] * cos_t + kr_t[MLA_ROPE:, :] * sin_t
    kr = jnp.concatenate([kr_rope_t, jnp.zeros_like(kr_rope_t)], axis=0).T.astype(BF16)
    for g0 in range(0, heads, group):
        kn = jnp.dot(ckvn, wk_ref[:, g0 * MLA_NOPE:(g0 + group) * MLA_NOPE],
                     preferred_element_type=F32)
        for j in range(group):
            o0 = (g0 + j) * MLA_HEAD_PAD
            k_ref[:, o0:o0 + LANES] = kn[:, j * MLA_NOPE:(j + 1) * MLA_NOPE].astype(BF16)
            k_ref[:, o0 + LANES:o0 + 2 * LANES] = kr
    ckvn_t = ckvn_f32.T.astype(BF16)
    gv = group * MLA_V
    for r0 in range(0, heads * MLA_V, gv):
        vt_ref[r0:r0 + gv, :] = jnp.dot(wvt_ref[r0:r0 + gv, :], ckvn_t,
                                        preferred_element_type=F32).astype(BF16)


def _mla_up(proj, q_norm_g, kv_norm_g, wqt_bf16, wk_bf16, wvt_bf16, cos_t, sin_t, q_rank,
            kv_rank, kr_col, heads, tm):
    t = proj.shape[0]
    assert t % tm == 0 and q_rank == kv_rank and kr_col % LANES == 0
    return pl.pallas_call(
        functools.partial(_mla_up_kernel, heads=heads,
                          q_scale=LOG2_E * (MLA_NOPE + MLA_ROPE) ** -0.5, group=4),
        grid=(t // tm,),
        in_specs=[
            pl.BlockSpec((tm, q_rank), lambda i: (i, 0)),
            pl.BlockSpec((tm, kv_rank), lambda i: (i, 1)),
            pl.BlockSpec((tm, LANES), lambda i: (i, kr_col // LANES)),
            pl.BlockSpec((1, q_rank), lambda i: (0, 0)),
            pl.BlockSpec((1, kv_rank), lambda i: (0, 0)),
            _resident_spec(wqt_bf16.shape),
            _resident_spec(wk_bf16.shape),
            _resident_spec(wvt_bf16.shape),
            pl.BlockSpec((MLA_ROPE, tm), lambda i: (0, i)),
            pl.BlockSpec((MLA_ROPE, tm), lambda i: (0, i)),
        ],
        out_specs=[
            pl.BlockSpec((None, heads * MLA_HEAD_PAD, tm), lambda i: (i, 0, 0)),
            pl.BlockSpec((tm, heads * MLA_HEAD_PAD), lambda i: (i, 0)),
            pl.BlockSpec((None, heads * MLA_V, tm), lambda i: (i, 0, 0)),
        ],
        out_shape=[
            jax.ShapeDtypeStruct((t // tm, heads * MLA_HEAD_PAD, tm), BF16),
            jax.ShapeDtypeStruct((t, heads * MLA_HEAD_PAD), BF16),
            jax.ShapeDtypeStruct((t // tm, heads * MLA_V, tm), BF16),
        ],
        compiler_params=_params("parallel"),
        name="mla_up_proj",
    )(proj, proj, proj, q_norm_g.reshape(1, q_rank), kv_norm_g.reshape(1, kv_rank),
      wqt_bf16, wk_bf16, wvt_bf16, cos_t, sin_t)


def _attn_kernel(qt_ref, k_ref, vt_ref, sg_ref, o_ref, vt1_ref, s_ref, smax_ref, acc_ref, m_ref,
                 *, tk):
    nslab = ATTN_SLABS
    tq = nslab * tk
    nq = qt_ref.shape[0] // nslab
    all_q = slice(0, tq)

    def slabs(a, b):
        return slice(a * tk, b * tk)

    for j in range(vt_ref.shape[0]):
        vt1_ref[j, :MLA_V, :] = vt_ref[j]
        vt1_ref[j, MLA_V:, :] = jnp.ones((ONES_ROWS, tk), BF16)

    def put_scores(slot, t, qi, first_slab=0):
        kt = k_ref[pl.ds(pl.multiple_of(t * tk, tk), tk), :]
        for j in range(first_slab, nslab):
            st = jnp.dot(kt, qt_ref[nslab * qi + j], preferred_element_type=F32)
            s_ref[slot, :, slabs(j, j + 1)] = st
            smax_ref[slot, :, slabs(j, j + 1)] = jnp.max(st, axis=0, keepdims=True)

    def consume(t, slot, qs, diagonal):
        st = s_ref[slot, :, qs]
        if diagonal:
            key = lax.broadcasted_iota(jnp.int32, st.shape, 0)
            qry = lax.broadcasted_iota(jnp.int32, st.shape, 1)
            st = jnp.where(key <= qry, st, NEG_BIG)
            tile_max = jnp.max(st, axis=0, keepdims=True)
        else:
            tile_max = smax_ref[slot, :, qs]
        m_old = m_ref[:, qs]
        m_new = jnp.maximum(m_old, tile_max)
        p = jnp.exp2(st - m_new).astype(BF16)
        acc_ref[:, qs] = jnp.exp2(m_old - m_new) * acc_ref[:, qs] + jnp.dot(
            vt1_ref[t], p, preferred_element_type=F32)
        m_ref[:, qs] = m_new

    def finish(q0, qs):
        acc = acc_ref[:, qs]
        o = (acc[:MLA_V, :] * (1.0 / acc[MLA_V:MLA_V + 1, :])).T
        rows = pl.ds(pl.multiple_of(q0 + qs.start, tk), qs.stop - qs.start)
        o_ref[rows, :] = (sg_ref[rows, :].astype(F32) * o).astype(BF16)

    put_scores(0, 0, 0)

    def query_block(qi, carry):
        q0 = qi * tq
        m_ref[...] = jnp.full(m_ref.shape, NEG_BIG, F32)
        acc_ref[...] = jnp.zeros_like(acc_ref)

        def full_tiles(t0, n):
            for i in range(n):
                put_scores((i + 1) % 2, t0 + i + 1, qi)
                consume(t0 + i, i % 2, all_q, False)

        def loop_body(u, c):
            full_tiles(ATTN_LOOP_TILES * u, ATTN_LOOP_TILES)
            return c

        base = nslab * qi
        lax.fori_loop(0, base // ATTN_LOOP_TILES, loop_body, 0)
        if ATTN_LOOP_TILES == 2 * nslab:

            @pl.when(base % ATTN_LOOP_TILES != 0)
            def _():
                full_tiles(base - nslab, nslab)

        for j in range(nslab):
            last = j == nslab - 1
            if not last:
                put_scores((j + 1) % 2, base + j + 1, qi, first_slab=j + 1)
            consume(base + j, j % 2, slabs(j, j + 1), True)
            if last:
                put_scores(0, 0, jnp.minimum(qi + 1, nq - 1))
            else:
                consume(base + j, j % 2, slabs(j + 1, nslab), False)
            finish(q0, slabs(j, j + 1))
        return carry

    lax.fori_loop(0, nq, query_block, 0)


def _attention(qt, k, vt, proj, gate_col, batch, seq, heads, tk):
    t = k.shape[0]
    tq = ATTN_SLABS * tk
    assert seq % tq == 0 and gate_col % MLA_V == 0 and tk % LANES == 0 and ATTN_SLABS % 2 == 0
    assert ATTN_LOOP_TILES in (ATTN_SLABS, 2 * ATTN_SLABS)
    g_blk0 = gate_col // MLA_V
    return pl.pallas_call(
        functools.partial(_attn_kernel, tk=tk),
        grid=(batch, heads),
        in_specs=[
            pl.BlockSpec((seq // tk, MLA_HEAD_PAD, tk), lambda b, h: (b, h, 0)),
            pl.BlockSpec((seq, MLA_HEAD_PAD), lambda b, h: (b, h)),
            pl.BlockSpec((seq // tk, MLA_V, tk), lambda b, h: (b, h, 0)),
            pl.BlockSpec((seq, MLA_V), lambda b, h: (b, g_blk0 + h)),
        ],
        out_specs=pl.BlockSpec((seq, MLA_V), lambda b, h: (b, h)),
        out_shape=jax.ShapeDtypeStruct((t, heads * MLA_V), BF16),
        scratch_shapes=[
            pltpu.VMEM((seq // tk, MLA_V + ONES_ROWS, tk), BF16),
            pltpu.VMEM((2, tk, tq), F32),
            pltpu.VMEM((2, 1, tq), F32),
            pltpu.VMEM((MLA_V + ONES_ROWS, tq), F32),
            pltpu.VMEM((1, tq), F32),
        ],
        compiler_params=_params("parallel", "parallel"),
        name="mla_attention",
    )(qt, k, vt, proj)


def _rotate_half_cols(w):
    half = w.shape[-1] // 2
    return jnp.concatenate([-w[..., half:], w[..., :half]], axis=-1)


def _mla_weights(w_in, w_uq, w_ukv, q_rank, kv_rank, heads):
    d = w_in.shape[0]
    cq = w_in[:, :q_rank]
    ckv = w_in[:, q_rank:q_rank + kv_rank]
    kr = w_in[:, q_rank + kv_rank:q_rank + kv_rank + MLA_ROPE]
    gate = w_in[:, q_rank + kv_rank + MLA_ROPE:]
    gate_col = q_rank + kv_rank
    kr_col = gate_col + gate.shape[1]
    pad = jnp.zeros((d, -(kr_col + 2 * MLA_ROPE) % MXU_TILE), w_in.dtype)
    w_in_p = jnp.concatenate([cq, ckv, gate, kr, _rotate_half_cols(kr), pad],
                             axis=1).astype(BF16)
    uq = w_uq.reshape(q_rank, heads, MLA_NOPE + MLA_ROPE)
    rope = uq[..., MLA_NOPE:]
    uq_p = jnp.concatenate([uq[..., :MLA_NOPE], rope, _rotate_half_cols(rope)], axis=-1)
    uq_p = uq_p.reshape(q_rank, heads * MLA_HEAD_PAD).T.astype(BF16)
    ukv = w_ukv.reshape(kv_rank, heads, MLA_NOPE + MLA_V)
    uk_p = ukv[..., :MLA_NOPE].reshape(kv_rank, heads * MLA_NOPE).astype(BF16)
    uv_p = ukv[..., MLA_NOPE:].reshape(kv_rank, heads * MLA_V).T.astype(BF16)
    return w_in_p, uq_p, uk_p, uv_p, gate_col, kr_col


def kernel(x, c, positions, ada_w, ada_b, norm_g, ret_w_in, ret_gn_g, ret_w_out, mla_w_in,
           mla_q_norm_g, mla_w_uq, mla_kv_norm_g, mla_w_ukv, mla_w_out, final_norm_g):
    batch, seq, d = x.shape
    t = batch * seq
    depth = ada_w.shape[0]
    assert depth % 2 == 0, "the last layer must be a latent-attention layer (final norm fusion)"
    ret_dk = d // RET_HEADS
    ret_dv = ret_w_out.shape[1] // RET_HEADS
    q_rank = mla_q_norm_g.shape[1]
    kv_rank = mla_kv_norm_g.shape[1]

    mod = _modulation(c, ada_w, ada_b)

    pos = positions.reshape(t, 1).astype(F32)
    cos_r, sin_r = _rope_tables(jnp.broadcast_to(pos, (t, LANES)),
                                _inv_freq(ret_dk).reshape(1, LANES))
    cos_t, sin_t = _rope_tables_t(positions.reshape(1, t).astype(F32),
                                  _inv_freq(MLA_ROPE).reshape(MLA_ROPE // 2, 1))
    attn_tk = min(seq // ATTN_SLABS, 512)

    x2d = x.reshape(t, d)
    for layer in range(depth):
        j = layer // 2
        if layer % 2 == 0:
            proj = _ret_inproj(x2d, norm_g[layer], mod, layer, cos_r, sin_r,
                               ret_w_in[j].astype(BF16), seq, 2 * d,
                               2 * d + RET_HEADS * ret_dv, ret_dk)
            y = _retention(proj, ret_gn_g[j], batch, seq, RET_HEADS, ret_dk, ret_dv)
            x2d = _outproj(y, ret_w_out[j].astype(BF16), x2d, mod, layer, seq)
        else:
            w_in_p, uq_p, uk_p, uv_p, gate_col, kr_col = _mla_weights(
                mla_w_in[j], mla_w_uq[j], mla_w_ukv[j], q_rank, kv_rank, MLA_HEADS)
            proj = _mla_inproj(x2d, norm_g[layer], mod, layer, w_in_p, seq,
                               (gate_col, kr_col))
            qt, k, vt = _mla_up(proj, mla_q_norm_g[j], mla_kv_norm_g[j], uq_p, uk_p, uv_p,
                                cos_t, sin_t, q_rank, kv_rank, kr_col, MLA_HEADS, attn_tk)
            o = _attention(qt, k, vt, proj, gate_col, batch, seq, MLA_HEADS, attn_tk)
            final_g = final_norm_g if layer == depth - 1 else None
            x2d = _outproj(o, mla_w_out[j].astype(BF16), x2d, mod, layer, seq, final_g)
    return x2d.reshape(batch, seq, d)
```

```python
import functools

import jax
import jax.numpy as jnp
import numpy as np
from jax import lax
from jax.experimental import pallas as pl
from jax.experimental.pallas import tpu as pltpu

F32 = jnp.float32
BF16 = jnp.bfloat16

EPS = 1e-6
ROPE_BASE = 10000.0
NEG_BIG = -1e30
LOG2_E = 1.4426950408889634

LANES = 128
SUBLANES = 8
MXU_TILE = 256
OUT_CHUNK = 2 * MXU_TILE
VMEM_LIMIT_BYTES = 56 * 2**20

RET_HEADS = 8
RET_CHUNK = 256
MLA_HEADS = 16
MLA_NOPE = 128
MLA_ROPE = 64
MLA_V = 128
MLA_HEAD_PAD = 256
ONES_ROWS = 16
ATTN_SLABS = 2
ATTN_LOOP_TILES = 8


def _params(*semantics):
    return pltpu.CompilerParams(dimension_semantics=semantics,
                                vmem_limit_bytes=VMEM_LIMIT_BYTES)


def _silu(x):
    h = 0.5 * x
    return h + h * jnp.tanh(h)


def _mod_kernel(c_ref, w_ref, b_ref, o_ref):
    act = _silu(c_ref[...]).astype(BF16)
    o_ref[...] = jnp.dot(act, w_ref[...].astype(BF16),
                         preferred_element_type=F32) + b_ref[...]


def _modulation(c, ada_w, ada_b):
    depth, d, n3 = ada_w.shape
    b = c.shape[0]
    assert b <= SUBLANES
    tn = 768
    assert n3 % tn == 0
    c_pad = jnp.zeros((SUBLANES, d), F32).at[:b].set(c)
    out = pl.pallas_call(
        _mod_kernel,
        grid=(depth, n3 // tn),
        in_specs=[
            pl.BlockSpec((SUBLANES, d), lambda i, n: (0, 0)),
            pl.BlockSpec((None, d, tn), lambda i, n: (i, 0, n)),
            pl.BlockSpec((None, 1, tn), lambda i, n: (i, 0, n)),
        ],
        out_specs=pl.BlockSpec((None, SUBLANES, tn), lambda i, n: (i, 0, n)),
        out_shape=jax.ShapeDtypeStruct((depth, SUBLANES, n3), F32),
        compiler_params=_params("parallel", "parallel"),
        name="adaln_modulation",
    )(c_pad, ada_w, ada_b.reshape(depth, 1, n3))
    return out[:, :b].reshape(depth, b, 1, n3)


def _rope_table_kernel(pos_ref, freq_ref, cos_ref, sin_ref):
    ang = pos_ref[...] * freq_ref[...]
    cos_ref[...] = jnp.cos(ang)
    sin_ref[...] = jnp.sin(ang)


def _rope_tables(pos_lanes, freq_lanes):
    rows = pos_lanes.shape[0]
    tr = min(rows, 1024)
    assert rows % tr == 0
    spec = pl.BlockSpec((tr, LANES), lambda i: (i, 0))
    return pl.pallas_call(
        _rope_table_kernel,
        grid=(rows // tr,),
        in_specs=[spec, pl.BlockSpec((1, LANES), lambda i: (0, 0))],
        out_specs=[spec, spec],
        out_shape=[jax.ShapeDtypeStruct((rows, LANES), F32)] * 2,
        compiler_params=_params("parallel"),
        name="rope_tables",
    )(pos_lanes, freq_lanes)


def _rope_table_t_kernel(pos_ref, freq_ref, cos_ref, sin_ref):
    ang = freq_ref[...] * pos_ref[...]
    n = ang.shape[0]
    cos = jnp.cos(ang)
    sin = jnp.sin(ang)
    cos_ref[:n, :] = cos
    cos_ref[n:, :] = cos
    sin_ref[:n, :] = sin
    sin_ref[n:, :] = sin


def _rope_tables_t(pos_row, freq_col):
    t = pos_row.shape[1]
    n = freq_col.shape[0]
    tc = min(t, 4096)
    assert t % tc == 0
    out_spec = pl.BlockSpec((2 * n, tc), lambda i: (0, i))
    return pl.pallas_call(
        _rope_table_t_kernel,
        grid=(t // tc,),
        in_specs=[pl.BlockSpec((1, tc), lambda i: (0, i)),
                  pl.BlockSpec((n, 1), lambda i: (0, 0))],
        out_specs=[out_spec, out_spec],
        out_shape=[jax.ShapeDtypeStruct((2 * n, t), F32)] * 2,
        compiler_params=_params("parallel"),
        name="rope_tables_t",
    )(pos_row, freq_col)


def _inv_freq(d):
    return ROPE_BASE ** (-jnp.arange(0, d, 2, dtype=F32) / d)


def _adaln_rmsnorm(x_ref, g_ref, shift_ref, scale_ref, rows=slice(None)):
    x = x_ref[rows, :]
    y = x * lax.rsqrt(jnp.mean(x * x, axis=-1, keepdims=True) + EPS)
    return (y * (g_ref[...] * (1.0 + scale_ref[...])) + shift_ref[...]).astype(BF16)


def _mod_specs(layer, rows_per_batch_block, d):
    def spec(part):
        return pl.BlockSpec((None, None, 1, d),
                            lambda i, *_: (layer, i // rows_per_batch_block, 0, part))
    return spec


def _resident_spec(shape):
    return pl.BlockSpec(shape, lambda *_: (0,) * len(shape), pipeline_mode=pl.Buffered(1))


def _ret_inproj_kernel(x_ref, g_ref, shift_ref, scale_ref, cos_ref, sin_ref, w_ref,
                       o_ref, h_ref, *, rope_blocks, gate_block0, head_dim):
    n = pl.program_id(1)
    half = head_dim // 2
    tm = o_ref.shape[0]
    chunks = range(0, o_ref.shape[1], head_dim)
    halves = range(0, tm, tm // 2)
    pieces = [(c0, r0) for c0 in chunks for r0 in halves]

    def project(c0, r0):
        return jnp.dot(h_ref[r0:r0 + tm // 2, :], w_ref[:, c0:c0 + head_dim],
                       preferred_element_type=F32)

    def rope_piece(c0, r0):
        rows = slice(r0, r0 + tm // 2)
        acc = project(c0, r0)
        x1 = acc[:, :half]
        x2 = acc[:, half:]
        cos = cos_ref[rows, :]
        sin = sin_ref[rows, :]
        o_ref[rows, c0:c0 + half] = (x1 * cos - x2 * sin).astype(BF16)
        o_ref[rows, c0 + half:c0 + head_dim] = (x2 * cos + x1 * sin).astype(BF16)

    @pl.when(n == 0)
    def _():
        for r0 in halves:
            rows = slice(r0, r0 + tm // 2)
            h_ref[rows, :] = _adaln_rmsnorm(x_ref, g_ref, shift_ref, scale_ref, rows)
            for c0 in chunks:
                rope_piece(c0, r0)

    @pl.when((n > 0) & (n < rope_blocks))
    def _():
        for c0, r0 in pieces:
            rope_piece(c0, r0)

    @pl.when((n >= rope_blocks) & (n < gate_block0))
    def _():
        for c0, r0 in pieces:
            o_ref[r0:r0 + tm // 2, c0:c0 + head_dim] = project(c0, r0).astype(BF16)

    @pl.when(n >= gate_block0)
    def _():
        for c0, r0 in pieces:
            o_ref[r0:r0 + tm // 2, c0:c0 + head_dim] = _silu(project(c0, r0)).astype(BF16)


def _ret_inproj(x2d, norm_g, mod, layer, cos, sin, w_bf16, seq, rope_cols, gate_col, head_dim):
    t, d = x2d.shape
    n_out = w_bf16.shape[1]
    tm = min(seq, 1024)
    tn = 2048
    assert seq % tm == 0 and n_out % tn == 0 and rope_cols % tn == 0 and tn % head_dim == 0
    assert head_dim // 2 == LANES and gate_col % tn == 0
    mspec = _mod_specs(layer, seq // tm, d)
    return pl.pallas_call(
        functools.partial(_ret_inproj_kernel, rope_blocks=rope_cols // tn,
                          gate_block0=gate_col // tn, head_dim=head_dim),
        grid=(t // tm, n_out // tn),
        in_specs=[
            pl.BlockSpec((tm, d), lambda i, n: (i, 0)),
            pl.BlockSpec((1, d), lambda i, n: (0, 0)),
            mspec(0), mspec(1),
            pl.BlockSpec((tm, LANES), lambda i, n: (i, 0)),
            pl.BlockSpec((tm, LANES), lambda i, n: (i, 0)),
            pl.BlockSpec((d, tn), lambda i, n: (0, n)),
        ],
        out_specs=pl.BlockSpec((tm, tn), lambda i, n: (i, n)),
        out_shape=jax.ShapeDtypeStruct((t, n_out), BF16),
        scratch_shapes=[pltpu.VMEM((tm, d), BF16)],
        compiler_params=_params("parallel", "arbitrary"),
        name="ret_in_proj",
    )(x2d, norm_g.reshape(1, d), mod, mod, cos, sin, w_bf16)


def _retention_kernel(decay_ref, q_ref, k_ref, v_ref, sg_ref, gn_ref, dmat_ref, xi_ref,
                      zeta_ref, o_ref, state_ref, *, chunk):
    @pl.when(pl.program_id(2) == 0)
    def _():
        state_ref[...] = jnp.zeros_like(state_ref)

    decay = decay_ref[pl.program_id(1)]
    for c in range(q_ref.shape[0] // chunk):
        rows = slice(c * chunk, (c + 1) * chunk)
        q = q_ref[rows, :]
        k = k_ref[rows, :]
        v = v_ref[rows, :]
        s = lax.dot_general(q, k, (((1,), (1,)), ((), ())), preferred_element_type=F32)
        kz = (k.astype(F32) * zeta_ref[...]).astype(BF16)
        update = lax.dot_general(kz, v, (((0,), (0,)), ((), ())), preferred_element_type=F32)
        s = (s * dmat_ref[...]).astype(BF16)
        state = state_ref[...]
        y = (jnp.dot(s, v, preferred_element_type=F32)
             + jnp.dot(q, state.astype(BF16), preferred_element_type=F32) * xi_ref[...])
        state_ref[...] = state * decay + update
        yc = y - jnp.mean(y, axis=-1, keepdims=True)
        var = jnp.mean(yc * yc, axis=-1, keepdims=True)
        yn = yc * lax.rsqrt(var + EPS) * gn_ref[...]
        o_ref[rows, :] = sg_ref[rows, :] * yn.astype(BF16)


def _retention_constants(heads, chunk, dk, dv):
    f32 = np.float32
    log_gamma = np.log1p(-np.exp2(f32(-5.0) - np.arange(heads, dtype=f32))).astype(f32)
    idx = np.arange(chunk, dtype=f32)
    diff = idx[:, None] - idx[None, :]
    k_scale = f32(dk ** -0.5)
    d_intra = np.where(diff >= 0, np.exp(log_gamma[:, None, None] * np.maximum(diff, 0)),
                       0).astype(f32) * k_scale
    xi = np.exp(log_gamma[:, None] * (idx + 1)).astype(f32)
    zeta = np.exp(log_gamma[:, None] * (chunk - 1 - idx)).astype(f32) * k_scale
    chunk_decay = np.exp(log_gamma * chunk).astype(f32)
    bcast = lambda a, n: np.ascontiguousarray(np.broadcast_to(a[:, :, None], (heads, chunk, n)))
    return d_intra, bcast(xi, dv), bcast(zeta, dk), chunk_decay


def _retention(proj, gn_g, batch, seq, heads, dk, dv):
    t = proj.shape[0]
    width = heads * dv
    chunk = RET_CHUNK
    sb = min(seq, 1024)
    assert seq % sb == 0 and sb % chunk == 0
    nsb = seq // sb
    d_intra, xi, zeta, chunk_decay = _retention_constants(heads, chunk, dk, dv)
    k_blk0 = heads
    v_blk0 = 2 * heads * dk // dv
    g_blk0 = v_blk0 + heads
    row = lambda b, h, s: b * nsb + s
    return pl.pallas_call(
        functools.partial(_retention_kernel, chunk=chunk),
        grid=(batch, heads, nsb),
        in_specs=[
            pl.BlockSpec(memory_space=pltpu.SMEM),
            pl.BlockSpec((sb, dk), lambda b, h, s: (row(b, h, s), h)),
            pl.BlockSpec((sb, dk), lambda b, h, s: (row(b, h, s), k_blk0 + h)),
            pl.BlockSpec((sb, dv), lambda b, h, s: (row(b, h, s), v_blk0 + h)),
            pl.BlockSpec((sb, dv), lambda b, h, s: (row(b, h, s), g_blk0 + h)),
            pl.BlockSpec((1, dv), lambda b, h, s: (0, h)),
            pl.BlockSpec((None, chunk, chunk), lambda b, h, s: (h, 0, 0)),
            pl.BlockSpec((None, chunk, dv), lambda b, h, s: (h, 0, 0)),
            pl.BlockSpec((None, chunk, dk), lambda b, h, s: (h, 0, 0)),
        ],
        out_specs=pl.BlockSpec((sb, dv), lambda b, h, s: (row(b, h, s), h)),
        out_shape=jax.ShapeDtypeStruct((t, width), BF16),
        scratch_shapes=[pltpu.VMEM((dk, dv), F32)],
        compiler_params=_params("parallel", "parallel", "arbitrary"),
        name="retention",
    )(chunk_decay, proj, proj, proj, proj, gn_g.reshape(1, width), d_intra, xi, zeta)


def _residual_chunks(a_ref, w_ref, x_ref, gate_ref, o_ref, rows, want_ssq):
    ssq = None
    for c0 in range(0, o_ref.shape[1], OUT_CHUNK):
        cols = slice(c0, c0 + OUT_CHUNK)
        y = jnp.dot(a_ref[rows, :], w_ref[:, cols], preferred_element_type=F32)
        z = x_ref[rows, cols] + gate_ref[:, cols] * y
        o_ref[rows, cols] = z
        if want_ssq:
            part = jnp.sum(z * z, axis=-1, keepdims=True)
            ssq = part if ssq is None else ssq + part
    return ssq


def _row_halves(ref):
    half = ref.shape[0] // 2
    return (slice(0, half), slice(half, 2 * half))


def _outproj_kernel(a_ref, w_ref, x_ref, gate_ref, o_ref):
    _residual_chunks(a_ref, w_ref, x_ref, gate_ref, o_ref, slice(None), False)


def _outproj_final_kernel(a_ref, w_ref, x_ref, gate_ref, fg_ref, o_ref):
    ssq = _residual_chunks(a_ref, w_ref, x_ref, gate_ref, o_ref, slice(None), True)
    inv = lax.rsqrt(ssq * (1.0 / o_ref.shape[1]) + EPS)
    o_ref[...] = o_ref[...] * inv * fg_ref[...]


def _outproj(a, w_bf16, x2d, mod, layer, seq, final_g=None):
    t, k = a.shape
    d = w_bf16.shape[1]
    tm = min(seq, 512)
    assert seq % tm == 0 and d % OUT_CHUNK == 0
    in_specs = [
        pl.BlockSpec((tm, k), lambda i: (i, 0)),
        _resident_spec(w_bf16.shape),
        pl.BlockSpec((tm, d), lambda i: (i, 0)),
        _mod_specs(layer, seq // tm, d)(2),
    ]
    args = [a, w_bf16, x2d, mod]
    kern = _outproj_kernel
    if final_g is not None:
        in_specs.append(pl.BlockSpec((1, d), lambda i: (0, 0)))
        args.append(final_g.reshape(1, d))
        kern = _outproj_final_kernel
    return pl.pallas_call(
        kern,
        grid=(t // tm,),
        in_specs=in_specs,
        out_specs=pl.BlockSpec((tm, d), lambda i: (i, 0)),
        out_shape=jax.ShapeDtypeStruct((t, d), F32),
        compiler_params=_params("parallel"),
        name="out_proj_final" if final_g is not None else "out_proj",
    )(*args)


def _mla_inproj_kernel(x_ref, g_ref, shift_ref, scale_ref, w_ref, o_ref, h_ref, *, gate_cols):
    for rows in _row_halves(o_ref):
        h_ref[rows, :] = _adaln_rmsnorm(x_ref, g_ref, shift_ref, scale_ref, rows)
        for c0 in range(0, o_ref.shape[1], MXU_TILE):
            acc = jnp.dot(h_ref[rows, :], w_ref[:, c0:c0 + MXU_TILE],
                          preferred_element_type=F32)
            if gate_cols[0] <= c0 < gate_cols[1]:
                acc = _silu(acc)
            o_ref[rows, c0:c0 + MXU_TILE] = acc.astype(BF16)


def _mla_inproj(x2d, norm_g, mod, layer, w_bf16, seq, gate_cols):
    t, d = x2d.shape
    n_out = w_bf16.shape[1]
    tm = min(seq, 512)
    assert seq % tm == 0 and n_out % MXU_TILE == 0
    assert gate_cols[0] % MXU_TILE == 0 and gate_cols[1] % MXU_TILE == 0
    mspec = _mod_specs(layer, seq // tm, d)
    return pl.pallas_call(
        functools.partial(_mla_inproj_kernel, gate_cols=gate_cols),
        grid=(t // tm,),
        in_specs=[
            pl.BlockSpec((tm, d), lambda i: (i, 0)),
            pl.BlockSpec((1, d), lambda i: (0, 0)),
            mspec(0), mspec(1),
            _resident_spec(w_bf16.shape),
        ],
        out_specs=pl.BlockSpec((tm, n_out), lambda i: (i, 0)),
        out_shape=jax.ShapeDtypeStruct((t, n_out), BF16),
        scratch_shapes=[pltpu.VMEM((tm, d), BF16)],
        compiler_params=_params("parallel"),
        name="mla_in_proj",
    )(x2d, norm_g.reshape(1, d), mod, mod, w_bf16)


def _mla_up_kernel(cq_ref, ckv_ref, kr_ref, qg_ref, kvg_ref, wqt_ref, wk_ref, wvt_ref, cost_ref,
                   sint_ref, qt_ref, k_ref, vt_ref, *, heads, q_scale, group):
    def rms(x_ref, g_ref):
        x = x_ref[...].astype(F32)
        return x * lax.rsqrt(jnp.mean(x * x, axis=-1, keepdims=True) + EPS) * g_ref[...]

    cqn_t = rms(cq_ref, qg_ref).T.astype(BF16)
    cos_t = cost_ref[...]
    sin_t = sint_ref[...]
    gw = group * MLA_HEAD_PAD
    for g0 in range(0, heads, group):
        qt = jnp.dot(wqt_ref[g0 * MLA_HEAD_PAD:g0 * MLA_HEAD_PAD + gw, :], cqn_t,
                     preferred_element_type=F32)
        for j in range(group):
            r0 = j * MLA_HEAD_PAD
            o0 = (g0 + j) * MLA_HEAD_PAD
            r1 = r0 + MLA_NOPE
            qt_ref[o0:o0 + MLA_NOPE, :] = (qt[r0:r1, :] * q_scale).astype(BF16)
            rope = (qt[r1:r1 + MLA_ROPE, :] * cos_t
                    + qt[r1 + MLA_ROPE:r1 + 2 * MLA_ROPE, :] * sin_t)
            qt_ref[o0 + MLA_NOPE:o0 + MLA_NOPE + MLA_ROPE, :] = (rope * q_scale).astype(BF16)
            qt_ref[o0 + MLA_NOPE + MLA_ROPE:o0 + MLA_HEAD_PAD, :] = jnp.zeros(
                (MLA_HEAD_PAD - MLA_NOPE - MLA_ROPE, qt.shape[1]), BF16)

    ckvn_f32 = rms(ckv_ref, kvg_ref)
    ckvn = ckvn_f32.astype(BF16)
    kr_t = kr_ref[...].astype(F32).T
    kr_rope_t = kr_t[:MLA_ROPE, :] * cos_t + kr_t[MLA_ROPE:, :] * sin_t
    kr = jnp.concatenate([kr_rope_t, jnp.zeros_like(kr_rope_t)], axis=0).T.astype(BF16)
    for g0 in range(0, heads, group):
        kn = jnp.dot(ckvn, wk_ref[:, g0 * MLA_NOPE:(g0 + group) * MLA_NOPE],
                     preferred_element_type=F32)
        for j in range(group):
            o0 = (g0 + j) * MLA_HEAD_PAD
            k_ref[:, o0:o0 + LANES] = kn[:, j * MLA_NOPE:(j + 1) * MLA_NOPE].astype(BF16)
            k_ref[:, o0 + LANES:o0 + 2 * LANES] = kr
    ckvn_t = ckvn_f32.T.astype(BF16)
    gv = group * MLA_V
    for r0 in range(0, heads * MLA_V, gv):
        vt_ref[r0:r0 + gv, :] = jnp.dot(wvt_ref[r0:r0 + gv, :], ckvn_t,
                                        preferred_element_type=F32).astype(BF16)


def _mla_up(proj, q_norm_g, kv_norm_g, wqt_bf16, wk_bf16, wvt_bf16, cos_t, sin_t, q_rank,
            kv_rank, kr_col, heads, tm):
    t = proj.shape[0]
    assert t % tm == 0 and q_rank == kv_rank and kr_col % LANES == 0
    return pl.pallas_call(
        functools.partial(_mla_up_kernel, heads=heads,
                          q_scale=LOG2_E * (MLA_NOPE + MLA_ROPE) ** -0.5, group=4),
        grid=(t // tm,),
        in_specs=[
            pl.BlockSpec((tm, q_rank), lambda i: (i, 0)),
            pl.BlockSpec((tm, kv_rank), lambda i: (i, 1)),
            pl.BlockSpec((tm, LANES), lambda i: (i, kr_col // LANES)),
            pl.BlockSpec((1, q_rank), lambda i: (0, 0)),
            pl.BlockSpec((1, kv_rank), lambda i: (0, 0)),
            _resident_spec(wqt_bf16.shape),
            _resident_spec(wk_bf16.shape),
            _resident_spec(wvt_bf16.shape),
            pl.BlockSpec((MLA_ROPE, tm), lambda i: (0, i)),
            pl.BlockSpec((MLA_ROPE, tm), lambda i: (0, i)),
        ],
        out_specs=[
            pl.BlockSpec((None, heads * MLA_HEAD_PAD, tm), lambda i: (i, 0, 0)),
            pl.BlockSpec((tm, heads * MLA_HEAD_PAD), lambda i: (i, 0)),
            pl.BlockSpec((None, heads * MLA_V, tm), lambda i: (i, 0, 0)),
        ],
        out_shape=[
            jax.ShapeDtypeStruct((t // tm, heads * MLA_HEAD_PAD, tm), BF16),
            jax.ShapeDtypeStruct((t, heads * MLA_HEAD_PAD), BF16),
            jax.ShapeDtypeStruct((t // tm, heads * MLA_V, tm), BF16),
        ],
        compiler_params=_params("parallel"),
        name="mla_up_proj",
    )(proj, proj, proj, q_norm_g.reshape(1, q_rank), kv_norm_g.reshape(1, kv_rank),
      wqt_bf16, wk_bf16, wvt_bf16, cos_t, sin_t)


def _attn_kernel(qt_ref, k_ref, vt_ref, sg_ref, o_ref, vt1_ref, s_ref, smax_ref, acc_ref, m_ref,
                 *, tk):
    nslab = ATTN_SLABS
    tq = nslab * tk
    nq = qt_ref.shape[0] // nslab
    all_q = slice(0, tq)

    def slabs(a, b):
        return slice(a * tk, b * tk)

    for j in range(vt_ref.shape[0]):
        vt1_ref[j, :MLA_V, :] = vt_ref[j]
        vt1_ref[j, MLA_V:, :] = jnp.ones((ONES_ROWS, tk), BF16)

    def put_scores(slot, t, qi, first_slab=0):
        kt = k_ref[pl.ds(pl.multiple_of(t * tk, tk), tk), :]
        for j in range(first_slab, nslab):
            st = jnp.dot(kt, qt_ref[nslab * qi + j], preferred_element_type=F32)
            s_ref[slot, :, slabs(j, j + 1)] = st
            smax_ref[slot, :, slabs(j, j + 1)] = jnp.max(st, axis=0, keepdims=True)

    def consume(t, slot, qs, diagonal):
        st = s_ref[slot, :, qs]
        if diagonal:
            key = lax.broadcasted_iota(jnp.int32, st.shape, 0)
            qry = lax.broadcasted_iota(jnp.int32, st.shape, 1)
            st = jnp.where(key <= qry, st, NEG_BIG)
            tile_max = jnp.max(st, axis=0, keepdims=True)
        else:
            tile_max = smax_ref[slot, :, qs]
        m_old = m_ref[:, qs]
        m_new = jnp.maximum(m_old, tile_max)
        p = jnp.exp2(st - m_new).astype(BF16)
        acc_ref[:, qs] = jnp.exp2(m_old - m_new) * acc_ref[:, qs] + jnp.dot(
            vt1_ref[t], p, preferred_element_type=F32)
        m_ref[:, qs] = m_new

    def finish(q0, qs):
        acc = acc_ref[:, qs]
        o = (acc[:MLA_V, :] * (1.0 / acc[MLA_V:MLA_V + 1, :])).T
        rows = pl.ds(pl.multiple_of(q0 + qs.start, tk), qs.stop - qs.start)
        o_ref[rows, :] = (sg_ref[rows, :].astype(F32) * o).astype(BF16)

    put_scores(0, 0, 0)

    def query_block(qi, carry):
        q0 = qi * tq
        m_ref[...] = jnp.full(m_ref.shape, NEG_BIG, F32)
        acc_ref[...] = jnp.zeros_like(acc_ref)

        def full_tiles(t0, n):
            for i in range(n):
                put_scores((i + 1) % 2, t0 + i + 1, qi)
                consume(t0 + i, i % 2, all_q, False)

        def loop_body(u, c):
            full_tiles(ATTN_LOOP_TILES * u, ATTN_LOOP_TILES)
            return c

        base = nslab * qi
        lax.fori_loop(0, base // ATTN_LOOP_TILES, loop_body, 0)
        size = ATTN_LOOP_TILES // 2
        while size >= nslab:
            rem = base % (2 * size)

            @pl.when(rem >= size)
            def _(rem=rem, size=size):
                full_tiles(base - rem, size)

            size //= 2

        for j in range(nslab):
            last = j == nslab - 1
            if not last:
                put_scores((j + 1) % 2, base + j + 1, qi, first_slab=j + 1)
            consume(base + j, j % 2, slabs(j, j + 1), True)
            if last:
                put_scores(0, 0, jnp.minimum(qi + 1, nq - 1))
            else:
                consume(base + j, j % 2, slabs(j + 1, nslab), False)
            finish(q0, slabs(j, j + 1))
        return carry

    lax.fori_loop(0, nq, query_block, 0)


def _attention(qt, k, vt, proj, gate_col, batch, seq, heads, tk):
    t = k.shape[0]
    tq = ATTN_SLABS * tk
    assert seq % tq == 0 and gate_col % MLA_V == 0 and tk % LANES == 0 and ATTN_SLABS % 2 == 0
    assert ATTN_LOOP_TILES in (ATTN_SLABS, 2 * ATTN_SLABS, 4 * ATTN_SLABS)
    g_blk0 = gate_col // MLA_V
    return pl.pallas_call(
        functools.partial(_attn_kernel, tk=tk),
        grid=(batch, heads),
        in_specs=[
            pl.BlockSpec((seq // tk, MLA_HEAD_PAD, tk), lambda b, h: (b, h, 0)),
            pl.BlockSpec((seq, MLA_HEAD_PAD), lambda b, h: (b, h)),
            pl.BlockSpec((seq // tk, MLA_V, tk), lambda b, h: (b, h, 0)),
            pl.BlockSpec((seq, MLA_V), lambda b, h: (b, g_blk0 + h)),
        ],
        out_specs=pl.BlockSpec((seq, MLA_V), lambda b, h: (b, h)),
        out_shape=jax.ShapeDtypeStruct((t, heads * MLA_V), BF16),
        scratch_shapes=[
            pltpu.VMEM((seq // tk, MLA_V + ONES_ROWS, tk), BF16),
            pltpu.VMEM((2, tk, tq), F32),
            pltpu.VMEM((2, 1, tq), F32),
            pltpu.VMEM((MLA_V + ONES_ROWS, tq), F32),
            pltpu.VMEM((1, tq), F32),
        ],
        compiler_params=_params("parallel", "parallel"),
        name="mla_attention",
    )(qt, k, vt, proj)


def _rotate_half_cols(w):
    half = w.shape[-1] // 2
    return jnp.concatenate([-w[..., half:], w[..., :half]], axis=-1)


def _mla_weights(w_in, w_uq, w_ukv, q_rank, kv_rank, heads):
    d = w_in.shape[0]
    cq = w_in[:, :q_rank]
    ckv = w_in[:, q_rank:q_rank + kv_rank]
    kr = w_in[:, q_rank + kv_rank:q_rank + kv_rank + MLA_ROPE]
    gate = w_in[:, q_rank + kv_rank + MLA_ROPE:]
    gate_col = q_rank + kv_rank
    kr_col = gate_col + gate.shape[1]
    pad = jnp.zeros((d, -(kr_col + 2 * MLA_ROPE) % MXU_TILE), w_in.dtype)
    w_in_p = jnp.concatenate([cq, ckv, gate, kr, _rotate_half_cols(kr), pad],
                             axis=1).astype(BF16)
    uq = w_uq.reshape(q_rank, heads, MLA_NOPE + MLA_ROPE)
    rope = uq[..., MLA_NOPE:]
    uq_p = jnp.concatenate([uq[..., :MLA_NOPE], rope, _rotate_half_cols(rope)], axis=-1)
    uq_p = uq_p.reshape(q_rank, heads * MLA_HEAD_PAD).T.astype(BF16)
    ukv = w_ukv.reshape(kv_rank, heads, MLA_NOPE + MLA_V)
    uk_p = ukv[..., :MLA_NOPE].reshape(kv_rank, heads * MLA_NOPE).astype(BF16)
    uv_p = ukv[..., MLA_NOPE:].reshape(kv_rank, heads * MLA_V).T.astype(BF16)
    return w_in_p, uq_p, uk_p, uv_p, gate_col, kr_col


def kernel(x, c, positions, ada_w, ada_b, norm_g, ret_w_in, ret_gn_g, ret_w_out, mla_w_in,
           mla_q_norm_g, mla_w_uq, mla_kv_norm_g, mla_w_ukv, mla_w_out, final_norm_g):
    batch, seq, d = x.shape
    t = batch * seq
    depth = ada_w.shape[0]
    assert depth % 2 == 0, "the last layer must be a latent-attention layer (final norm fusion)"
    ret_dk = d // RET_HEADS
    ret_dv = ret_w_out.shape[1] // RET_HEADS
    q_rank = mla_q_norm_g.shape[1]
    kv_rank = mla_kv_norm_g.shape[1]

    mod = _modulation(c, ada_w, ada_b)

    pos = positions.reshape(t, 1).astype(F32)
    cos_r, sin_r = _rope_tables(jnp.broadcast_to(pos, (t, LANES)),
                                _inv_freq(ret_dk).reshape(1, LANES))
    cos_t, sin_t = _rope_tables_t(positions.reshape(1, t).astype(F32),
                                  _inv_freq(MLA_ROPE).reshape(MLA_ROPE // 2, 1))
    attn_tk = min(seq // ATTN_SLABS, 512)

    x2d = x.reshape(t, d)
    for layer in range(depth):
        j = layer // 2
        if layer % 2 == 0:
            proj = _ret_inproj(x2d, norm_g[layer], mod, layer, cos_r, sin_r,
                               ret_w_in[j].astype(BF16), seq, 2 * d,
                               2 * d + RET_HEADS * ret_dv, ret_dk)
            y = _retention(proj, ret_gn_g[j], batch, seq, RET_HEADS, ret_dk, ret_dv)
            x2d = _outproj(y, ret_w_out[j].astype(BF16), x2d, mod, layer, seq)
        else:
            w_in_p, uq_p, uk_p, uv_p, gate_col, kr_col = _mla_weights(
                mla_w_in[j], mla_w_uq[j], mla_w_ukv[j], q_rank, kv_rank, MLA_HEADS)
            proj = _mla_inproj(x2d, norm_g[layer], mod, layer, w_in_p, seq,
                               (gate_col, kr_col))
            qt, k, vt = _mla_up(proj, mla_q_norm_g[j], mla_kv_norm_g[j], uq_p, uk_p, uv_p,
                                cos_t, sin_t, q_rank, kv_rank, kr_col, MLA_HEADS, attn_tk)
            o = _attention(qt, k, vt, proj, gate_col, batch, seq, MLA_HEADS, attn_tk)
            final_g = final_norm_g if layer == depth - 1 else None
            x2d = _outproj(o, mla_w_out[j].astype(BF16), x2d, mod, layer, seq, final_g)
    return x2d.reshape(batch, seq, d)
```

```python
import functools

import jax
import jax.numpy as jnp
import numpy as np
from jax import lax
from jax.experimental import pallas as pl
from jax.experimental.pallas import tpu as pltpu

F32 = jnp.float32
BF16 = jnp.bfloat16

EPS = 1e-6
ROPE_BASE = 10000.0
NEG_BIG = -1e30
LOG2_E = 1.4426950408889634

LANES = 128
SUBLANES = 8
MXU_TILE = 256
OUT_CHUNK = 2 * MXU_TILE
VMEM_LIMIT_BYTES = 56 * 2**20

RET_HEADS = 8
RET_CHUNK = 256
MLA_HEADS = 16
MLA_NOPE = 128
MLA_ROPE = 64
MLA_V = 128
MLA_HEAD_PAD = 256
ONES_ROWS = 16
ATTN_SLABS = 2
ATTN_LOOP_TILES = 8


def _params(*semantics):
    return pltpu.CompilerParams(dimension_semantics=semantics,
                                vmem_limit_bytes=VMEM_LIMIT_BYTES)


def _silu(x):
    h = 0.5 * x
    return h + h * jnp.tanh(h)


def _mod_kernel(c_ref, w_ref, b_ref, o_ref):
    act = _silu(c_ref[...]).astype(BF16)
    o_ref[...] = jnp.dot(act, w_ref[...].astype(BF16),
                         preferred_element_type=F32) + b_ref[...]


def _modulation(c, ada_w, ada_b):
    depth, d, n3 = ada_w.shape
    b = c.shape[0]
    assert b <= SUBLANES
    tn = 768
    assert n3 % tn == 0
    c_pad = jnp.zeros((SUBLANES, d), F32).at[:b].set(c)
    out = pl.pallas_call(
        _mod_kernel,
        grid=(depth, n3 // tn),
        in_specs=[
            pl.BlockSpec((SUBLANES, d), lambda i, n: (0, 0)),
            pl.BlockSpec((None, d, tn), lambda i, n: (i, 0, n)),
            pl.BlockSpec((None, 1, tn), lambda i, n: (i, 0, n)),
        ],
        out_specs=pl.BlockSpec((None, SUBLANES, tn), lambda i, n: (i, 0, n)),
        out_shape=jax.ShapeDtypeStruct((depth, SUBLANES, n3), F32),
        compiler_params=_params("parallel", "parallel"),
        name="adaln_modulation",
    )(c_pad, ada_w, ada_b.reshape(depth, 1, n3))
    return out[:, :b].reshape(depth, b, 1, n3)


def _rope_table_kernel(pos_ref, freq_ref, cos_ref, sin_ref):
    ang = pos_ref[...] * freq_ref[...]
    cos_ref[...] = jnp.cos(ang)
    sin_ref[...] = jnp.sin(ang)


def _rope_tables(pos_lanes, freq_lanes):
    rows = pos_lanes.shape[0]
    tr = min(rows, 1024)
    assert rows % tr == 0
    spec = pl.BlockSpec((tr, LANES), lambda i: (i, 0))
    return pl.pallas_call(
        _rope_table_kernel,
        grid=(rows // tr,),
        in_specs=[spec, pl.BlockSpec((1, LANES), lambda i: (0, 0))],
        out_specs=[spec, spec],
        out_shape=[jax.ShapeDtypeStruct((rows, LANES), F32)] * 2,
        compiler_params=_params("parallel"),
        name="rope_tables",
    )(pos_lanes, freq_lanes)


def _rope_table_t_kernel(pos_ref, freq_ref, cos_ref, sin_ref):
    ang = freq_ref[...] * pos_ref[...]
    n = ang.shape[0]
    cos = jnp.cos(ang)
    sin = jnp.sin(ang)
    cos_ref[:n, :] = cos
    cos_ref[n:, :] = cos
    sin_ref[:n, :] = sin
    sin_ref[n:, :] = sin


def _rope_tables_t(pos_row, freq_col):
    t = pos_row.shape[1]
    n = freq_col.shape[0]
    tc = min(t, 4096)
    assert t % tc == 0
    out_spec = pl.BlockSpec((2 * n, tc), lambda i: (0, i))
    return pl.pallas_call(
        _rope_table_t_kernel,
        grid=(t // tc,),
        in_specs=[pl.BlockSpec((1, tc), lambda i: (0, i)),
                  pl.BlockSpec((n, 1), lambda i: (0, 0))],
        out_specs=[out_spec, out_spec],
        out_shape=[jax.ShapeDtypeStruct((2 * n, t), F32)] * 2,
        compiler_params=_params("parallel"),
        name="rope_tables_t",
    )(pos_row, freq_col)


def _inv_freq(d):
    return ROPE_BASE ** (-jnp.arange(0, d, 2, dtype=F32) / d)


def _adaln_rmsnorm(x_ref, g_ref, shift_ref, scale_ref, rows=slice(None)):
    x = x_ref[rows, :]
    y = x * lax.rsqrt(jnp.mean(x * x, axis=-1, keepdims=True) + EPS)
    return (y * (g_ref[...] * (1.0 + scale_ref[...])) + shift_ref[...]).astype(BF16)


def _mod_specs(layer, rows_per_batch_block, d):
    def spec(part):
        return pl.BlockSpec((None, None, 1, d),
                            lambda i, *_: (layer, i // rows_per_batch_block, 0, part))
    return spec


def _resident_spec(shape):
    return pl.BlockSpec(shape, lambda *_: (0,) * len(shape), pipeline_mode=pl.Buffered(1))


def _ret_inproj_kernel(x_ref, g_ref, shift_ref, scale_ref, cos_ref, sin_ref, w_ref,
                       o_ref, h_ref, *, rope_blocks, gate_block0, head_dim):
    n = pl.program_id(1)
    half = head_dim // 2
    tm = o_ref.shape[0]
    chunks = range(0, o_ref.shape[1], head_dim)
    halves = range(0, tm, tm // 2)
    pieces = [(c0, r0) for c0 in chunks for r0 in halves]

    def project(c0, r0):
        return jnp.dot(h_ref[r0:r0 + tm // 2, :], w_ref[:, c0:c0 + head_dim],
                       preferred_element_type=F32)

    def rope_piece(c0, r0):
        rows = slice(r0, r0 + tm // 2)
        acc = project(c0, r0)
        x1 = acc[:, :half]
        x2 = acc[:, half:]
        cos = cos_ref[rows, :]
        sin = sin_ref[rows, :]
        o_ref[rows, c0:c0 + half] = (x1 * cos - x2 * sin).astype(BF16)
        o_ref[rows, c0 + half:c0 + head_dim] = (x2 * cos + x1 * sin).astype(BF16)

    @pl.when(n == 0)
    def _():
        for r0 in halves:
            rows = slice(r0, r0 + tm // 2)
            h_ref[rows, :] = _adaln_rmsnorm(x_ref, g_ref, shift_ref, scale_ref, rows)
            for c0 in chunks:
                rope_piece(c0, r0)

    @pl.when((n > 0) & (n < rope_blocks))
    def _():
        for c0, r0 in pieces:
            rope_piece(c0, r0)

    @pl.when((n >= rope_blocks) & (n < gate_block0))
    def _():
        for c0, r0 in pieces:
            o_ref[r0:r0 + tm // 2, c0:c0 + head_dim] = project(c0, r0).astype(BF16)

    @pl.when(n >= gate_block0)
    def _():
        for c0, r0 in pieces:
            o_ref[r0:r0 + tm // 2, c0:c0 + head_dim] = _silu(project(c0, r0)).astype(BF16)


def _ret_inproj(x2d, norm_g, mod, layer, cos, sin, w_bf16, seq, rope_cols, gate_col, head_dim):
    t, d = x2d.shape
    n_out = w_bf16.shape[1]
    tm = min(seq, 1024)
    tn = 2048
    assert seq % tm == 0 and n_out % tn == 0 and rope_cols % tn == 0 and tn % head_dim == 0
    assert head_dim // 2 == LANES and gate_col % tn == 0
    mspec = _mod_specs(layer, seq // tm, d)
    return pl.pallas_call(
        functools.partial(_ret_inproj_kernel, rope_blocks=rope_cols // tn,
                          gate_block0=gate_col // tn, head_dim=head_dim),
        grid=(t // tm, n_out // tn),
        in_specs=[
            pl.BlockSpec((tm, d), lambda i, n: (i, 0)),
            pl.BlockSpec((1, d), lambda i, n: (0, 0)),
            mspec(0), mspec(1),
            pl.BlockSpec((tm, LANES), lambda i, n: (i, 0)),
            pl.BlockSpec((tm, LANES), lambda i, n: (i, 0)),
            pl.BlockSpec((d, tn), lambda i, n: (0, n)),
        ],
        out_specs=pl.BlockSpec((tm, tn), lambda i, n: (i, n)),
        out_shape=jax.ShapeDtypeStruct((t, n_out), BF16),
        scratch_shapes=[pltpu.VMEM((tm, d), BF16)],
        compiler_params=_params("parallel", "arbitrary"),
        name="ret_in_proj",
    )(x2d, norm_g.reshape(1, d), mod, mod, cos, sin, w_bf16)


def _retention_kernel(decay_ref, q_ref, k_ref, v_ref, sg_ref, gn_ref, dmat_ref, xi_ref,
                      zeta_ref, o_ref, state_ref, *, chunk):
    @pl.when(pl.program_id(2) == 0)
    def _():
        state_ref[...] = jnp.zeros_like(state_ref)

    decay = decay_ref[pl.program_id(1)]
    for c in range(q_ref.shape[0] // chunk):
        rows = slice(c * chunk, (c + 1) * chunk)
        q = q_ref[rows, :]
        k = k_ref[rows, :]
        v = v_ref[rows, :]
        s = lax.dot_general(q, k, (((1,), (1,)), ((), ())), preferred_element_type=F32)
        kz = (k.astype(F32) * zeta_ref[...]).astype(BF16)
        update = lax.dot_general(kz, v, (((0,), (0,)), ((), ())), preferred_element_type=F32)
        s = (s * dmat_ref[...]).astype(BF16)
        state = state_ref[...]
        y = (jnp.dot(s, v, preferred_element_type=F32)
             + jnp.dot(q, state.astype(BF16), preferred_element_type=F32) * xi_ref[...])
        state_ref[...] = state * decay + update
        yc = y - jnp.mean(y, axis=-1, keepdims=True)
        var = jnp.mean(yc * yc, axis=-1, keepdims=True)
        yn = yc * lax.rsqrt(var + EPS) * gn_ref[...]
        o_ref[rows, :] = sg_ref[rows, :] * yn.astype(BF16)


def _retention_constants(heads, chunk, dk, dv):
    f32 = np.float32
    log_gamma = np.log1p(-np.exp2(f32(-5.0) - np.arange(heads, dtype=f32))).astype(f32)
    idx = np.arange(chunk, dtype=f32)
    diff = idx[:, None] - idx[None, :]
    k_scale = f32(dk ** -0.5)
    d_intra = np.where(diff >= 0, np.exp(log_gamma[:, None, None] * np.maximum(diff, 0)),
                       0).astype(f32) * k_scale
    xi = np.exp(log_gamma[:, None] * (idx + 1)).astype(f32)
    zeta = np.exp(log_gamma[:, None] * (chunk - 1 - idx)).astype(f32) * k_scale
    chunk_decay = np.exp(log_gamma * chunk).astype(f32)
    bcast = lambda a, n: np.ascontiguousarray(np.broadcast_to(a[:, :, None], (heads, chunk, n)))
    return d_intra, bcast(xi, dv), bcast(zeta, dk), chunk_decay


def _retention(proj, gn_g, batch, seq, heads, dk, dv):
    t = proj.shape[0]
    width = heads * dv
    chunk = RET_CHUNK
    sb = min(seq, 1024)
    assert seq % sb == 0 and sb % chunk == 0
    nsb = seq // sb
    d_intra, xi, zeta, chunk_decay = _retention_constants(heads, chunk, dk, dv)
    k_blk0 = heads
    v_blk0 = 2 * heads * dk // dv
    g_blk0 = v_blk0 + heads
    row = lambda b, h, s: b * nsb + s
    return pl.pallas_call(
        functools.partial(_retention_kernel, chunk=chunk),
        grid=(batch, heads, nsb),
        in_specs=[
            pl.BlockSpec(memory_space=pltpu.SMEM),
            pl.BlockSpec((sb, dk), lambda b, h, s: (row(b, h, s), h)),
            pl.BlockSpec((sb, dk), lambda b, h, s: (row(b, h, s), k_blk0 + h)),
            pl.BlockSpec((sb, dv), lambda b, h, s: (row(b, h, s), v_blk0 + h)),
            pl.BlockSpec((sb, dv), lambda b, h, s: (row(b, h, s), g_blk0 + h)),
            pl.BlockSpec((1, dv), lambda b, h, s: (0, h)),
            pl.BlockSpec((None, chunk, chunk), lambda b, h, s: (h, 0, 0)),
            pl.BlockSpec((None, chunk, dv), lambda b, h, s: (h, 0, 0)),
            pl.BlockSpec((None, chunk, dk), lambda b, h, s: (h, 0, 0)),
        ],
        out_specs=pl.BlockSpec((sb, dv), lambda b, h, s: (row(b, h, s), h)),
        out_shape=jax.ShapeDtypeStruct((t, width), BF16),
        scratch_shapes=[pltpu.VMEM((dk, dv), F32)],
        compiler_params=_params("parallel", "parallel", "arbitrary"),
        name="retention",
    )(chunk_decay, proj, proj, proj, proj, gn_g.reshape(1, width), d_intra, xi, zeta)


def _residual_chunks(a_ref, w_ref, x_ref, gate_ref, o_ref, rows, want_ssq):
    ssq = None
    for c0 in range(0, o_ref.shape[1], OUT_CHUNK):
        cols = slice(c0, c0 + OUT_CHUNK)
        y = jnp.dot(a_ref[rows, :], w_ref[:, cols], preferred_element_type=F32)
        z = x_ref[rows, cols] + gate_ref[:, cols] * y
        o_ref[rows, cols] = z
        if want_ssq:
            part = jnp.sum(z * z, axis=-1, keepdims=True)
            ssq = part if ssq is None else ssq + part
    return ssq


def _row_halves(ref):
    half = ref.shape[0] // 2
    return (slice(0, half), slice(half, 2 * half))


def _outproj_kernel(a_ref, w_ref, x_ref, gate_ref, o_ref):
    _residual_chunks(a_ref, w_ref, x_ref, gate_ref, o_ref, slice(None), False)


def _outproj_final_kernel(a_ref, w_ref, x_ref, gate_ref, fg_ref, o_ref):
    ssq = _residual_chunks(a_ref, w_ref, x_ref, gate_ref, o_ref, slice(None), True)
    inv = lax.rsqrt(ssq * (1.0 / o_ref.shape[1]) + EPS)
    o_ref[...] = o_ref[...] * inv * fg_ref[...]


def _outproj(a, w_bf16, x2d, mod, layer, seq, final_g=None):
    t, k = a.shape
    d = w_bf16.shape[1]
    tm = min(seq, 512)
    assert seq % tm == 0 and d % OUT_CHUNK == 0
    in_specs = [
        pl.BlockSpec((tm, k), lambda i: (i, 0)),
        _resident_spec(w_bf16.shape),
        pl.BlockSpec((tm, d), lambda i: (i, 0)),
        _mod_specs(layer, seq // tm, d)(2),
    ]
    args = [a, w_bf16, x2d, mod]
    kern = _outproj_kernel
    if final_g is not None:
        in_specs.append(pl.BlockSpec((1, d), lambda i: (0, 0)))
        args.append(final_g.reshape(1, d))
        kern = _outproj_final_kernel
    return pl.pallas_call(
        kern,
        grid=(t // tm,),
        in_specs=in_specs,
        out_specs=pl.BlockSpec((tm, d), lambda i: (i, 0)),
        out_shape=jax.ShapeDtypeStruct((t, d), F32),
        compiler_params=_params("parallel"),
        name="out_proj_final" if final_g is not None else "out_proj",
    )(*args)


def _mla_inproj_kernel(x_ref, g_ref, shift_ref, scale_ref, w_ref, o_ref, h_ref, *, gate_cols):
    for rows in _row_halves(o_ref):
        h_ref[rows, :] = _adaln_rmsnorm(x_ref, g_ref, shift_ref, scale_ref, rows)
        for c0 in range(0, o_ref.shape[1], MXU_TILE):
            acc = jnp.dot(h_ref[rows, :], w_ref[:, c0:c0 + MXU_TILE],
                          preferred_element_type=F32)
            if gate_cols[0] <= c0 < gate_cols[1]:
                acc = _silu(acc)
            o_ref[rows, c0:c0 + MXU_TILE] = acc.astype(BF16)


def _mla_inproj(x2d, norm_g, mod, layer, w_bf16, seq, gate_cols):
    t, d = x2d.shape
    n_out = w_bf16.shape[1]
    tm = min(seq, 512)
    assert seq % tm == 0 and n_out % MXU_TILE == 0
    assert gate_cols[0] % MXU_TILE == 0 and gate_cols[1] % MXU_TILE == 0
    mspec = _mod_specs(layer, seq // tm, d)
    return pl.pallas_call(
        functools.partial(_mla_inproj_kernel, gate_cols=gate_cols),
        grid=(t // tm,),
        in_specs=[
            pl.BlockSpec((tm, d), lambda i: (i, 0)),
            pl.BlockSpec((1, d), lambda i: (0, 0)),
            mspec(0), mspec(1),
            _resident_spec(w_bf16.shape),
        ],
        out_specs=pl.BlockSpec((tm, n_out), lambda i: (i, 0)),
        out_shape=jax.ShapeDtypeStruct((t, n_out), BF16),
        scratch_shapes=[pltpu.VMEM((tm, d), BF16)],
        compiler_params=_params("parallel"),
        name="mla_in_proj",
    )(x2d, norm_g.reshape(1, d), mod, mod, w_bf16)


def _mla_up_kernel(cq_ref, ckv_ref, kr_ref, qg_ref, kvg_ref, wqt_ref, wk_ref, wvt_ref, cost_ref,
                   sint_ref, qt_ref, k_ref, vt_ref, *, heads, q_scale, group):
    def rms(x_ref, g_ref):
        x = x_ref[...].astype(F32)
        return x * lax.rsqrt(jnp.mean(x * x, axis=-1, keepdims=True) + EPS) * g_ref[...]

    cos_t = cost_ref[...]
    sin_t = sint_ref[...]

    ckvn_f32 = rms(ckv_ref, kvg_ref)
    ckvn = ckvn_f32.astype(BF16)
    kr_t = kr_ref[...].astype(F32).T
    kr_rope_t = kr_t[:MLA_ROPE, :] * cos_t + kr_t[MLA_ROPE:, :] * sin_t
    kr = jnp.concatenate([kr_rope_t, jnp.zeros_like(kr_rope_t)], axis=0).T.astype(BF16)
    for g0 in range(0, heads, group):
        kn = jnp.dot(ckvn, wk_ref[:, g0 * MLA_NOPE:(g0 + group) * MLA_NOPE],
                     preferred_element_type=F32)
        for j in range(group):
            o0 = (g0 + j) * MLA_HEAD_PAD
            k_ref[:, o0:o0 + LANES] = kn[:, j * MLA_NOPE:(j + 1) * MLA_NOPE].astype(BF16)
            k_ref[:, o0 + LANES:o0 + 2 * LANES] = kr

    cqn_t = rms(cq_ref, qg_ref).T.astype(BF16)
    gw = group * MLA_HEAD_PAD
    for g0 in range(0, heads, group):
        qt = jnp.dot(wqt_ref[g0 * MLA_HEAD_PAD:g0 * MLA_HEAD_PAD + gw, :], cqn_t,
                     preferred_element_type=F32)
        for j in range(group):
            r0 = j * MLA_HEAD_PAD
            o0 = (g0 + j) * MLA_HEAD_PAD
            r1 = r0 + MLA_NOPE
            qt_ref[o0:o0 + MLA_NOPE, :] = (qt[r0:r1, :] * q_scale).astype(BF16)
            rope = (qt[r1:r1 + MLA_ROPE, :] * cos_t
                    + qt[r1 + MLA_ROPE:r1 + 2 * MLA_ROPE, :] * sin_t)
            qt_ref[o0 + MLA_NOPE:o0 + MLA_NOPE + MLA_ROPE, :] = (rope * q_scale).astype(BF16)
            qt_ref[o0 + MLA_NOPE + MLA_ROPE:o0 + MLA_HEAD_PAD, :] = jnp.zeros(
                (MLA_HEAD_PAD - MLA_NOPE - MLA_ROPE, qt.shape[1]), BF16)

    ckvn_t = ckvn_f32.T.astype(BF16)
    gv = group * MLA_V
    for r0 in range(0, heads * MLA_V, gv):
        vt_ref[r0:r0 + gv, :] = jnp.dot(wvt_ref[r0:r0 + gv, :], ckvn_t,
                                        preferred_element_type=F32).astype(BF16)


def _mla_up(proj, q_norm_g, kv_norm_g, wqt_bf16, wk_bf16, wvt_bf16, cos_t, sin_t, q_rank,
            kv_rank, kr_col, heads, tm):
    t = proj.shape[0]
    assert t % tm == 0 and q_rank == kv_rank and kr_col % LANES == 0
    return pl.pallas_call(
        functools.partial(_mla_up_kernel, heads=heads,
                          q_scale=LOG2_E * (MLA_NOPE + MLA_ROPE) ** -0.5, group=4),
        grid=(t // tm,),
        in_specs=[
            pl.BlockSpec((tm, q_rank), lambda i: (i, 0)),
            pl.BlockSpec((tm, kv_rank), lambda i: (i, 1)),
            pl.BlockSpec((tm, LANES), lambda i: (i, kr_col // LANES)),
            pl.BlockSpec((1, q_rank), lambda i: (0, 0)),
            pl.BlockSpec((1, kv_rank), lambda i: (0, 0)),
            _resident_spec(wqt_bf16.shape),
            _resident_spec(wk_bf16.shape),
            _resident_spec(wvt_bf16.shape),
            pl.BlockSpec((MLA_ROPE, tm), lambda i: (0, i)),
            pl.BlockSpec((MLA_ROPE, tm), lambda i: (0, i)),
        ],
        out_specs=[
            pl.BlockSpec((None, heads * MLA_HEAD_PAD, tm), lambda i: (i, 0, 0)),
            pl.BlockSpec((tm, heads * MLA_HEAD_PAD), lambda i: (i, 0)),
            pl.BlockSpec((None, heads * MLA_V, tm), lambda i: (i, 0, 0)),
        ],
        out_shape=[
            jax.ShapeDtypeStruct((t // tm, heads * MLA_HEAD_PAD, tm), BF16),
            jax.ShapeDtypeStruct((t, heads * MLA_HEAD_PAD), BF16),
            jax.ShapeDtypeStruct((t // tm, heads * MLA_V, tm), BF16),
        ],
        compiler_params=_params("parallel"),
        name="mla_up_proj",
    )(proj, proj, proj, q_norm_g.reshape(1, q_rank), kv_norm_g.reshape(1, kv_rank),
      wqt_bf16, wk_bf16, wvt_bf16, cos_t, sin_t)


def _attn_kernel(qt_ref, k_ref, vt_ref, sg_ref, o_ref, vt1_ref, s_ref, smax_ref, acc_ref, m_ref,
                 *, tk):
    nslab = ATTN_SLABS
    tq = nslab * tk
    nq = qt_ref.shape[0] // nslab
    all_q = slice(0, tq)

    def slabs(a, b):
        return slice(a * tk, b * tk)

    for j in range(vt_ref.shape[0]):
        vt1_ref[j, :MLA_V, :] = vt_ref[j]
        vt1_ref[j, MLA_V:, :] = jnp.ones((ONES_ROWS, tk), BF16)

    def put_scores(slot, t, qi, first_slab=0):
        kt = k_ref[pl.ds(pl.multiple_of(t * tk, tk), tk), :]
        for j in range(first_slab, nslab):
            st = jnp.dot(kt, qt_ref[nslab * qi + j], preferred_element_type=F32)
            s_ref[slot, :, slabs(j, j + 1)] = st
            smax_ref[slot, :, slabs(j, j + 1)] = jnp.max(st, axis=0, keepdims=True)

    def consume(t, slot, qs, diagonal):
        st = s_ref[slot, :, qs]
        if diagonal:
            key = lax.broadcasted_iota(jnp.int32, st.shape, 0)
            qry = lax.broadcasted_iota(jnp.int32, st.shape, 1)
            st = jnp.where(key <= qry, st, NEG_BIG)
            tile_max = jnp.max(st, axis=0, keepdims=True)
        else:
            tile_max = smax_ref[slot, :, qs]
        m_old = m_ref[:, qs]
        m_new = jnp.maximum(m_old, tile_max)
        p = jnp.exp2(st - m_new).astype(BF16)
        acc_ref[:, qs] = jnp.exp2(m_old - m_new) * acc_ref[:, qs] + jnp.dot(
            vt1_ref[t], p, preferred_element_type=F32)
        m_ref[:, qs] = m_new

    def finish(q0, qs):
        acc = acc_ref[:, qs]
        o = (acc[:MLA_V, :] * (1.0 / acc[MLA_V:MLA_V + 1, :])).T
        rows = pl.ds(pl.multiple_of(q0 + qs.start, tk), qs.stop - qs.start)
        o_ref[rows, :] = (sg_ref[rows, :].astype(F32) * o).astype(BF16)

    put_scores(0, 0, 0)

    def query_block(qi, carry):
        q0 = qi * tq
        m_ref[...] = jnp.full(m_ref.shape, NEG_BIG, F32)
        acc_ref[...] = jnp.zeros_like(acc_ref)

        def full_tiles(t0, n):
            for i in range(n):
                put_scores((i + 1) % 2, t0 + i + 1, qi)
                consume(t0 + i, i % 2, all_q, False)

        def loop_body(u, c):
            full_tiles(ATTN_LOOP_TILES * u, ATTN_LOOP_TILES)
            return c

        base = nslab * qi
        lax.fori_loop(0, base // ATTN_LOOP_TILES, loop_body, 0)
        size = ATTN_LOOP_TILES // 2
        while size >= nslab:
            rem = base % (2 * size)

            @pl.when(rem >= size)
            def _(rem=rem, size=size):
                full_tiles(base - rem, size)

            size //= 2

        for j in range(nslab):
            last = j == nslab - 1
            if not last:
                put_scores((j + 1) % 2, base + j + 1, qi, first_slab=j + 1)
            consume(base + j, j % 2, slabs(j, j + 1), True)
            if last:
                put_scores(0, 0, jnp.minimum(qi + 1, nq - 1))
            else:
                consume(base + j, j % 2, slabs(j + 1, nslab), False)
            finish(q0, slabs(j, j + 1))
        return carry

    lax.fori_loop(0, nq, query_block, 0)


def _attention(qt, k, vt, proj, gate_col, batch, seq, heads, tk):
    t = k.shape[0]
    tq = ATTN_SLABS * tk
    assert seq % tq == 0 and gate_col % MLA_V == 0 and tk % LANES == 0 and ATTN_SLABS % 2 == 0
    assert ATTN_LOOP_TILES in (ATTN_SLABS, 2 * ATTN_SLABS, 4 * ATTN_SLABS)
    g_blk0 = gate_col // MLA_V
    return pl.pallas_call(
        functools.partial(_attn_kernel, tk=tk),
        grid=(batch, heads),
        in_specs=[
            pl.BlockSpec((seq // tk, MLA_HEAD_PAD, tk), lambda b, h: (b, h, 0)),
            pl.BlockSpec((seq, MLA_HEAD_PAD), lambda b, h: (b, h)),
            pl.BlockSpec((seq // tk, MLA_V, tk), lambda b, h: (b, h, 0)),
            pl.BlockSpec((seq, MLA_V), lambda b, h: (b, g_blk0 + h)),
        ],
        out_specs=pl.BlockSpec((seq, MLA_V), lambda b, h: (b, h)),
        out_shape=jax.ShapeDtypeStruct((t, heads * MLA_V), BF16),
        scratch_shapes=[
            pltpu.VMEM((seq // tk, MLA_V + ONES_ROWS, tk), BF16),
            pltpu.VMEM((2, tk, tq), F32),
            pltpu.VMEM((2, 1, tq), F32),
            pltpu.VMEM((MLA_V + ONES_ROWS, tq), F32),
            pltpu.VMEM((1, tq), F32),
        ],
        compiler_params=_params("parallel", "parallel"),
        name="mla_attention",
    )(qt, k, vt, proj)


def _rotate_half_cols(w):
    half = w.shape[-1] // 2
    return jnp.concatenate([-w[..., half:], w[..., :half]], axis=-1)


def _mla_weights(w_in, w_uq, w_ukv, q_rank, kv_rank, heads):
    w_in, w_uq, w_ukv = (w.astype(BF16) for w in (w_in, w_uq, w_ukv))
    d = w_in.shape[0]
    cq = w_in[:, :q_rank]
    ckv = w_in[:, q_rank:q_rank + kv_rank]
    kr = w_in[:, q_rank + kv_rank:q_rank + kv_rank + MLA_ROPE]
    gate = w_in[:, q_rank + kv_rank + MLA_ROPE:]
    gate_col = q_rank + kv_rank
    kr_col = gate_col + gate.shape[1]
    pad = jnp.zeros((d, -(kr_col + 2 * MLA_ROPE) % MXU_TILE), w_in.dtype)
    w_in_p = jnp.concatenate([cq, ckv, gate, kr, _rotate_half_cols(kr), pad], axis=1)
    uq = w_uq.reshape(q_rank, heads, MLA_NOPE + MLA_ROPE)
    rope = uq[..., MLA_NOPE:]
    uq_p = jnp.concatenate([uq[..., :MLA_NOPE], rope, _rotate_half_cols(rope)], axis=-1)
    uq_p = uq_p.reshape(q_rank, heads * MLA_HEAD_PAD).T
    ukv = w_ukv.reshape(kv_rank, heads, MLA_NOPE + MLA_V)
    uk_p = ukv[..., :MLA_NOPE].reshape(kv_rank, heads * MLA_NOPE)
    uv_p = ukv[..., MLA_NOPE:].reshape(kv_rank, heads * MLA_V).T
    return w_in_p, uq_p, uk_p, uv_p, gate_col, kr_col


def kernel(x, c, positions, ada_w, ada_b, norm_g, ret_w_in, ret_gn_g, ret_w_out, mla_w_in,
           mla_q_norm_g, mla_w_uq, mla_kv_norm_g, mla_w_ukv, mla_w_out, final_norm_g):
    batch, seq, d = x.shape
    t = batch * seq
    depth = ada_w.shape[0]
    assert depth % 2 == 0, "the last layer must be a latent-attention layer (final norm fusion)"
    ret_dk = d // RET_HEADS
    ret_dv = ret_w_out.shape[1] // RET_HEADS
    q_rank = mla_q_norm_g.shape[1]
    kv_rank = mla_kv_norm_g.shape[1]

    mod = _modulation(c, ada_w, ada_b)

    pos = positions.reshape(t, 1).astype(F32)
    cos_r, sin_r = _rope_tables(jnp.broadcast_to(pos, (t, LANES)),
                                _inv_freq(ret_dk).reshape(1, LANES))
    cos_t, sin_t = _rope_tables_t(positions.reshape(1, t).astype(F32),
                                  _inv_freq(MLA_ROPE).reshape(MLA_ROPE // 2, 1))
    attn_tk = min(seq // ATTN_SLABS, 512)

    x2d = x.reshape(t, d)
    for layer in range(depth):
        j = layer // 2
        if layer % 2 == 0:
            proj = _ret_inproj(x2d, norm_g[layer], mod, layer, cos_r, sin_r,
                               ret_w_in[j].astype(BF16), seq, 2 * d,
                               2 * d + RET_HEADS * ret_dv, ret_dk)
            y = _retention(proj, ret_gn_g[j], batch, seq, RET_HEADS, ret_dk, ret_dv)
            x2d = _outproj(y, ret_w_out[j].astype(BF16), x2d, mod, layer, seq)
        else:
            w_in_p, uq_p, uk_p, uv_p, gate_col, kr_col = _mla_weights(
                mla_w_in[j], mla_w_uq[j], mla_w_ukv[j], q_rank, kv_rank, MLA_HEADS)
            proj = _mla_inproj(x2d, norm_g[layer], mod, layer, w_in_p, seq,
                               (gate_col, kr_col))
            qt, k, vt = _mla_up(proj, mla_q_norm_g[j], mla_kv_norm_g[j], uq_p, uk_p, uv_p,
                                cos_t, sin_t, q_rank, kv_rank, kr_col, MLA_HEADS, attn_tk)
            o = _attention(qt, k, vt, proj, gate_col, batch, seq, MLA_HEADS, attn_tk)
            final_g = final_norm_g if layer == depth - 1 else None
            x2d = _outproj(o, mla_w_out[j].astype(BF16), x2d, mod, layer, seq, final_g)
    return x2d.reshape(batch, seq, d)
```

```python
import functools

import jax
import jax.numpy as jnp
import numpy as np
from jax import lax
from jax.experimental import pallas as pl
from jax.experimental.pallas import tpu as pltpu

F32 = jnp.float32
BF16 = jnp.bfloat16

EPS = 1e-6
ROPE_BASE = 10000.0
NEG_BIG = -1e30
LOG2_E = 1.4426950408889634

LANES = 128
SUBLANES = 8
MXU_TILE = 256
OUT_CHUNK = 2 * MXU_TILE
VMEM_LIMIT_BYTES = 56 * 2**20
VMEM_BLOCK_BUDGET = 50 * 2**20

RET_HEADS = 8
RET_CHUNK = 256
MLA_HEADS = 16
MLA_NOPE = 128
MLA_ROPE = 64
MLA_V = 128
MLA_HEAD_PAD = 256
ONES_ROWS = 16
ATTN_SLABS = 2
ATTN_LOOP_TILES = 8


def _params(*semantics):
    return pltpu.CompilerParams(dimension_semantics=semantics,
                                vmem_limit_bytes=VMEM_LIMIT_BYTES)


def _silu(x):
    h = 0.5 * x
    return h + h * jnp.tanh(h)


def _mod_kernel(c_ref, w_ref, b_ref, o_ref):
    act = _silu(c_ref[...]).astype(BF16)
    o_ref[...] = jnp.dot(act, w_ref[...].astype(BF16),
                         preferred_element_type=F32) + b_ref[...]


def _modulation(c, ada_w, ada_b):
    depth, d, n3 = ada_w.shape
    b = c.shape[0]
    assert b <= SUBLANES
    tn = 768
    assert n3 % tn == 0
    c_pad = jnp.zeros((SUBLANES, d), F32).at[:b].set(c)
    out = pl.pallas_call(
        _mod_kernel,
        grid=(depth, n3 // tn),
        in_specs=[
            pl.BlockSpec((SUBLANES, d), lambda i, n: (0, 0)),
            pl.BlockSpec((None, d, tn), lambda i, n: (i, 0, n)),
            pl.BlockSpec((None, 1, tn), lambda i, n: (i, 0, n)),
        ],
        out_specs=pl.BlockSpec((None, SUBLANES, tn), lambda i, n: (i, 0, n)),
        out_shape=jax.ShapeDtypeStruct((depth, SUBLANES, n3), F32),
        compiler_params=_params("parallel", "parallel"),
        name="adaln_modulation",
    )(c_pad, ada_w, ada_b.reshape(depth, 1, n3))
    return out[:, :b].reshape(depth, b, 1, n3)


def _rope_table_kernel(pos_ref, freq_ref, cos_ref, sin_ref):
    ang = pos_ref[...] * freq_ref[...]
    cos_ref[...] = jnp.cos(ang)
    sin_ref[...] = jnp.sin(ang)


def _rope_tables(pos_lanes, freq_lanes):
    rows = pos_lanes.shape[0]
    tr = min(rows, 1024)
    assert rows % tr == 0
    spec = pl.BlockSpec((tr, LANES), lambda i: (i, 0))
    return pl.pallas_call(
        _rope_table_kernel,
        grid=(rows // tr,),
        in_specs=[spec, pl.BlockSpec((1, LANES), lambda i: (0, 0))],
        out_specs=[spec, spec],
        out_shape=[jax.ShapeDtypeStruct((rows, LANES), F32)] * 2,
        compiler_params=_params("parallel"),
        name="rope_tables",
    )(pos_lanes, freq_lanes)


def _rope_table_t_kernel(pos_ref, freq_ref, cos_ref, sin_ref):
    ang = freq_ref[...] * pos_ref[...]
    n = ang.shape[0]
    cos = jnp.cos(ang)
    sin = jnp.sin(ang)
    cos_ref[:n, :] = cos
    cos_ref[n:, :] = cos
    sin_ref[:n, :] = sin
    sin_ref[n:, :] = sin


def _rope_tables_t(pos_row, freq_col):
    t = pos_row.shape[1]
    n = freq_col.shape[0]
    tc = min(t, 4096)
    assert t % tc == 0
    out_spec = pl.BlockSpec((2 * n, tc), lambda i: (0, i))
    return pl.pallas_call(
        _rope_table_t_kernel,
        grid=(t // tc,),
        in_specs=[pl.BlockSpec((1, tc), lambda i: (0, i)),
                  pl.BlockSpec((n, 1), lambda i: (0, 0))],
        out_specs=[out_spec, out_spec],
        out_shape=[jax.ShapeDtypeStruct((2 * n, t), F32)] * 2,
        compiler_params=_params("parallel"),
        name="rope_tables_t",
    )(pos_row, freq_col)


def _inv_freq(d):
    return ROPE_BASE ** (-jnp.arange(0, d, 2, dtype=F32) / d)


def _adaln_rmsnorm(x_ref, g_ref, shift_ref, scale_ref, rows=slice(None)):
    x = x_ref[rows, :]
    y = x * lax.rsqrt(jnp.mean(x * x, axis=-1, keepdims=True) + EPS)
    return (y * (g_ref[...] * (1.0 + scale_ref[...])) + shift_ref[...]).astype(BF16)


def _mod_specs(layer, rows_per_batch_block, d):
    def spec(part):
        return pl.BlockSpec((None, None, 1, d),
                            lambda i, *_: (layer, i // rows_per_batch_block, 0, part))
    return spec


def _resident_spec(shape):
    return pl.BlockSpec(shape, lambda *_: (0,) * len(shape), pipeline_mode=pl.Buffered(1))


def _ret_inproj_kernel(x_ref, g_ref, shift_ref, scale_ref, cos_ref, sin_ref, w_ref,
                       o_ref, h_ref, *, rope_blocks, gate_block0, head_dim):
    n = pl.program_id(1)
    half = head_dim // 2
    tm = o_ref.shape[0]
    chunks = range(0, o_ref.shape[1], head_dim)
    halves = range(0, tm, tm // 2)
    pieces = [(c0, r0) for c0 in chunks for r0 in halves]

    def project(c0, r0):
        return jnp.dot(h_ref[r0:r0 + tm // 2, :], w_ref[:, c0:c0 + head_dim],
                       preferred_element_type=F32)

    def rope_piece(c0, r0):
        rows = slice(r0, r0 + tm // 2)
        acc = project(c0, r0)
        x1 = acc[:, :half]
        x2 = acc[:, half:]
        cos = cos_ref[rows, :]
        sin = sin_ref[rows, :]
        o_ref[rows, c0:c0 + half] = (x1 * cos - x2 * sin).astype(BF16)
        o_ref[rows, c0 + half:c0 + head_dim] = (x2 * cos + x1 * sin).astype(BF16)

    @pl.when(n == 0)
    def _():
        for r0 in halves:
            rows = slice(r0, r0 + tm // 2)
            h_ref[rows, :] = _adaln_rmsnorm(x_ref, g_ref, shift_ref, scale_ref, rows)
            for c0 in chunks:
                rope_piece(c0, r0)

    @pl.when((n > 0) & (n < rope_blocks))
    def _():
        for c0, r0 in pieces:
            rope_piece(c0, r0)

    @pl.when((n >= rope_blocks) & (n < gate_block0))
    def _():
        for c0, r0 in pieces:
            o_ref[r0:r0 + tm // 2, c0:c0 + head_dim] = project(c0, r0).astype(BF16)

    @pl.when(n >= gate_block0)
    def _():
        for c0, r0 in pieces:
            o_ref[r0:r0 + tm // 2, c0:c0 + head_dim] = _silu(project(c0, r0)).astype(BF16)


def _ret_inproj(x2d, norm_g, mod, layer, cos, sin, w_bf16, seq, rope_cols, gate_col, head_dim):
    t, d = x2d.shape
    n_out = w_bf16.shape[1]
    tm = min(seq, 1024)
    tn = 2048
    assert seq % tm == 0 and n_out % tn == 0 and rope_cols % tn == 0 and tn % head_dim == 0
    assert head_dim // 2 == LANES and gate_col % tn == 0
    mspec = _mod_specs(layer, seq // tm, d)
    return pl.pallas_call(
        functools.partial(_ret_inproj_kernel, rope_blocks=rope_cols // tn,
                          gate_block0=gate_col // tn, head_dim=head_dim),
        grid=(t // tm, n_out // tn),
        in_specs=[
            pl.BlockSpec((tm, d), lambda i, n: (i, 0)),
            pl.BlockSpec((1, d), lambda i, n: (0, 0)),
            mspec(0), mspec(1),
            pl.BlockSpec((tm, LANES), lambda i, n: (i, 0)),
            pl.BlockSpec((tm, LANES), lambda i, n: (i, 0)),
            pl.BlockSpec((d, tn), lambda i, n: (0, n)),
        ],
        out_specs=pl.BlockSpec((tm, tn), lambda i, n: (i, n)),
        out_shape=jax.ShapeDtypeStruct((t, n_out), BF16),
        scratch_shapes=[pltpu.VMEM((tm, d), BF16)],
        compiler_params=_params("parallel", "arbitrary"),
        name="ret_in_proj",
    )(x2d, norm_g.reshape(1, d), mod, mod, cos, sin, w_bf16)


def _retention_kernel(decay_ref, q_ref, k_ref, v_ref, sg_ref, gn_ref, dmat_ref, xi_ref,
                      zeta_ref, o_ref, state_ref, *, chunk):
    @pl.when(pl.program_id(2) == 0)
    def _():
        state_ref[...] = jnp.zeros_like(state_ref)

    decay = decay_ref[pl.program_id(1)]
    for c in range(q_ref.shape[0] // chunk):
        rows = slice(c * chunk, (c + 1) * chunk)
        q = q_ref[rows, :]
        k = k_ref[rows, :]
        v = v_ref[rows, :]
        s = lax.dot_general(q, k, (((1,), (1,)), ((), ())), preferred_element_type=F32)
        kz = (k.astype(F32) * zeta_ref[...]).astype(BF16)
        update = lax.dot_general(kz, v, (((0,), (0,)), ((), ())), preferred_element_type=F32)
        s = (s * dmat_ref[...]).astype(BF16)
        state = state_ref[...]
        y = (jnp.dot(s, v, preferred_element_type=F32)
             + jnp.dot(q, state.astype(BF16), preferred_element_type=F32) * xi_ref[...])
        state_ref[...] = state * decay + update
        yc = y - jnp.mean(y, axis=-1, keepdims=True)
        var = jnp.mean(yc * yc, axis=-1, keepdims=True)
        yn = yc * lax.rsqrt(var + EPS) * gn_ref[...]
        o_ref[rows, :] = sg_ref[rows, :] * yn.astype(BF16)


def _retention_constants(heads, chunk, dk, dv):
    f32 = np.float32
    log_gamma = np.log1p(-np.exp2(f32(-5.0) - np.arange(heads, dtype=f32))).astype(f32)
    idx = np.arange(chunk, dtype=f32)
    diff = idx[:, None] - idx[None, :]
    k_scale = f32(dk ** -0.5)
    d_intra = np.where(diff >= 0, np.exp(log_gamma[:, None, None] * np.maximum(diff, 0)),
                       0).astype(f32) * k_scale
    xi = np.exp(log_gamma[:, None] * (idx + 1)).astype(f32)
    zeta = np.exp(log_gamma[:, None] * (chunk - 1 - idx)).astype(f32) * k_scale
    chunk_decay = np.exp(log_gamma * chunk).astype(f32)
    bcast = lambda a, n: np.ascontiguousarray(np.broadcast_to(a[:, :, None], (heads, chunk, n)))
    return d_intra, bcast(xi, dv), bcast(zeta, dk), chunk_decay


def _retention(proj, gn_g, batch, seq, heads, dk, dv):
    t = proj.shape[0]
    width = heads * dv
    chunk = RET_CHUNK
    sb = min(seq, 1024)
    assert seq % sb == 0 and sb % chunk == 0
    nsb = seq // sb
    d_intra, xi, zeta, chunk_decay = _retention_constants(heads, chunk, dk, dv)
    k_blk0 = heads
    v_blk0 = 2 * heads * dk // dv
    g_blk0 = v_blk0 + heads
    row = lambda b, h, s: b * nsb + s
    return pl.pallas_call(
        functools.partial(_retention_kernel, chunk=chunk),
        grid=(batch, heads, nsb),
        in_specs=[
            pl.BlockSpec(memory_space=pltpu.SMEM),
            pl.BlockSpec((sb, dk), lambda b, h, s: (row(b, h, s), h)),
            pl.BlockSpec((sb, dk), lambda b, h, s: (row(b, h, s), k_blk0 + h)),
            pl.BlockSpec((sb, dv), lambda b, h, s: (row(b, h, s), v_blk0 + h)),
            pl.BlockSpec((sb, dv), lambda b, h, s: (row(b, h, s), g_blk0 + h)),
            pl.BlockSpec((1, dv), lambda b, h, s: (0, h)),
            pl.BlockSpec((None, chunk, chunk), lambda b, h, s: (h, 0, 0)),
            pl.BlockSpec((None, chunk, dv), lambda b, h, s: (h, 0, 0)),
            pl.BlockSpec((None, chunk, dk), lambda b, h, s: (h, 0, 0)),
        ],
        out_specs=pl.BlockSpec((sb, dv), lambda b, h, s: (row(b, h, s), h)),
        out_shape=jax.ShapeDtypeStruct((t, width), BF16),
        scratch_shapes=[pltpu.VMEM((dk, dv), F32)],
        compiler_params=_params("parallel", "parallel", "arbitrary"),
        name="retention",
    )(chunk_decay, proj, proj, proj, proj, gn_g.reshape(1, width), d_intra, xi, zeta)


def _residual_chunks(a_ref, w_ref, x_ref, gate_ref, o_ref, rows, want_ssq):
    ssq = None
    for c0 in range(0, o_ref.shape[1], OUT_CHUNK):
        cols = slice(c0, c0 + OUT_CHUNK)
        y = jnp.dot(a_ref[rows, :], w_ref[:, cols], preferred_element_type=F32)
        z = x_ref[rows, cols] + gate_ref[:, cols] * y
        o_ref[rows, cols] = z
        if want_ssq:
            part = jnp.sum(z * z, axis=-1, keepdims=True)
            ssq = part if ssq is None else ssq + part
    return ssq


def _row_halves(ref):
    half = ref.shape[0] // 2
    return (slice(0, half), slice(half, 2 * half))


def _outproj_kernel(a_ref, w_ref, x_ref, gate_ref, o_ref):
    _residual_chunks(a_ref, w_ref, x_ref, gate_ref, o_ref, slice(None), False)


def _outproj_final_kernel(a_ref, w_ref, x_ref, gate_ref, fg_ref, o_ref):
    ssq = _residual_chunks(a_ref, w_ref, x_ref, gate_ref, o_ref, slice(None), True)
    inv = lax.rsqrt(ssq * (1.0 / o_ref.shape[1]) + EPS)
    o_ref[...] = o_ref[...] * inv * fg_ref[...]


def _outproj(a, w_bf16, x2d, mod, layer, seq, final_g=None):
    t, k = a.shape
    d = w_bf16.shape[1]

    def block_bytes(rows):
        return 2 * rows * k * 2 + 2 * 2 * rows * d * 4 + k * d * 2

    tm = min(seq, 1024 if block_bytes(1024) <= VMEM_BLOCK_BUDGET else 512)
    assert seq % tm == 0 and d % OUT_CHUNK == 0
    in_specs = [
        pl.BlockSpec((tm, k), lambda i: (i, 0)),
        _resident_spec(w_bf16.shape),
        pl.BlockSpec((tm, d), lambda i: (i, 0)),
        _mod_specs(layer, seq // tm, d)(2),
    ]
    args = [a, w_bf16, x2d, mod]
    kern = _outproj_kernel
    if final_g is not None:
        in_specs.append(pl.BlockSpec((1, d), lambda i: (0, 0)))
        args.append(final_g.reshape(1, d))
        kern = _outproj_final_kernel
    return pl.pallas_call(
        kern,
        grid=(t // tm,),
        in_specs=in_specs,
        out_specs=pl.BlockSpec((tm, d), lambda i: (i, 0)),
        out_shape=jax.ShapeDtypeStruct((t, d), F32),
        compiler_params=_params("parallel"),
        name="out_proj_final" if final_g is not None else "out_proj",
    )(*args)


def _mla_inproj_kernel(x_ref, g_ref, shift_ref, scale_ref, w_ref, o_ref, h_ref, *, gate_cols):
    for rows in _row_halves(o_ref):
        h_ref[rows, :] = _adaln_rmsnorm(x_ref, g_ref, shift_ref, scale_ref, rows)
        for c0 in range(0, o_ref.shape[1], MXU_TILE):
            acc = jnp.dot(h_ref[rows, :], w_ref[:, c0:c0 + MXU_TILE],
                          preferred_element_type=F32)
            if gate_cols[0] <= c0 < gate_cols[1]:
                acc = _silu(acc)
            o_ref[rows, c0:c0 + MXU_TILE] = acc.astype(BF16)


def _mla_inproj(x2d, norm_g, mod, layer, w_bf16, seq, gate_cols):
    t, d = x2d.shape
    n_out = w_bf16.shape[1]
    tm = min(seq, 1024)
    assert seq % tm == 0 and n_out % MXU_TILE == 0
    assert gate_cols[0] % MXU_TILE == 0 and gate_cols[1] % MXU_TILE == 0
    mspec = _mod_specs(layer, seq // tm, d)
    return pl.pallas_call(
        functools.partial(_mla_inproj_kernel, gate_cols=gate_cols),
        grid=(t // tm,),
        in_specs=[
            pl.BlockSpec((tm, d), lambda i: (i, 0)),
            pl.BlockSpec((1, d), lambda i: (0, 0)),
            mspec(0), mspec(1),
            _resident_spec(w_bf16.shape),
        ],
        out_specs=pl.BlockSpec((tm, n_out), lambda i: (i, 0)),
        out_shape=jax.ShapeDtypeStruct((t, n_out), BF16),
        scratch_shapes=[pltpu.VMEM((tm, d), BF16)],
        compiler_params=_params("parallel"),
        name="mla_in_proj",
    )(x2d, norm_g.reshape(1, d), mod, mod, w_bf16)


def _mla_up_kernel(cq_ref, ckv_ref, kr_ref, qg_ref, kvg_ref, wqt_ref, wk_ref, wvt_ref, cost_ref,
                   sint_ref, qt_ref, k_ref, vt_ref, *, heads, q_scale, group):
    def rms(x_ref, g_ref):
        x = x_ref[...].astype(F32)
        return x * lax.rsqrt(jnp.mean(x * x, axis=-1, keepdims=True) + EPS) * g_ref[...]

    cos_t = cost_ref[...]
    sin_t = sint_ref[...]

    ckvn_f32 = rms(ckv_ref, kvg_ref)
    ckvn = ckvn_f32.astype(BF16)
    kr_t = kr_ref[...].astype(F32).T
    kr_rope_t = kr_t[:MLA_ROPE, :] * cos_t + kr_t[MLA_ROPE:, :] * sin_t
    kr = jnp.concatenate([kr_rope_t, jnp.zeros_like(kr_rope_t)], axis=0).T.astype(BF16)
    for g0 in range(0, heads, group):
        kn = jnp.dot(ckvn, wk_ref[:, g0 * MLA_NOPE:(g0 + group) * MLA_NOPE],
                     preferred_element_type=F32)
        for j in range(group):
            o0 = (g0 + j) * MLA_HEAD_PAD
            k_ref[:, o0:o0 + LANES] = kn[:, j * MLA_NOPE:(j + 1) * MLA_NOPE].astype(BF16)
            k_ref[:, o0 + LANES:o0 + 2 * LANES] = kr

    cqn_t = rms(cq_ref, qg_ref).T.astype(BF16)
    gw = group * MLA_HEAD_PAD
    for g0 in range(0, heads, group):
        qt = jnp.dot(wqt_ref[g0 * MLA_HEAD_PAD:g0 * MLA_HEAD_PAD + gw, :], cqn_t,
                     preferred_element_type=F32)
        for j in range(group):
            r0 = j * MLA_HEAD_PAD
            o0 = (g0 + j) * MLA_HEAD_PAD
            r1 = r0 + MLA_NOPE
            qt_ref[o0:o0 + MLA_NOPE, :] = (qt[r0:r1, :] * q_scale).astype(BF16)
            rope = (qt[r1:r1 + MLA_ROPE, :] * cos_t
                    + qt[r1 + MLA_ROPE:r1 + 2 * MLA_ROPE, :] * sin_t)
            qt_ref[o0 + MLA_NOPE:o0 + MLA_NOPE + MLA_ROPE, :] = (rope * q_scale).astype(BF16)
            qt_ref[o0 + MLA_NOPE + MLA_ROPE:o0 + MLA_HEAD_PAD, :] = jnp.zeros(
                (MLA_HEAD_PAD - MLA_NOPE - MLA_ROPE, qt.shape[1]), BF16)

    ckvn_t = ckvn_f32.T.astype(BF16)
    gv = group * MLA_V
    for r0 in range(0, heads * MLA_V, gv):
        vt_ref[r0:r0 + gv, :] = jnp.dot(wvt_ref[r0:r0 + gv, :], ckvn_t,
                                        preferred_element_type=F32).astype(BF16)


def _mla_up(proj, q_norm_g, kv_norm_g, wqt_bf16, wk_bf16, wvt_bf16, cos_t, sin_t, q_rank,
            kv_rank, kr_col, heads, tm):
    t = proj.shape[0]
    assert t % tm == 0 and q_rank == kv_rank and kr_col % LANES == 0
    return pl.pallas_call(
        functools.partial(_mla_up_kernel, heads=heads,
                          q_scale=LOG2_E * (MLA_NOPE + MLA_ROPE) ** -0.5, group=4),
        grid=(t // tm,),
        in_specs=[
            pl.BlockSpec((tm, q_rank), lambda i: (i, 0)),
            pl.BlockSpec((tm, kv_rank), lambda i: (i, 1)),
            pl.BlockSpec((tm, LANES), lambda i: (i, kr_col // LANES)),
            pl.BlockSpec((1, q_rank), lambda i: (0, 0)),
            pl.BlockSpec((1, kv_rank), lambda i: (0, 0)),
            _resident_spec(wqt_bf16.shape),
            _resident_spec(wk_bf16.shape),
            _resident_spec(wvt_bf16.shape),
            pl.BlockSpec((MLA_ROPE, tm), lambda i: (0, i)),
            pl.BlockSpec((MLA_ROPE, tm), lambda i: (0, i)),
        ],
        out_specs=[
            pl.BlockSpec((None, heads * MLA_HEAD_PAD, tm), lambda i: (i, 0, 0)),
            pl.BlockSpec((tm, heads * MLA_HEAD_PAD), lambda i: (i, 0)),
            pl.BlockSpec((None, heads * MLA_V, tm), lambda i: (i, 0, 0)),
        ],
        out_shape=[
            jax.ShapeDtypeStruct((t // tm, heads * MLA_HEAD_PAD, tm), BF16),
            jax.ShapeDtypeStruct((t, heads * MLA_HEAD_PAD), BF16),
            jax.ShapeDtypeStruct((t // tm, heads * MLA_V, tm), BF16),
        ],
        compiler_params=_params("parallel"),
        name="mla_up_proj",
    )(proj, proj, proj, q_norm_g.reshape(1, q_rank), kv_norm_g.reshape(1, kv_rank),
      wqt_bf16, wk_bf16, wvt_bf16, cos_t, sin_t)


def _attn_kernel(qt_ref, k_ref, vt_ref, sg_ref, o_ref, vt1_ref, s_ref, smax_ref, acc_ref, m_ref,
                 *, tk):
    nslab = ATTN_SLABS
    tq = nslab * tk
    nq = qt_ref.shape[0] // nslab
    all_q = slice(0, tq)

    def slabs(a, b):
        return slice(a * tk, b * tk)

    for j in range(vt_ref.shape[0]):
        vt1_ref[j, :MLA_V, :] = vt_ref[j]
        vt1_ref[j, MLA_V:, :] = jnp.ones((ONES_ROWS, tk), BF16)

    def put_scores(slot, t, qi, first_slab=0):
        kt = k_ref[pl.ds(pl.multiple_of(t * tk, tk), tk), :]
        for j in range(first_slab, nslab):
            st = jnp.dot(kt, qt_ref[nslab * qi + j], preferred_element_type=F32)
            s_ref[slot, :, slabs(j, j + 1)] = st
            smax_ref[slot, :, slabs(j, j + 1)] = jnp.max(st, axis=0, keepdims=True)

    def consume(t, slot, qs, diagonal):
        if diagonal:
            half = (qs.stop - qs.start) // 2
            consume_keys(t, slot, slice(qs.start, qs.start + half), half, 0)
            consume_keys(t, slot, slice(qs.start + half, qs.stop), 2 * half, half)
        else:
            consume_keys(t, slot, qs, tk, None)

    def consume_keys(t, slot, qs, nkeys, first_query):
        st = s_ref[slot, :nkeys, qs]
        if first_query is None:
            tile_max = smax_ref[slot, :, qs]
        else:
            key = lax.broadcasted_iota(jnp.int32, st.shape, 0)
            qry = lax.broadcasted_iota(jnp.int32, st.shape, 1) + first_query
            st = jnp.where(key <= qry, st, NEG_BIG)
            tile_max = jnp.max(st, axis=0, keepdims=True)
        m_old = m_ref[:, qs]
        m_new = jnp.maximum(m_old, tile_max)
        p = jnp.exp2(st - m_new).astype(BF16)
        acc_ref[:, qs] = jnp.exp2(m_old - m_new) * acc_ref[:, qs] + jnp.dot(
            vt1_ref[t, :, :nkeys], p, preferred_element_type=F32)
        m_ref[:, qs] = m_new

    def finish(q0, qs):
        acc = acc_ref[:, qs]
        o = (acc[:MLA_V, :] * (1.0 / acc[MLA_V:MLA_V + 1, :])).T
        rows = pl.ds(pl.multiple_of(q0 + qs.start, tk), qs.stop - qs.start)
        o_ref[rows, :] = (sg_ref[rows, :].astype(F32) * o).astype(BF16)

    put_scores(0, 0, 0)

    def query_block(qi, carry):
        q0 = qi * tq
        m_ref[...] = jnp.full(m_ref.shape, NEG_BIG, F32)
        acc_ref[...] = jnp.zeros_like(acc_ref)

        def full_tiles(t0, n):
            for i in range(n):
                put_scores((i + 1) % 2, t0 + i + 1, qi)
                consume(t0 + i, i % 2, all_q, False)

        def loop_body(u, c):
            full_tiles(ATTN_LOOP_TILES * u, ATTN_LOOP_TILES)
            return c

        base = nslab * qi
        lax.fori_loop(0, base // ATTN_LOOP_TILES, loop_body, 0)
        size = ATTN_LOOP_TILES // 2
        while size >= nslab:
            rem = base % (2 * size)

            @pl.when(rem >= size)
            def _(rem=rem, size=size):
                full_tiles(base - rem, size)

            size //= 2

        for j in range(nslab):
            last = j == nslab - 1
            if not last:
                put_scores((j + 1) % 2, base + j + 1, qi, first_slab=j + 1)
            consume(base + j, j % 2, slabs(j, j + 1), True)
            if last:
                put_scores(0, 0, jnp.minimum(qi + 1, nq - 1))
            else:
                consume(base + j, j % 2, slabs(j + 1, nslab), False)
            finish(q0, slabs(j, j + 1))
        return carry

    lax.fori_loop(0, nq, query_block, 0)


def _attention(qt, k, vt, proj, gate_col, batch, seq, heads, tk):
    t = k.shape[0]
    tq = ATTN_SLABS * tk
    assert seq % tq == 0 and gate_col % MLA_V == 0 and tk % LANES == 0 and ATTN_SLABS % 2 == 0
    assert ATTN_LOOP_TILES in (ATTN_SLABS, 2 * ATTN_SLABS, 4 * ATTN_SLABS)
    g_blk0 = gate_col // MLA_V
    return pl.pallas_call(
        functools.partial(_attn_kernel, tk=tk),
        grid=(batch, heads),
        in_specs=[
            pl.BlockSpec((seq // tk, MLA_HEAD_PAD, tk), lambda b, h: (b, h, 0)),
            pl.BlockSpec((seq, MLA_HEAD_PAD), lambda b, h: (b, h)),
            pl.BlockSpec((seq // tk, MLA_V, tk), lambda b, h: (b, h, 0)),
            pl.BlockSpec((seq, MLA_V), lambda b, h: (b, g_blk0 + h)),
        ],
        out_specs=pl.BlockSpec((seq, MLA_V), lambda b, h: (b, h)),
        out_shape=jax.ShapeDtypeStruct((t, heads * MLA_V), BF16),
        scratch_shapes=[
            pltpu.VMEM((seq // tk, MLA_V + ONES_ROWS, tk), BF16),
            pltpu.VMEM((2, tk, tq), F32),
            pltpu.VMEM((2, 1, tq), F32),
            pltpu.VMEM((MLA_V + ONES_ROWS, tq), F32),
            pltpu.VMEM((1, tq), F32),
        ],
        compiler_params=_params("parallel", "parallel"),
        name="mla_attention",
    )(qt, k, vt, proj)


def _rotate_half_cols(w):
    half = w.shape[-1] // 2
    return jnp.concatenate([-w[..., half:], w[..., :half]], axis=-1)


def _mla_weights(w_in, w_uq, w_ukv, q_rank, kv_rank, heads):
    w_in, w_uq, w_ukv = (w.astype(BF16) for w in (w_in, w_uq, w_ukv))
    d = w_in.shape[0]
    cq = w_in[:, :q_rank]
    ckv = w_in[:, q_rank:q_rank + kv_rank]
    kr = w_in[:, q_rank + kv_rank:q_rank + kv_rank + MLA_ROPE]
    gate = w_in[:, q_rank + kv_rank + MLA_ROPE:]
    gate_col = q_rank + kv_rank
    kr_col = gate_col + gate.shape[1]
    pad = jnp.zeros((d, -(kr_col + 2 * MLA_ROPE) % MXU_TILE), w_in.dtype)
    w_in_p = jnp.concatenate([cq, ckv, gate, kr, _rotate_half_cols(kr), pad], axis=1)
    uq = w_uq.reshape(q_rank, heads, MLA_NOPE + MLA_ROPE)
    rope = uq[..., MLA_NOPE:]
    uq_p = jnp.concatenate([uq[..., :MLA_NOPE], rope, _rotate_half_cols(rope)], axis=-1)
    uq_p = uq_p.reshape(q_rank, heads * MLA_HEAD_PAD).T
    ukv = w_ukv.reshape(kv_rank, heads, MLA_NOPE + MLA_V)
    uk_p = ukv[..., :MLA_NOPE].reshape(kv_rank, heads * MLA_NOPE)
    uv_p = ukv[..., MLA_NOPE:].reshape(kv_rank, heads * MLA_V).T
    return w_in_p, uq_p, uk_p, uv_p, gate_col, kr_col


def kernel(x, c, positions, ada_w, ada_b, norm_g, ret_w_in, ret_gn_g, ret_w_out, mla_w_in,
           mla_q_norm_g, mla_w_uq, mla_kv_norm_g, mla_w_ukv, mla_w_out, final_norm_g):
    batch, seq, d = x.shape
    t = batch * seq
    depth = ada_w.shape[0]
    assert depth % 2 == 0, "the last layer must be a latent-attention layer (final norm fusion)"
    ret_dk = d // RET_HEADS
    ret_dv = ret_w_out.shape[1] // RET_HEADS
    q_rank = mla_q_norm_g.shape[1]
    kv_rank = mla_kv_norm_g.shape[1]

    mod = _modulation(c, ada_w, ada_b)

    pos = positions.reshape(t, 1).astype(F32)
    cos_r, sin_r = _rope_tables(jnp.broadcast_to(pos, (t, LANES)),
                                _inv_freq(ret_dk).reshape(1, LANES))
    cos_t, sin_t = _rope_tables_t(positions.reshape(1, t).astype(F32),
                                  _inv_freq(MLA_ROPE).reshape(MLA_ROPE // 2, 1))
    attn_tk = min(seq // ATTN_SLABS, 512)

    x2d = x.reshape(t, d)
    for layer in range(depth):
        j = layer // 2
        if layer % 2 == 0:
            proj = _ret_inproj(x2d, norm_g[layer], mod, layer, cos_r, sin_r,
                               ret_w_in[j].astype(BF16), seq, 2 * d,
                               2 * d + RET_HEADS * ret_dv, ret_dk)
            y = _retention(proj, ret_gn_g[j], batch, seq, RET_HEADS, ret_dk, ret_dv)
            x2d = _outproj(y, ret_w_out[j].astype(BF16), x2d, mod, layer, seq)
        else:
            w_in_p, uq_p, uk_p, uv_p, gate_col, kr_col = _mla_weights(
                mla_w_in[j], mla_w_uq[j], mla_w_ukv[j], q_rank, kv_rank, MLA_HEADS)
            proj = _mla_inproj(x2d, norm_g[layer], mod, layer, w_in_p, seq,
                               (gate_col, kr_col))
            qt, k, vt = _mla_up(proj, mla_q_norm_g[j], mla_kv_norm_g[j], uq_p, uk_p, uv_p,
                                cos_t, sin_t, q_rank, kv_rank, kr_col, MLA_HEADS, attn_tk)
            o = _attention(qt, k, vt, proj, gate_col, batch, seq, MLA_HEADS, attn_tk)
            final_g = final_norm_g if layer == depth - 1 else None
            x2d = _outproj(o, mla_w_out[j].astype(BF16), x2d, mod, layer, seq, final_g)
    return x2d.reshape(batch, seq, d)
```

```python
import functools

import jax
import jax.numpy as jnp
import numpy as np
from jax import lax
from jax.experimental import pallas as pl
from jax.experimental.pallas import tpu as pltpu

F32 = jnp.float32
BF16 = jnp.bfloat16

EPS = 1e-6
ROPE_BASE = 10000.0
NEG_BIG = -1e30
LOG2_E = 1.4426950408889634

LANES = 128
SUBLANES = 8
MXU_TILE = 256
OUT_CHUNK = 2 * MXU_TILE
VMEM_LIMIT_BYTES = 56 * 2**20
VMEM_BLOCK_BUDGET = 50 * 2**20

RET_HEADS = 8
RET_CHUNK = 256
MLA_HEADS = 16
MLA_NOPE = 128
MLA_ROPE = 64
MLA_V = 128
MLA_HEAD_PAD = 256
ONES_ROWS = 16
ATTN_SLABS = 2
ATTN_LOOP_TILES = 8


def _params(*semantics):
    return pltpu.CompilerParams(dimension_semantics=semantics,
                                vmem_limit_bytes=VMEM_LIMIT_BYTES)


def _silu(x):
    h = 0.5 * x
    return h + h * jnp.tanh(h)


def _mod_kernel(c_ref, w_ref, b_ref, o_ref):
    act = _silu(c_ref[...]).astype(BF16)
    o_ref[...] = jnp.dot(act, w_ref[...].astype(BF16),
                         preferred_element_type=F32) + b_ref[...]


def _modulation(c, ada_w, ada_b):
    depth, d, n3 = ada_w.shape
    b = c.shape[0]
    assert b <= SUBLANES
    tn = 768
    assert n3 % tn == 0
    c_pad = jnp.zeros((SUBLANES, d), F32).at[:b].set(c)
    out = pl.pallas_call(
        _mod_kernel,
        grid=(depth, n3 // tn),
        in_specs=[
            pl.BlockSpec((SUBLANES, d), lambda i, n: (0, 0)),
            pl.BlockSpec((None, d, tn), lambda i, n: (i, 0, n)),
            pl.BlockSpec((None, 1, tn), lambda i, n: (i, 0, n)),
        ],
        out_specs=pl.BlockSpec((None, SUBLANES, tn), lambda i, n: (i, 0, n)),
        out_shape=jax.ShapeDtypeStruct((depth, SUBLANES, n3), F32),
        compiler_params=_params("parallel", "parallel"),
        name="adaln_modulation",
    )(c_pad, ada_w, ada_b.reshape(depth, 1, n3))
    return out[:, :b].reshape(depth, b, 1, n3)


def _rope_table_kernel(pos_ref, freq_ref, cos_ref, sin_ref):
    ang = pos_ref[...] * freq_ref[...]
    cos_ref[...] = jnp.cos(ang)
    sin_ref[...] = jnp.sin(ang)


def _rope_tables(pos_lanes, freq_lanes):
    rows = pos_lanes.shape[0]
    tr = min(rows, 1024)
    assert rows % tr == 0
    spec = pl.BlockSpec((tr, LANES), lambda i: (i, 0))
    return pl.pallas_call(
        _rope_table_kernel,
        grid=(rows // tr,),
        in_specs=[spec, pl.BlockSpec((1, LANES), lambda i: (0, 0))],
        out_specs=[spec, spec],
        out_shape=[jax.ShapeDtypeStruct((rows, LANES), F32)] * 2,
        compiler_params=_params("parallel"),
        name="rope_tables",
    )(pos_lanes, freq_lanes)


def _rope_table_t_kernel(pos_ref, freq_ref, cos_ref, sin_ref):
    ang = freq_ref[...] * pos_ref[...]
    n = ang.shape[0]
    cos = jnp.cos(ang)
    sin = jnp.sin(ang)
    cos_ref[:n, :] = cos
    cos_ref[n:, :] = cos
    sin_ref[:n, :] = sin
    sin_ref[n:, :] = sin


def _rope_tables_t(pos_row, freq_col):
    t = pos_row.shape[1]
    n = freq_col.shape[0]
    tc = min(t, 4096)
    assert t % tc == 0
    out_spec = pl.BlockSpec((2 * n, tc), lambda i: (0, i))
    return pl.pallas_call(
        _rope_table_t_kernel,
        grid=(t // tc,),
        in_specs=[pl.BlockSpec((1, tc), lambda i: (0, i)),
                  pl.BlockSpec((n, 1), lambda i: (0, 0))],
        out_specs=[out_spec, out_spec],
        out_shape=[jax.ShapeDtypeStruct((2 * n, t), F32)] * 2,
        compiler_params=_params("parallel"),
        name="rope_tables_t",
    )(pos_row, freq_col)


def _inv_freq(d):
    return ROPE_BASE ** (-jnp.arange(0, d, 2, dtype=F32) / d)


def _adaln_rmsnorm(x_ref, g_ref, shift_ref, scale_ref, rows=slice(None)):
    x = x_ref[rows, :]
    y = x * lax.rsqrt(jnp.mean(x * x, axis=-1, keepdims=True) + EPS)
    return (y * (g_ref[...] * (1.0 + scale_ref[...])) + shift_ref[...]).astype(BF16)


def _mod_specs(layer, rows_per_batch_block, d):
    def spec(part):
        return pl.BlockSpec((None, None, 1, d),
                            lambda i, *_: (layer, i // rows_per_batch_block, 0, part))
    return spec


def _resident_spec(shape):
    return pl.BlockSpec(shape, lambda *_: (0,) * len(shape), pipeline_mode=pl.Buffered(1))


def _ret_inproj_kernel(x_ref, g_ref, shift_ref, scale_ref, cos_ref, sin_ref, w_ref,
                       o_ref, h_ref, *, rope_blocks, gate_block0, head_dim):
    n = pl.program_id(1)
    half = head_dim // 2
    tm = o_ref.shape[0]
    chunks = range(0, o_ref.shape[1], head_dim)
    halves = range(0, tm, tm // 2)
    pieces = [(c0, r0) for c0 in chunks for r0 in halves]

    def project(c0, r0):
        return jnp.dot(h_ref[r0:r0 + tm // 2, :], w_ref[:, c0:c0 + head_dim],
                       preferred_element_type=F32)

    def rope_piece(c0, r0):
        rows = slice(r0, r0 + tm // 2)
        acc = project(c0, r0)
        x1 = acc[:, :half]
        x2 = acc[:, half:]
        cos = cos_ref[rows, :]
        sin = sin_ref[rows, :]
        o_ref[rows, c0:c0 + half] = (x1 * cos - x2 * sin).astype(BF16)
        o_ref[rows, c0 + half:c0 + head_dim] = (x2 * cos + x1 * sin).astype(BF16)

    @pl.when(n == 0)
    def _():
        for r0 in halves:
            rows = slice(r0, r0 + tm // 2)
            h_ref[rows, :] = _adaln_rmsnorm(x_ref, g_ref, shift_ref, scale_ref, rows)
            for c0 in chunks:
                rope_piece(c0, r0)

    @pl.when((n > 0) & (n < rope_blocks))
    def _():
        for c0, r0 in pieces:
            rope_piece(c0, r0)

    @pl.when((n >= rope_blocks) & (n < gate_block0))
    def _():
        for c0, r0 in pieces:
            o_ref[r0:r0 + tm // 2, c0:c0 + head_dim] = project(c0, r0).astype(BF16)

    @pl.when(n >= gate_block0)
    def _():
        for c0, r0 in pieces:
            o_ref[r0:r0 + tm // 2, c0:c0 + head_dim] = _silu(project(c0, r0)).astype(BF16)


def _ret_inproj(x2d, norm_g, mod, layer, cos, sin, w_bf16, seq, rope_cols, gate_col, head_dim):
    t, d = x2d.shape
    n_out = w_bf16.shape[1]
    tm = min(seq, 1024)
    tn = 2048
    assert seq % tm == 0 and n_out % tn == 0 and rope_cols % tn == 0 and tn % head_dim == 0
    assert head_dim // 2 == LANES and gate_col % tn == 0
    mspec = _mod_specs(layer, seq // tm, d)
    return pl.pallas_call(
        functools.partial(_ret_inproj_kernel, rope_blocks=rope_cols // tn,
                          gate_block0=gate_col // tn, head_dim=head_dim),
        grid=(t // tm, n_out // tn),
        in_specs=[
            pl.BlockSpec((tm, d), lambda i, n: (i, 0)),
            pl.BlockSpec((1, d), lambda i, n: (0, 0)),
            mspec(0), mspec(1),
            pl.BlockSpec((tm, LANES), lambda i, n: (i, 0)),
            pl.BlockSpec((tm, LANES), lambda i, n: (i, 0)),
            pl.BlockSpec((d, tn), lambda i, n: (0, n)),
        ],
        out_specs=pl.BlockSpec((tm, tn), lambda i, n: (i, n)),
        out_shape=jax.ShapeDtypeStruct((t, n_out), BF16),
        scratch_shapes=[pltpu.VMEM((tm, d), BF16)],
        compiler_params=_params("parallel", "arbitrary"),
        name="ret_in_proj",
    )(x2d, norm_g.reshape(1, d), mod, mod, cos, sin, w_bf16)


def _retention_kernel(decay_ref, q_ref, k_ref, v_ref, sg_ref, gn_ref, dmat_ref, xi_ref,
                      zeta_ref, o_ref, state_ref, *, chunk):
    @pl.when(pl.program_id(2) == 0)
    def _():
        state_ref[...] = jnp.zeros_like(state_ref)

    decay = decay_ref[pl.program_id(1)]
    for c in range(q_ref.shape[0] // chunk):
        rows = slice(c * chunk, (c + 1) * chunk)
        q = q_ref[rows, :]
        k = k_ref[rows, :]
        v = v_ref[rows, :]
        s = lax.dot_general(q, k, (((1,), (1,)), ((), ())), preferred_element_type=F32)
        kz = (k.astype(F32) * zeta_ref[...]).astype(BF16)
        update = lax.dot_general(kz, v, (((0,), (0,)), ((), ())), preferred_element_type=F32)
        s = (s * dmat_ref[...]).astype(BF16)
        state = state_ref[...]
        y = (jnp.dot(s, v, preferred_element_type=F32)
             + jnp.dot(q, state.astype(BF16), preferred_element_type=F32) * xi_ref[...])
        state_ref[...] = state * decay + update
        yc = y - jnp.mean(y, axis=-1, keepdims=True)
        var = jnp.mean(yc * yc, axis=-1, keepdims=True)
        yn = yc * lax.rsqrt(var + EPS) * gn_ref[...]
        o_ref[rows, :] = sg_ref[rows, :] * yn.astype(BF16)


def _retention_constants(heads, chunk, dk, dv):
    f32 = np.float32
    log_gamma = np.log1p(-np.exp2(f32(-5.0) - np.arange(heads, dtype=f32))).astype(f32)
    idx = np.arange(chunk, dtype=f32)
    diff = idx[:, None] - idx[None, :]
    k_scale = f32(dk ** -0.5)
    d_intra = np.where(diff >= 0, np.exp(log_gamma[:, None, None] * np.maximum(diff, 0)),
                       0).astype(f32) * k_scale
    xi = np.exp(log_gamma[:, None] * (idx + 1)).astype(f32)
    zeta = np.exp(log_gamma[:, None] * (chunk - 1 - idx)).astype(f32) * k_scale
    chunk_decay = np.exp(log_gamma * chunk).astype(f32)
    bcast = lambda a, n: np.ascontiguousarray(np.broadcast_to(a[:, :, None], (heads, chunk, n)))
    return d_intra, bcast(xi, dv), bcast(zeta, dk), chunk_decay


def _retention(proj, gn_g, batch, seq, heads, dk, dv):
    t = proj.shape[0]
    width = heads * dv
    chunk = RET_CHUNK
    sb = min(seq, 2048)
    assert seq % sb == 0 and sb % chunk == 0
    nsb = seq // sb
    d_intra, xi, zeta, chunk_decay = _retention_constants(heads, chunk, dk, dv)
    k_blk0 = heads
    v_blk0 = 2 * heads * dk // dv
    g_blk0 = v_blk0 + heads
    row = lambda b, h, s: b * nsb + s
    return pl.pallas_call(
        functools.partial(_retention_kernel, chunk=chunk),
        grid=(batch, heads, nsb),
        in_specs=[
            pl.BlockSpec(memory_space=pltpu.SMEM),
            pl.BlockSpec((sb, dk), lambda b, h, s: (row(b, h, s), h)),
            pl.BlockSpec((sb, dk), lambda b, h, s: (row(b, h, s), k_blk0 + h)),
            pl.BlockSpec((sb, dv), lambda b, h, s: (row(b, h, s), v_blk0 + h)),
            pl.BlockSpec((sb, dv), lambda b, h, s: (row(b, h, s), g_blk0 + h)),
            pl.BlockSpec((1, dv), lambda b, h, s: (0, h)),
            pl.BlockSpec((None, chunk, chunk), lambda b, h, s: (h, 0, 0)),
            pl.BlockSpec((None, chunk, dv), lambda b, h, s: (h, 0, 0)),
            pl.BlockSpec((None, chunk, dk), lambda b, h, s: (h, 0, 0)),
        ],
        out_specs=pl.BlockSpec((sb, dv), lambda b, h, s: (row(b, h, s), h)),
        out_shape=jax.ShapeDtypeStruct((t, width), BF16),
        scratch_shapes=[pltpu.VMEM((dk, dv), F32)],
        compiler_params=_params("parallel", "parallel", "arbitrary"),
        name="retention",
    )(chunk_decay, proj, proj, proj, proj, gn_g.reshape(1, width), d_intra, xi, zeta)


def _residual_chunks(a_ref, w_ref, x_ref, gate_ref, o_ref, rows, want_ssq):
    ssq = None
    for c0 in range(0, o_ref.shape[1], OUT_CHUNK):
        cols = slice(c0, c0 + OUT_CHUNK)
        y = jnp.dot(a_ref[rows, :], w_ref[:, cols], preferred_element_type=F32)
        z = x_ref[rows, cols] + gate_ref[:, cols] * y
        o_ref[rows, cols] = z
        if want_ssq:
            part = jnp.sum(z * z, axis=-1, keepdims=True)
            ssq = part if ssq is None else ssq + part
    return ssq


def _row_halves(ref):
    half = ref.shape[0] // 2
    return (slice(0, half), slice(half, 2 * half))


def _outproj_kernel(a_ref, w_ref, x_ref, gate_ref, o_ref):
    _residual_chunks(a_ref, w_ref, x_ref, gate_ref, o_ref, slice(None), False)


def _outproj_final_kernel(a_ref, w_ref, x_ref, gate_ref, fg_ref, o_ref):
    ssq = _residual_chunks(a_ref, w_ref, x_ref, gate_ref, o_ref, slice(None), True)
    inv = lax.rsqrt(ssq * (1.0 / o_ref.shape[1]) + EPS)
    o_ref[...] = o_ref[...] * inv * fg_ref[...]


def _outproj(a, w_bf16, x2d, mod, layer, seq, final_g=None):
    t, k = a.shape
    d = w_bf16.shape[1]

    def block_bytes(rows):
        return 2 * rows * k * 2 + 2 * 2 * rows * d * 4 + k * d * 2

    tm = min(seq, 1024 if block_bytes(1024) <= VMEM_BLOCK_BUDGET else 512)
    assert seq % tm == 0 and d % OUT_CHUNK == 0
    in_specs = [
        pl.BlockSpec((tm, k), lambda i: (i, 0)),
        _resident_spec(w_bf16.shape),
        pl.BlockSpec((tm, d), lambda i: (i, 0)),
        _mod_specs(layer, seq // tm, d)(2),
    ]
    args = [a, w_bf16, x2d, mod]
    kern = _outproj_kernel
    if final_g is not None:
        in_specs.append(pl.BlockSpec((1, d), lambda i: (0, 0)))
        args.append(final_g.reshape(1, d))
        kern = _outproj_final_kernel
    return pl.pallas_call(
        kern,
        grid=(t // tm,),
        in_specs=in_specs,
        out_specs=pl.BlockSpec((tm, d), lambda i: (i, 0)),
        out_shape=jax.ShapeDtypeStruct((t, d), F32),
        compiler_params=_params("parallel"),
        name="out_proj_final" if final_g is not None else "out_proj",
    )(*args)


def _mla_inproj_kernel(x_ref, g_ref, shift_ref, scale_ref, w_ref, o_ref, h_ref, *, gate_cols):
    for rows in _row_halves(o_ref):
        h_ref[rows, :] = _adaln_rmsnorm(x_ref, g_ref, shift_ref, scale_ref, rows)
        for c0 in range(0, o_ref.shape[1], MXU_TILE):
            acc = jnp.dot(h_ref[rows, :], w_ref[:, c0:c0 + MXU_TILE],
                          preferred_element_type=F32)
            if gate_cols[0] <= c0 < gate_cols[1]:
                acc = _silu(acc)
            o_ref[rows, c0:c0 + MXU_TILE] = acc.astype(BF16)


def _mla_inproj(x2d, norm_g, mod, layer, w_bf16, seq, gate_cols):
    t, d = x2d.shape
    n_out = w_bf16.shape[1]
    tm = min(seq, 1024)
    assert seq % tm == 0 and n_out % MXU_TILE == 0
    assert gate_cols[0] % MXU_TILE == 0 and gate_cols[1] % MXU_TILE == 0
    mspec = _mod_specs(layer, seq // tm, d)
    return pl.pallas_call(
        functools.partial(_mla_inproj_kernel, gate_cols=gate_cols),
        grid=(t // tm,),
        in_specs=[
            pl.BlockSpec((tm, d), lambda i: (i, 0)),
            pl.BlockSpec((1, d), lambda i: (0, 0)),
            mspec(0), mspec(1),
            _resident_spec(w_bf16.shape),
        ],
        out_specs=pl.BlockSpec((tm, n_out), lambda i: (i, 0)),
        out_shape=jax.ShapeDtypeStruct((t, n_out), BF16),
        scratch_shapes=[pltpu.VMEM((tm, d), BF16)],
        compiler_params=_params("parallel"),
        name="mla_in_proj",
    )(x2d, norm_g.reshape(1, d), mod, mod, w_bf16)


def _mla_up_kernel(cq_ref, ckv_ref, kr_ref, qg_ref, kvg_ref, wqt_ref, wk_ref, wvt_ref, cost_ref,
                   sint_ref, qt_ref, k_ref, vt_ref, *, heads, q_scale, group):
    def rms(x_ref, g_ref):
        x = x_ref[...].astype(F32)
        return x * lax.rsqrt(jnp.mean(x * x, axis=-1, keepdims=True) + EPS) * g_ref[...]

    cos_t = cost_ref[...]
    sin_t = sint_ref[...]

    ckvn_f32 = rms(ckv_ref, kvg_ref)
    ckvn = ckvn_f32.astype(BF16)
    kr_t = kr_ref[...].astype(F32).T
    kr_rope_t = kr_t[:MLA_ROPE, :] * cos_t + kr_t[MLA_ROPE:, :] * sin_t
    kr = jnp.concatenate([kr_rope_t, jnp.zeros_like(kr_rope_t)], axis=0).T.astype(BF16)
    for g0 in range(0, heads, group):
        kn = jnp.dot(ckvn, wk_ref[:, g0 * MLA_NOPE:(g0 + group) * MLA_NOPE],
                     preferred_element_type=F32)
        for j in range(group):
            o0 = (g0 + j) * MLA_HEAD_PAD
            k_ref[:, o0:o0 + LANES] = kn[:, j * MLA_NOPE:(j + 1) * MLA_NOPE].astype(BF16)
            k_ref[:, o0 + LANES:o0 + 2 * LANES] = kr

    cqn_t = rms(cq_ref, qg_ref).T.astype(BF16)
    gw = group * MLA_HEAD_PAD
    for g0 in range(0, heads, group):
        qt = jnp.dot(wqt_ref[g0 * MLA_HEAD_PAD:g0 * MLA_HEAD_PAD + gw, :], cqn_t,
                     preferred_element_type=F32)
        for j in range(group):
            r0 = j * MLA_HEAD_PAD
            o0 = (g0 + j) * MLA_HEAD_PAD
            r1 = r0 + MLA_NOPE
            qt_ref[o0:o0 + MLA_NOPE, :] = (qt[r0:r1, :] * q_scale).astype(BF16)
            rope = (qt[r1:r1 + MLA_ROPE, :] * cos_t
                    + qt[r1 + MLA_ROPE:r1 + 2 * MLA_ROPE, :] * sin_t)
            qt_ref[o0 + MLA_NOPE:o0 + MLA_NOPE + MLA_ROPE, :] = (rope * q_scale).astype(BF16)
            qt_ref[o0 + MLA_NOPE + MLA_ROPE:o0 + MLA_HEAD_PAD, :] = jnp.zeros(
                (MLA_HEAD_PAD - MLA_NOPE - MLA_ROPE, qt.shape[1]), BF16)

    ckvn_t = ckvn_f32.T.astype(BF16)
    gv = group * MLA_V
    for r0 in range(0, heads * MLA_V, gv):
        vt_ref[r0:r0 + gv, :] = jnp.dot(wvt_ref[r0:r0 + gv, :], ckvn_t,
                                        preferred_element_type=F32).astype(BF16)


def _mla_up(proj, q_norm_g, kv_norm_g, wqt_bf16, wk_bf16, wvt_bf16, cos_t, sin_t, q_rank,
            kv_rank, kr_col, heads, tm):
    t = proj.shape[0]
    assert t % tm == 0 and q_rank == kv_rank and kr_col % LANES == 0
    return pl.pallas_call(
        functools.partial(_mla_up_kernel, heads=heads,
                          q_scale=LOG2_E * (MLA_NOPE + MLA_ROPE) ** -0.5, group=4),
        grid=(t // tm,),
        in_specs=[
            pl.BlockSpec((tm, q_rank), lambda i: (i, 0)),
            pl.BlockSpec((tm, kv_rank), lambda i: (i, 1)),
            pl.BlockSpec((tm, LANES), lambda i: (i, kr_col // LANES)),
            pl.BlockSpec((1, q_rank), lambda i: (0, 0)),
            pl.BlockSpec((1, kv_rank), lambda i: (0, 0)),
            _resident_spec(wqt_bf16.shape),
            _resident_spec(wk_bf16.shape),
            _resident_spec(wvt_bf16.shape),
            pl.BlockSpec((MLA_ROPE, tm), lambda i: (0, i)),
            pl.BlockSpec((MLA_ROPE, tm), lambda i: (0, i)),
        ],
        out_specs=[
            pl.BlockSpec((None, heads * MLA_HEAD_PAD, tm), lambda i: (i, 0, 0)),
            pl.BlockSpec((tm, heads * MLA_HEAD_PAD), lambda i: (i, 0)),
            pl.BlockSpec((None, heads * MLA_V, tm), lambda i: (i, 0, 0)),
        ],
        out_shape=[
            jax.ShapeDtypeStruct((t // tm, heads * MLA_HEAD_PAD, tm), BF16),
            jax.ShapeDtypeStruct((t, heads * MLA_HEAD_PAD), BF16),
            jax.ShapeDtypeStruct((t // tm, heads * MLA_V, tm), BF16),
        ],
        compiler_params=_params("parallel"),
        name="mla_up_proj",
    )(proj, proj, proj, q_norm_g.reshape(1, q_rank), kv_norm_g.reshape(1, kv_rank),
      wqt_bf16, wk_bf16, wvt_bf16, cos_t, sin_t)


def _attn_kernel(qt_ref, k_ref, vt_ref, sg_ref, o_ref, vt1_ref, s_ref, smax_ref, acc_ref, m_ref,
                 *, tk):
    nslab = ATTN_SLABS
    tq = nslab * tk
    nq = qt_ref.shape[0] // nslab
    all_q = slice(0, tq)

    def slabs(a, b):
        return slice(a * tk, b * tk)

    for j in range(vt_ref.shape[0]):
        vt1_ref[j, :MLA_V, :] = vt_ref[j]
        vt1_ref[j, MLA_V:, :] = jnp.ones((ONES_ROWS, tk), BF16)

    def put_scores(slot, t, qi, first_slab=0):
        kt = k_ref[pl.ds(pl.multiple_of(t * tk, tk), tk), :]
        for j in range(first_slab, nslab):
            st = jnp.dot(kt, qt_ref[nslab * qi + j], preferred_element_type=F32)
            s_ref[slot, :, slabs(j, j + 1)] = st
            smax_ref[slot, :, slabs(j, j + 1)] = jnp.max(st, axis=0, keepdims=True)

    def consume(t, slot, qs, diagonal):
        if diagonal:
            half = (qs.stop - qs.start) // 2
            consume_keys(t, slot, slice(qs.start, qs.start + half), half, 0)
            consume_keys(t, slot, slice(qs.start + half, qs.stop), 2 * half, half)
        else:
            consume_keys(t, slot, qs, tk, None)

    def consume_keys(t, slot, qs, nkeys, first_query):
        st = s_ref[slot, :nkeys, qs]
        if first_query is None:
            tile_max = smax_ref[slot, :, qs]
        else:
            key = lax.broadcasted_iota(jnp.int32, st.shape, 0)
            qry = lax.broadcasted_iota(jnp.int32, st.shape, 1) + first_query
            st = jnp.where(key <= qry, st, NEG_BIG)
            tile_max = jnp.max(st, axis=0, keepdims=True)
        m_old = m_ref[:, qs]
        m_new = jnp.maximum(m_old, tile_max)
        p = jnp.exp2(st - m_new).astype(BF16)
        acc_ref[:, qs] = jnp.exp2(m_old - m_new) * acc_ref[:, qs] + jnp.dot(
            vt1_ref[t, :, :nkeys], p, preferred_element_type=F32)
        m_ref[:, qs] = m_new

    def finish(q0, qs):
        acc = acc_ref[:, qs]
        o = (acc[:MLA_V, :] * (1.0 / acc[MLA_V:MLA_V + 1, :])).T
        rows = pl.ds(pl.multiple_of(q0 + qs.start, tk), qs.stop - qs.start)
        o_ref[rows, :] = (sg_ref[rows, :].astype(F32) * o).astype(BF16)

    put_scores(0, 0, 0)

    def query_block(qi, carry):
        q0 = qi * tq
        m_ref[...] = jnp.full(m_ref.shape, NEG_BIG, F32)
        acc_ref[...] = jnp.zeros_like(acc_ref)

        def full_tiles(t0, n):
            for i in range(n):
                put_scores((i + 1) % 2, t0 + i + 1, qi)
                consume(t0 + i, i % 2, all_q, False)

        def loop_body(u, c):
            full_tiles(ATTN_LOOP_TILES * u, ATTN_LOOP_TILES)
            return c

        base = nslab * qi
        lax.fori_loop(0, base // ATTN_LOOP_TILES, loop_body, 0)
        size = ATTN_LOOP_TILES // 2
        while size >= nslab:
            rem = base % (2 * size)

            @pl.when(rem >= size)
            def _(rem=rem, size=size):
                full_tiles(base - rem, size)

            size //= 2

        for j in range(nslab):
            last = j == nslab - 1
            if not last:
                put_scores((j + 1) % 2, base + j + 1, qi, first_slab=j + 1)
            consume(base + j, j % 2, slabs(j, j + 1), True)
            if last:
                put_scores(0, 0, jnp.minimum(qi + 1, nq - 1))
            else:
                consume(base + j, j % 2, slabs(j + 1, nslab), False)
            finish(q0, slabs(j, j + 1))
        return carry

    lax.fori_loop(0, nq, query_block, 0)


def _attention(qt, k, vt, proj, gate_col, batch, seq, heads, tk):
    t = k.shape[0]
    tq = ATTN_SLABS * tk
    assert seq % tq == 0 and gate_col % MLA_V == 0 and tk % LANES == 0 and ATTN_SLABS % 2 == 0
    assert ATTN_LOOP_TILES in (ATTN_SLABS, 2 * ATTN_SLABS, 4 * ATTN_SLABS)
    g_blk0 = gate_col // MLA_V
    return pl.pallas_call(
        functools.partial(_attn_kernel, tk=tk),
        grid=(batch, heads),
        in_specs=[
            pl.BlockSpec((seq // tk, MLA_HEAD_PAD, tk), lambda b, h: (b, h, 0)),
            pl.BlockSpec((seq, MLA_HEAD_PAD), lambda b, h: (b, h)),
            pl.BlockSpec((seq // tk, MLA_V, tk), lambda b, h: (b, h, 0)),
            pl.BlockSpec((seq, MLA_V), lambda b, h: (b, g_blk0 + h)),
        ],
        out_specs=pl.BlockSpec((seq, MLA_V), lambda b, h: (b, h)),
        out_shape=jax.ShapeDtypeStruct((t, heads * MLA_V), BF16),
        scratch_shapes=[
            pltpu.VMEM((seq // tk, MLA_V + ONES_ROWS, tk), BF16),
            pltpu.VMEM((2, tk, tq), F32),
            pltpu.VMEM((2, 1, tq), F32),
            pltpu.VMEM((MLA_V + ONES_ROWS, tq), F32),
            pltpu.VMEM((1, tq), F32),
        ],
        compiler_params=_params("parallel", "parallel"),
        name="mla_attention",
    )(qt, k, vt, proj)


def _rotate_half_cols(w):
    half = w.shape[-1] // 2
    return jnp.concatenate([-w[..., half:], w[..., :half]], axis=-1)


def _mla_weights(w_in, w_uq, w_ukv, q_rank, kv_rank, heads):
    w_in, w_uq, w_ukv = (w.astype(BF16) for w in (w_in, w_uq, w_ukv))
    d = w_in.shape[0]
    cq = w_in[:, :q_rank]
    ckv = w_in[:, q_rank:q_rank + kv_rank]
    kr = w_in[:, q_rank + kv_rank:q_rank + kv_rank + MLA_ROPE]
    gate = w_in[:, q_rank + kv_rank + MLA_ROPE:]
    gate_col = q_rank + kv_rank
    kr_col = gate_col + gate.shape[1]
    pad = jnp.zeros((d, -(kr_col + 2 * MLA_ROPE) % MXU_TILE), w_in.dtype)
    w_in_p = jnp.concatenate([cq, ckv, gate, kr, _rotate_half_cols(kr), pad], axis=1)
    uq = w_uq.reshape(q_rank, heads, MLA_NOPE + MLA_ROPE)
    rope = uq[..., MLA_NOPE:]
    uq_p = jnp.concatenate([uq[..., :MLA_NOPE], rope, _rotate_half_cols(rope)], axis=-1)
    uq_p = uq_p.reshape(q_rank, heads * MLA_HEAD_PAD).T
    ukv = w_ukv.reshape(kv_rank, heads, MLA_NOPE + MLA_V)
    uk_p = ukv[..., :MLA_NOPE].reshape(kv_rank, heads * MLA_NOPE)
    uv_p = ukv[..., MLA_NOPE:].reshape(kv_rank, heads * MLA_V).T
    return w_in_p, uq_p, uk_p, uv_p, gate_col, kr_col


def kernel(x, c, positions, ada_w, ada_b, norm_g, ret_w_in, ret_gn_g, ret_w_out, mla_w_in,
           mla_q_norm_g, mla_w_uq, mla_kv_norm_g, mla_w_ukv, mla_w_out, final_norm_g):
    batch, seq, d = x.shape
    t = batch * seq
    depth = ada_w.shape[0]
    assert depth % 2 == 0, "the last layer must be a latent-attention layer (final norm fusion)"
    ret_dk = d // RET_HEADS
    ret_dv = ret_w_out.shape[1] // RET_HEADS
    q_rank = mla_q_norm_g.shape[1]
    kv_rank = mla_kv_norm_g.shape[1]

    mod = _modulation(c, ada_w, ada_b)

    pos = positions.reshape(t, 1).astype(F32)
    cos_r, sin_r = _rope_tables(jnp.broadcast_to(pos, (t, LANES)),
                                _inv_freq(ret_dk).reshape(1, LANES))
    cos_t, sin_t = _rope_tables_t(positions.reshape(1, t).astype(F32),
                                  _inv_freq(MLA_ROPE).reshape(MLA_ROPE // 2, 1))
    attn_tk = min(seq // ATTN_SLABS, 512)

    x2d = x.reshape(t, d)
    for layer in range(depth):
        j = layer // 2
        if layer % 2 == 0:
            proj = _ret_inproj(x2d, norm_g[layer], mod, layer, cos_r, sin_r,
                               ret_w_in[j].astype(BF16), seq, 2 * d,
                               2 * d + RET_HEADS * ret_dv, ret_dk)
            y = _retention(proj, ret_gn_g[j], batch, seq, RET_HEADS, ret_dk, ret_dv)
            x2d = _outproj(y, ret_w_out[j].astype(BF16), x2d, mod, layer, seq)
        else:
            w_in_p, uq_p, uk_p, uv_p, gate_col, kr_col = _mla_weights(
                mla_w_in[j], mla_w_uq[j], mla_w_ukv[j], q_rank, kv_rank, MLA_HEADS)
            proj = _mla_inproj(x2d, norm_g[layer], mod, layer, w_in_p, seq,
                               (gate_col, kr_col))
            qt, k, vt = _mla_up(proj, mla_q_norm_g[j], mla_kv_norm_g[j], uq_p, uk_p, uv_p,
                                cos_t, sin_t, q_rank, kv_rank, kr_col, MLA_HEADS, attn_tk)
            o = _attention(qt, k, vt, proj, gate_col, batch, seq, MLA_HEADS, attn_tk)
            final_g = final_norm_g if layer == depth - 1 else None
            x2d = _outproj(o, mla_w_out[j].astype(BF16), x2d, mod, layer, seq, final_g)
    return x2d.reshape(batch, seq, d)
```
